```python
import math
import jax, jax.numpy as jnp
from jax import lax
import numpy as np

D_MODEL = 2048
BATCH = 2
SEQ = 16384
DEPTH = 2

CHUNK = 64
Q_BLOCK = 128
N_A_LAYERS = DEPTH // 2
N_B_LAYERS = DEPTH - N_A_LAYERS
HEAD_DIM = 128
DIFF_HEADS = D_MODEL // (2 * HEAD_DIM)
DIFF_VDIM = 2 * HEAD_DIM
FOX_HEADS = D_MODEL // HEAD_DIM
D_FF = 4 * D_MODEL
REL_BUCKETS = 32
REL_MAX_DIST = 128
NORM_EPS = 1e-6
SUBLN_EPS = 1e-5
NEG = -1e30

kernel_name = "yoco_diffattn_fox_hybrid"


def rmsnorm(x, g, eps=NORM_EPS):
    xf = x.astype(jnp.float32)
    y = xf * lax.rsqrt(jnp.mean(xf * xf, axis=-1, keepdims=True) + eps)
    return (y * g.astype(jnp.float32)).astype(x.dtype)


def t5_bucket(rel):
    half = REL_BUCKETS // 2
    max_exact = half // 2
    ret = jnp.where(rel > 0, half, 0)
    n = jnp.abs(rel)
    nf = jnp.maximum(n, 1).astype(jnp.float32)
    large = max_exact + (jnp.log(nf / max_exact) / math.log(REL_MAX_DIST / max_exact)
                         * (half - max_exact)).astype(jnp.int32)
    large = jnp.minimum(large, half - 1)
    return ret + jnp.where(n < max_exact, n, large)


def diff_attention(h, w_qkv, lq1, lk1, lq2, lk2, subln_g, rel_table, lambda_init):
    B, S, _ = h.shape
    nb = S // Q_BLOCK
    q, k, v = jnp.split(h @ w_qkv, 3, axis=-1)
    q = q.reshape(B, nb, Q_BLOCK, DIFF_HEADS, 2, HEAD_DIM).transpose(1, 0, 3, 4, 2, 5)
    k = k.reshape(B, S, DIFF_HEADS, 2, HEAD_DIM).transpose(0, 2, 3, 1, 4)
    v = v.reshape(B, S, DIFF_HEADS, DIFF_VDIM).transpose(0, 2, 1, 3)
    f32 = jnp.float32
    lam = (jnp.exp(jnp.sum(lq1.astype(f32) * lk1.astype(f32)))
           - jnp.exp(jnp.sum(lq2.astype(f32) * lk2.astype(f32))) + lambda_init)
    key_pos = jnp.arange(S, dtype=jnp.int32)
    scale = HEAD_DIM ** -0.5

    def block(args):
        qi, bi = args
        q_pos = bi * Q_BLOCK + jnp.arange(Q_BLOCK, dtype=jnp.int32)
        s = jnp.einsum('bhcqd,bhckd->bhcqk', qi, k).astype(f32) * scale
        rel = key_pos[None, :] - q_pos[:, None]
        bias = jnp.transpose(rel_table[t5_bucket(rel)], (2, 0, 1)).astype(f32)
        mask = (key_pos[None, :] // CHUNK) <= (q_pos[:, None] // CHUNK)
        s = jnp.where(mask, s + bias[None, :, None], NEG)
        p = jax.nn.softmax(s, axis=-1)
        a = p[:, :, 0] - lam * p[:, :, 1]
        return jnp.einsum('bhqk,bhkv->bhqv', a.astype(v.dtype), v)

    o = lax.map(block, (q, jnp.arange(nb, dtype=jnp.int32)))
    o = rmsnorm(o, subln_g, SUBLN_EPS) * (1.0 - lambda_init)
    return o.transpose(1, 0, 3, 2, 4).reshape(B, S, D_MODEL)


def shared_kv(h, g, w_k, w_v, w_f, b_f):
    B, S, _ = h.shape
    hn = rmsnorm(h, g)
    k = (hn @ w_k).reshape(B, S, FOX_HEADS, HEAD_DIM).transpose(0, 2, 1, 3)
    v = (hn @ w_v).reshape(B, S, FOX_HEADS, HEAD_DIM).transpose(0, 2, 1, 3)
    log_f = jax.nn.log_sigmoid((hn @ w_f + b_f).astype(jnp.float32))
    c = jnp.cumsum(log_f, axis=1).transpose(0, 2, 1)
    return k, v, c


def forgetting_attention(h, w_q, k, v, c):
    B, S, _ = h.shape
    nb = S // Q_BLOCK
    q = (h @ w_q).reshape(B, nb, Q_BLOCK, FOX_HEADS, HEAD_DIM).transpose(1, 0, 3, 2, 4)
    cq = c.reshape(B, FOX_HEADS, nb, Q_BLOCK).transpose(2, 0, 1, 3)
    key_pos = jnp.arange(S, dtype=jnp.int32)
    scale = HEAD_DIM ** -0.5

    def block(args):
        qi, ci, bi = args
        q_pos = bi * Q_BLOCK + jnp.arange(Q_BLOCK, dtype=jnp.int32)
        s = jnp.einsum('bhqd,bhkd->bhqk', qi, k).astype(jnp.float32) * scale
        mask = key_pos[None, :] <= q_pos[:, None]
        s = jnp.where(mask, s + ci[..., None] - c[:, :, None, :], NEG)
        p = jax.nn.softmax(s, axis=-1)
        return jnp.einsum('bhqk,bhkd->bhqd', p.astype(v.dtype), v)

    o = lax.map(block, (q, cq, jnp.arange(nb, dtype=jnp.int32)))
    return o.transpose(1, 0, 3, 2, 4).reshape(B, S, D_MODEL)


def sq_relu_mlp(h, w_in, w_out):
    return jnp.square(jax.nn.relu(h @ w_in)) @ w_out


def setup_inputs(seed: int = 0) -> dict:
    key = jax.random.key(seed)
    ks = jax.random.split(key, 24)
    D = D_MODEL
    nrm = jax.random.normal
    f32 = jnp.float32
    return {
        "x": nrm(ks[0], (BATCH, SEQ, D), f32),
        "rel_bias_table": 0.5 * nrm(ks[1], (REL_BUCKETS, DIFF_HEADS), f32),
        "attn_norm_g": 1.0 + 0.02 * nrm(ks[2], (DEPTH, D), f32),
        "mlp_norm_g": 1.0 + 0.02 * nrm(ks[3], (DEPTH, D), f32),
        "w_qkv_a": nrm(ks[4], (N_A_LAYERS, D, 3 * D), f32) * D ** -0.5,
        "lam_q1": 0.1 * nrm(ks[5], (N_A_LAYERS, HEAD_DIM), f32),
        "lam_k1": 0.1 * nrm(ks[6], (N_A_LAYERS, HEAD_DIM), f32),
        "lam_q2": 0.1 * nrm(ks[7], (N_A_LAYERS, HEAD_DIM), f32),
        "lam_k2": 0.1 * nrm(ks[8], (N_A_LAYERS, HEAD_DIM), f32),
        "subln_g": 1.0 + 0.02 * nrm(ks[9], (N_A_LAYERS, DIFF_VDIM), f32),
        "w_o_a": nrm(ks[10], (N_A_LAYERS, D, D), f32) * D ** -0.5,
        "kv_norm_g": 1.0 + 0.02 * nrm(ks[11], (D,), f32),
        "w_k_b": nrm(ks[12], (D, D), f32) * D ** -0.5,
        "w_v_b": nrm(ks[13], (D, D), f32) * D ** -0.5,
        "w_f_b": nrm(ks[14], (D, FOX_HEADS), f32) * D ** -0.5,
        "b_f_b": 2.0 + 0.1 * nrm(ks[15], (FOX_HEADS,), f32),
        "w_q_b": nrm(ks[16], (N_B_LAYERS, D, D), f32) * D ** -0.5,
        "w_o_b": nrm(ks[17], (N_B_LAYERS, D, D), f32) * D ** -0.5,
        "w_mlp_in": nrm(ks[18], (DEPTH, D, D_FF), f32) * D ** -0.5,
        "w_mlp_out": nrm(ks[19], (DEPTH, D_FF, D), f32) * D_FF ** -0.5,
        "final_norm_g": 1.0 + 0.02 * nrm(ks[20], (D,), f32),
    }


def reference(x, rel_bias_table, attn_norm_g, mlp_norm_g, w_qkv_a, lam_q1, lam_k1,
              lam_q2, lam_k2, subln_g, w_o_a, kv_norm_g, w_k_b, w_v_b, w_f_b, b_f_b,
              w_q_b, w_o_b, w_mlp_in, w_mlp_out, final_norm_g):
    h = x
    k_sh = v_sh = c_sh = None
    for layer in range(DEPTH):
        hn = rmsnorm(h, attn_norm_g[layer])
        if layer < N_A_LAYERS:
            i = layer
            lambda_init = 0.8 - 0.6 * math.exp(-0.3 * layer)
            o = diff_attention(hn, w_qkv_a[i], lam_q1[i], lam_k1[i], lam_q2[i], lam_k2[i],
                               subln_g[i], rel_bias_table, lambda_init)
            h = h + o @ w_o_a[i]
        else:
            j = layer - N_A_LAYERS
            if j == 0:
                k_sh, v_sh, c_sh = shared_kv(h, kv_norm_g, w_k_b, w_v_b, w_f_b, b_f_b)
            o = forgetting_attention(hn, w_q_b[j], k_sh, v_sh, c_sh)
            h = h + o @ w_o_b[j]
        hn = rmsnorm(h, mlp_norm_g[layer])
        h = h + sq_relu_mlp(hn, w_mlp_in[layer], w_mlp_out[layer])
    return rmsnorm(h, final_norm_g)
```

```python
import functools
import math

import jax
import jax.numpy as jnp
from jax import lax
from jax.experimental import pallas as pl
from jax.experimental.pallas import tpu as pltpu

HEAD_DIM = 128
CHUNK = 64
REL_BUCKETS = 32
REL_MAX_DIST = 128
NORM_EPS = 1e-6
SUBLN_EPS = 1e-5
NEG = -1e30
SCALE = HEAD_DIM ** -0.5
EXP2_SCALE = SCALE * math.log2(math.e)
M_INIT = -1e30

V7X_VMEM_LIMIT_BYTES = 56 * 1024 * 1024
LANES = 128

F32 = jnp.float32
BF16 = jnp.bfloat16


def _params(*semantics):
    return pltpu.CompilerParams(dimension_semantics=semantics, vmem_limit_bytes=V7X_VMEM_LIMIT_BYTES)


def _rms_scale(x, eps):
    return lax.rsqrt(jnp.mean(x * x, axis=-1, keepdims=True) + eps)


def _dot(a, b):
    return jnp.dot(a, b, preferred_element_type=F32)


def _dot_nt(a, b):
    return lax.dot_general(a, b, (((1,), (1,)), ((), ())), preferred_element_type=F32)


def _norm_matmul_kernel(x_ref, g_ref, w_ref, o_ref, hn_ref):
    @pl.when(pl.program_id(1) == 0)
    def _():
        x = x_ref[...]
        hn_ref[...] = (x * _rms_scale(x, NORM_EPS) * g_ref[...]).astype(hn_ref.dtype)

    o_ref[...] = _dot(hn_ref[...], w_ref[...]).astype(o_ref.dtype)


def norm_matmul(x, g, w, *, tm, tn):
    n, d = x.shape
    nout = w.shape[1]
    return pl.pallas_call(
        _norm_matmul_kernel,
        out_shape=jax.ShapeDtypeStruct((n, nout), BF16),
        grid=(n // tm, nout // tn),
        in_specs=[
            pl.BlockSpec((tm, d), lambda i, j: (i, 0)),
            pl.BlockSpec((1, d), lambda i, j: (0, 0)),
            pl.BlockSpec((d, tn), lambda i, j: (0, j)),
        ],
        out_specs=pl.BlockSpec((tm, tn), lambda i, j: (i, j)),
        scratch_shapes=[pltpu.VMEM((tm, d), BF16)],
        compiler_params=_params("parallel", "arbitrary"),
        name="norm_matmul",
    )(x, g, w)


def _matmul_residual_kernel(a_ref, w_ref, r_ref, o_ref):
    o_ref[...] = r_ref[...] + _dot(a_ref[...], w_ref[...])


def matmul_residual(a, w, res, *, tm, tn):
    n, k = a.shape
    nout = w.shape[1]
    return pl.pallas_call(
        _matmul_residual_kernel,
        out_shape=jax.ShapeDtypeStruct((n, nout), F32),
        grid=(n // tm, nout // tn),
        in_specs=[
            pl.BlockSpec((tm, k), lambda i, j: (i, 0)),
            pl.BlockSpec((k, tn), lambda i, j: (0, j)),
            pl.BlockSpec((tm, tn), lambda i, j: (i, j)),
        ],
        out_specs=pl.BlockSpec((tm, tn), lambda i, j: (i, j)),
        compiler_params=_params("parallel", "arbitrary"),
        name="matmul_residual",
    )(a, w, res)


def _mlp_kernel(x_ref, g_ref, win_ref, wout_ref, gf_ref, o_ref, hn_ref, acc_ref, *, final_norm):
    f = pl.program_id(1)

    @pl.when(f == 0)
    def _():
        x = x_ref[...]
        hn_ref[...] = (x * _rms_scale(x, NORM_EPS) * g_ref[...]).astype(hn_ref.dtype)
        acc_ref[...] = jnp.zeros_like(acc_ref)

    u = _dot(hn_ref[...], win_ref[...])
    u = jnp.square(jnp.maximum(u, 0.0)).astype(BF16)
    acc_ref[...] += _dot(u, wout_ref[...])

    @pl.when(f == pl.num_programs(1) - 1)
    def _():
        y = x_ref[...] + acc_ref[...]
        if final_norm:
            y = y * _rms_scale(y, NORM_EPS) * gf_ref[...]
        o_ref[...] = y


def mlp_block(x, g, w_in, w_out, g_final, *, tm, tf, final_norm):
    n, d = x.shape
    dff = w_in.shape[1]
    return pl.pallas_call(
        functools.partial(_mlp_kernel, final_norm=final_norm),
        out_shape=jax.ShapeDtypeStruct((n, d), F32),
        grid=(n // tm, dff // tf),
        in_specs=[
            pl.BlockSpec((tm, d), lambda i, f: (i, 0)),
            pl.BlockSpec((1, d), lambda i, f: (0, 0)),
            pl.BlockSpec((d, tf), lambda i, f: (0, f)),
            pl.BlockSpec((tf, d), lambda i, f: (f, 0)),
            pl.BlockSpec((1, d), lambda i, f: (0, 0)),
        ],
        out_specs=pl.BlockSpec((tm, d), lambda i, f: (i, 0)),
        scratch_shapes=[pltpu.VMEM((tm, d), BF16), pltpu.VMEM((tm, d), F32)],
        compiler_params=_params("parallel", "arbitrary"),
        name="mlp_block",
    )(x, g, w_in, w_out, g_final)


def _t5_bias_kernel(table_ref, o_ref, *, t):
    h = pl.program_id(0)
    half = REL_BUCKETS // 2
    max_exact = half // 2
    qi = lax.broadcasted_iota(jnp.int32, (t, t), 0)
    ki = lax.broadcasted_iota(jnp.int32, (t, t), 1)
    far = table_ref[half - 1, h]
    for d in range(2):
        rel = ki - d * t - qi
        ret = jnp.where(rel > 0, half, 0)
        n = jnp.abs(rel)
        nf = jnp.maximum(n, 1).astype(F32)
        large = max_exact + (jnp.log(nf / max_exact) / math.log(REL_MAX_DIST / max_exact)
                             * (half - max_exact)).astype(jnp.int32)
        large = jnp.minimum(large, half - 1)
        bucket = ret + jnp.where(n < max_exact, n, large)
        val = jnp.zeros((t, t), F32)
        for b in range(REL_BUCKETS):
            val = jnp.where(bucket == b, table_ref[b, h], val)
        val = (val - far) * (1.0 / SCALE)
        if d == 0:
            mask = (ki >> 6) <= (qi >> 6)
            val = jnp.where(mask, val, NEG)
        o_ref[0, d] = val


def t5_bias_tiles(rel_table, *, t):
    nheads = rel_table.shape[1]
    assert CHUNK == 64 and t % CHUNK == 0 and t >= REL_MAX_DIST
    return pl.pallas_call(
        functools.partial(_t5_bias_kernel, t=t),
        out_shape=jax.ShapeDtypeStruct((nheads, 2, t, t), F32),
        grid=(nheads,),
        in_specs=[pl.BlockSpec(memory_space=pltpu.SMEM)],
        out_specs=pl.BlockSpec((1, 2, t, t), lambda h: (h, 0, 0, 0)),
        compiler_params=_params("arbitrary"),
        name="t5_bias_tiles",
    )(rel_table)


def _online_softmax_step(z, v, m_ref, l_ref, acc_ref):
    m_prev = m_ref[...]
    m_new = jnp.maximum(m_prev, jnp.max(z, axis=1, keepdims=True))
    p = jnp.exp2((z - m_new) * EXP2_SCALE)
    alpha = jnp.exp2((m_prev - m_new) * EXP2_SCALE)
    l_ref[...] = alpha * l_ref[...] + jnp.sum(p, axis=1, keepdims=True)
    acc_ref[...] = alpha * acc_ref[...] + _dot(p.astype(BF16), v)
    m_ref[...] = m_new


def _diff_attn_kernel(q_ref, k_ref, v_ref, bias_ref, lamv_ref, g_ref, o_ref,
                      m0, l0, a0, m1, l1, a1, *, t, lambda_init):
    i = pl.program_id(2)
    stats = ((m0, l0, a0), (m1, l1, a1))
    for m_ref, l_ref, a_ref in stats:
        m_ref[...] = jnp.full_like(m_ref, M_INIT)
        l_ref[...] = jnp.zeros_like(l_ref)
        a_ref[...] = jnp.zeros_like(a_ref)
    q = q_ref[0]

    def tile(j, bias):
        start = pl.multiple_of(j * t, t)
        k = k_ref[0, pl.ds(start, t), :]
        v = v_ref[0, pl.ds(start, t), :]
        for c, (m_ref, l_ref, a_ref) in enumerate(stats):
            z = _dot_nt(q[:, c * HEAD_DIM:(c + 1) * HEAD_DIM], k[:, c * HEAD_DIM:(c + 1) * HEAD_DIM])
            if bias is not None:
                z = z + bias
            _online_softmax_step(z, v, m_ref, l_ref, a_ref)

    def far_body(j, carry):
        tile(j, None)
        return carry

    lax.fori_loop(0, jnp.maximum(i - 1, 0), far_body, 0)

    @pl.when(i >= 1)
    def _():
        tile(i - 1, bias_ref[0, 1])

    tile(i, bias_ref[0, 0])

    lamv = lamv_ref[...]
    lam = (jnp.exp(jnp.sum(lamv[0:1] * lamv[1:2], axis=1, keepdims=True))
           - jnp.exp(jnp.sum(lamv[2:3] * lamv[3:4], axis=1, keepdims=True)) + lambda_init)
    o = a0[...] * (1.0 / l0[...]) - lam * (a1[...] * (1.0 / l1[...]))
    o = o * _rms_scale(o, SUBLN_EPS) * g_ref[...] * (1.0 - lambda_init)
    o_ref[0] = o.astype(o_ref.dtype)


def diff_attention(qkv, bias_tiles, lamv, subln_g, *, nheads, t, lambda_init):
    b, s, d3 = qkv.shape
    d = d3 // 3
    hw = 2 * HEAD_DIM
    assert d == nheads * hw and s % t == 0
    kernel = functools.partial(_diff_attn_kernel, t=t, lambda_init=lambda_init)
    return pl.pallas_call(
        kernel,
        out_shape=jax.ShapeDtypeStruct((b, s, d), BF16),
        grid=(b, nheads, s // t),
        in_specs=[
            pl.BlockSpec((1, t, hw), lambda bi, h, i: (bi, i, h)),
            pl.BlockSpec((1, s, hw), lambda bi, h, i: (bi, 0, nheads + h)),
            pl.BlockSpec((1, s, hw), lambda bi, h, i: (bi, 0, 2 * nheads + h)),
            pl.BlockSpec((1, 2, t, t), lambda bi, h, i: (h, 0, 0, 0)),
            pl.BlockSpec((4, HEAD_DIM), lambda bi, h, i: (0, 0)),
            pl.BlockSpec((1, hw), lambda bi, h, i: (0, 0)),
        ],
        out_specs=pl.BlockSpec((1, t, hw), lambda bi, h, i: (bi, i, h)),
        scratch_shapes=[pltpu.VMEM((t, 1), F32), pltpu.VMEM((t, 1), F32), pltpu.VMEM((t, hw), F32),
                        pltpu.VMEM((t, 1), F32), pltpu.VMEM((t, 1), F32), pltpu.VMEM((t, hw), F32)],
        compiler_params=_params("parallel", "parallel", "arbitrary"),
        name="diff_attention",
    )(qkv, qkv, qkv, bias_tiles, lamv, subln_g)


def _fox_attn_kernel(q_ref, k_ref, v_ref, cq_ref, ck_ref, o_ref, m_ref, l_ref, a_ref, *, t):
    h = pl.program_id(1)
    i = pl.program_id(2)
    m_ref[...] = jnp.full_like(m_ref, M_INIT)
    l_ref[...] = jnp.zeros_like(l_ref)
    a_ref[...] = jnp.zeros_like(a_ref)
    q = q_ref[0]
    cq = cq_ref[0]
    lane = lax.broadcasted_iota(jnp.int32, cq.shape, 1)
    ct = jnp.sum(jnp.where(lane == h, cq, 0.0), axis=1, keepdims=True) * (1.0 / SCALE)

    def tile(j, diagonal):
        start = pl.multiple_of(j * t, t)
        k = k_ref[0, pl.ds(start, t), :]
        v = v_ref[0, pl.ds(start, t), :]
        cs = ck_ref[0, 0, :, pl.ds(start, t)] * (1.0 / SCALE)
        z = _dot_nt(q, k) + ct - cs
        if diagonal:
            qi = lax.broadcasted_iota(jnp.int32, (t, t), 0)
            ki = lax.broadcasted_iota(jnp.int32, (t, t), 1)
            z = jnp.where(ki <= qi, z, NEG)
        _online_softmax_step(z, v, m_ref, l_ref, a_ref)

    def far_body(j, carry):
        tile(j, False)
        return carry

    lax.fori_loop(0, i, far_body, 0)
    tile(i, True)
    o_ref[0] = (a_ref[...] * (1.0 / l_ref[...])).astype(o_ref.dtype)


def fox_attention(q, kv, c_rows, c_cols, *, nheads, t):
    b, s, d = q.shape
    assert d == nheads * HEAD_DIM and s % t == 0
    return pl.pallas_call(
        functools.partial(_fox_attn_kernel, t=t),
        out_shape=jax.ShapeDtypeStruct((b, s, d), BF16),
        grid=(b, nheads, s // t),
        in_specs=[
            pl.BlockSpec((1, t, HEAD_DIM), lambda bi, h, i: (bi, i, h)),
            pl.BlockSpec((1, s, HEAD_DIM), lambda bi, h, i: (bi, 0, h)),
            pl.BlockSpec((1, s, HEAD_DIM), lambda bi, h, i: (bi, 0, nheads + h)),
            pl.BlockSpec((1, t, LANES), lambda bi, h, i: (bi, i, 0)),
            pl.BlockSpec((1, 1, 1, s), lambda bi, h, i: (bi, h, 0, 0)),
        ],
        out_specs=pl.BlockSpec((1, t, HEAD_DIM), lambda bi, h, i: (bi, i, h)),
        scratch_shapes=[pltpu.VMEM((t, 1), F32), pltpu.VMEM((t, 1), F32), pltpu.VMEM((t, HEAD_DIM), F32)],
        compiler_params=_params("parallel", "parallel", "arbitrary"),
        name="fox_attention",
    )(q, kv, kv, c_cols, c_rows)


def _split_bf16(x, terms):
    pieces = []
    for _ in range(terms - 1):
        p = x.astype(BF16)
        pieces.append(p)
        x = x - p.astype(F32)
    pieces.append(x.astype(BF16))
    return pieces


def _forget_scan_kernel(x_ref, g_ref, wt_ref, b_ref, crow_ref, ccol_ref, carry_ref, *, ts, nheads):
    @pl.when(pl.program_id(1) == 0)
    def _():
        carry_ref[...] = jnp.zeros_like(carry_ref)

    x = x_ref[0]
    hn = x * _rms_scale(x, NORM_EPS) * g_ref[...]
    hn_hi, hn_lo = _split_bf16(hn, 2)
    w_hi, w_lo = _split_bf16(wt_ref[...], 2)
    f = _dot_nt(w_hi, hn_hi) + _dot_nt(w_hi, hn_lo) + _dot_nt(w_lo, hn_hi) + b_ref[...]
    logf = -(jnp.maximum(-f, 0.0) + jnp.log1p(jnp.exp(-jnp.abs(f))))
    row = lax.broadcasted_iota(jnp.int32, (ts, ts), 0)
    col = lax.broadcasted_iota(jnp.int32, (ts, ts), 1)
    upper = jnp.where(row <= col, 1.0, 0.0).astype(BF16)
    c = carry_ref[:, 0:1]
    for piece in _split_bf16(logf, 3):
        c = c + _dot(piece, upper)
    carry_ref[...] = jnp.broadcast_to(c[:, ts - 1:ts], carry_ref.shape)
    crow_ref[0] = c[:nheads]
    ccol_ref[0] = c.T


def forget_scan(h, g, w_f_t, b_f, *, nheads, ts):
    b, s, d = h.shape
    kernel = functools.partial(_forget_scan_kernel, ts=ts, nheads=nheads)
    return pl.pallas_call(
        kernel,
        out_shape=(jax.ShapeDtypeStruct((b, nheads, s), F32), jax.ShapeDtypeStruct((b, s, LANES), F32)),
        grid=(b, s // ts),
        in_specs=[
            pl.BlockSpec((1, ts, d), lambda bi, i: (bi, i, 0)),
            pl.BlockSpec((1, d), lambda bi, i: (0, 0)),
            pl.BlockSpec((LANES, d), lambda bi, i: (0, 0)),
            pl.BlockSpec((LANES, 1), lambda bi, i: (0, 0)),
        ],
        out_specs=(pl.BlockSpec((1, nheads, ts), lambda bi, i: (bi, 0, i)),
                   pl.BlockSpec((1, ts, LANES), lambda bi, i: (bi, i, 0))),
        scratch_shapes=[pltpu.VMEM((LANES, LANES), F32)],
        compiler_params=_params("parallel", "arbitrary"),
        name="forget_scan",
    )(h, g, w_f_t, b_f)


def _pick(n, target):
    t = min(n, target)
    while n % t:
        t //= 2
    return t


def kernel(x, rel_bias_table, attn_norm_g, mlp_norm_g, w_qkv_a, lam_q1, lam_k1, lam_q2, lam_k2, subln_g,
           w_o_a, kv_norm_g, w_k_b, w_v_b, w_f_b, b_f_b, w_q_b, w_o_b, w_mlp_in, w_mlp_out, final_norm_g):
    b, s, d = x.shape
    n = b * s
    depth = attn_norm_g.shape[0]
    assert depth == 2 and w_qkv_a.shape[0] == 1 and w_q_b.shape[0] == 1
    diff_heads = d // (2 * HEAD_DIM)
    fox_heads = d // HEAD_DIM
    assert fox_heads <= LANES
    dff = w_mlp_in.shape[2]

    t_attn = _pick(s, 512)
    tm = _pick(n, 512)
    tn = _pick(d, 1024)
    tf = _pick(dff, 512)
    ts = _pick(s, 512)

    row = lambda v: v.reshape(1, -1).astype(F32)
    x2 = x.reshape(n, d)

    bias_tiles = t5_bias_tiles(rel_bias_table.astype(F32), t=t_attn)
    qkv = norm_matmul(x2, row(attn_norm_g[0]), w_qkv_a[0].astype(BF16), tm=tm, tn=tn)
    lamv = jnp.stack([lam_q1[0], lam_k1[0], lam_q2[0], lam_k2[0]]).astype(F32)
    lambda_init = 0.8 - 0.6 * math.exp(-0.3 * 0)
    o = diff_attention(qkv.reshape(b, s, 3 * d), bias_tiles, lamv, row(subln_g[0]),
                       nheads=diff_heads, t=t_attn, lambda_init=lambda_init)
    h = matmul_residual(o.reshape(n, d), w_o_a[0].astype(BF16), x2, tm=tm, tn=tn)
    h = mlp_block(h, row(mlp_norm_g[0]), w_mlp_in[0].astype(BF16), w_mlp_out[0].astype(BF16),
                  row(final_norm_g), tm=tm, tf=tf, final_norm=False)

    w_f_t = jnp.zeros((LANES, d), F32).at[:fox_heads].set(w_f_b.T.astype(F32))
    b_f = jnp.zeros((LANES, 1), F32).at[:fox_heads, 0].set(b_f_b.astype(F32))
    c_rows, c_cols = forget_scan(h.reshape(b, s, d), row(kv_norm_g), w_f_t, b_f, nheads=fox_heads, ts=ts)
    w_kv = jnp.concatenate([w_k_b, w_v_b], axis=1).astype(BF16)
    kv = norm_matmul(h, row(kv_norm_g), w_kv, tm=tm, tn=tn)

    q = norm_matmul(h, row(attn_norm_g[1]), w_q_b[0].astype(BF16), tm=tm, tn=tn)
    o = fox_attention(q.reshape(b, s, d), kv.reshape(b, s, 2 * d), c_rows.reshape(b, fox_heads, 1, s), c_cols,
                      nheads=fox_heads, t=t_attn)
    h = matmul_residual(o.reshape(n, d), w_o_b[0].astype(BF16), h, tm=tm, tn=tn)
    h = mlp_block(h, row(mlp_norm_g[1]), w_mlp_in[1].astype(BF16), w_mlp_out[1].astype(BF16),
                  row(final_norm_g), tm=tm, tf=tf, final_norm=True)
    return h.reshape(b, s, d)
```

```python
import functools
import math

import jax
import jax.numpy as jnp
from jax import lax
from jax.experimental import pallas as pl
from jax.experimental.pallas import tpu as pltpu

HEAD_DIM = 128
CHUNK = 64
REL_BUCKETS = 32
REL_MAX_DIST = 128
NORM_EPS = 1e-6
SUBLN_EPS = 1e-5
NEG = -1e30
SCALE = HEAD_DIM ** -0.5
EXP2_SCALE = SCALE * math.log2(math.e)
M_INIT = -1e30

V7X_VMEM_LIMIT_BYTES = 56 * 1024 * 1024
LANES = 128
C_PIECES = 3

F32 = jnp.float32
BF16 = jnp.bfloat16


def _params(*semantics):
    return pltpu.CompilerParams(dimension_semantics=semantics, vmem_limit_bytes=V7X_VMEM_LIMIT_BYTES)


def _rms_scale(x, eps):
    return lax.rsqrt(jnp.mean(x * x, axis=-1, keepdims=True) + eps)


def _dot(a, b):
    return jnp.dot(a, b, preferred_element_type=F32)


def _dot_nt(a, b):
    return lax.dot_general(a, b, (((1,), (1,)), ((), ())), preferred_element_type=F32)


def _split_bf16(x, terms):
    pieces = []
    for _ in range(terms - 1):
        p = x.astype(BF16).astype(x.dtype)
        pieces.append(p)
        x = x - p
    pieces.append(x.astype(BF16).astype(x.dtype))
    return pieces


def _norm_matmul_kernel(x_ref, g_ref, w_ref, o_ref, hn_ref, *, transpose_out):
    @pl.when(pl.program_id(1) == 0)
    def _():
        x = x_ref[...]
        hn_ref[...] = (x * _rms_scale(x, NORM_EPS) * g_ref[...]).astype(hn_ref.dtype)

    if transpose_out:
        o_ref[...] = _dot_nt(w_ref[...], hn_ref[...]).astype(o_ref.dtype)
    else:
        o_ref[...] = _dot(hn_ref[...], w_ref[...]).astype(o_ref.dtype)


def norm_matmul(x, g, w, *, tm, tn, transpose_out=False):
    n, d = x.shape
    if transpose_out:
        nout = w.shape[0]
        w_spec = pl.BlockSpec((tn, d), lambda i, j: (j, 0))
        out_shape = jax.ShapeDtypeStruct((nout, n), BF16)
        out_spec = pl.BlockSpec((tn, tm), lambda i, j: (j, i))
    else:
        nout = w.shape[1]
        w_spec = pl.BlockSpec((d, tn), lambda i, j: (0, j))
        out_shape = jax.ShapeDtypeStruct((n, nout), BF16)
        out_spec = pl.BlockSpec((tm, tn), lambda i, j: (i, j))
    return pl.pallas_call(
        functools.partial(_norm_matmul_kernel, transpose_out=transpose_out),
        out_shape=out_shape,
        grid=(n // tm, nout // tn),
        in_specs=[
            pl.BlockSpec((tm, d), lambda i, j: (i, 0)),
            pl.BlockSpec((1, d), lambda i, j: (0, 0)),
            w_spec,
        ],
        out_specs=out_spec,
        scratch_shapes=[pltpu.VMEM((tm, d), BF16)],
        compiler_params=_params("parallel", "arbitrary"),
        name="norm_matmul_t" if transpose_out else "norm_matmul",
    )(x, g, w)


def _matmul_residual_kernel(a_ref, w_ref, r_ref, o_ref):
    o_ref[...] = r_ref[...] + _dot(a_ref[...], w_ref[...])


def matmul_residual(a, w, res, *, tm, tn):
    n, k = a.shape
    nout = w.shape[1]
    return pl.pallas_call(
        _matmul_residual_kernel,
        out_shape=jax.ShapeDtypeStruct((n, nout), F32),
        grid=(n // tm, nout // tn),
        in_specs=[
            pl.BlockSpec((tm, k), lambda i, j: (i, 0)),
            pl.BlockSpec((k, tn), lambda i, j: (0, j)),
            pl.BlockSpec((tm, tn), lambda i, j: (i, j)),
        ],
        out_specs=pl.BlockSpec((tm, tn), lambda i, j: (i, j)),
        compiler_params=_params("parallel", "arbitrary"),
        name="matmul_residual",
    )(a, w, res)


def _mlp_kernel(x_ref, g_ref, win_ref, wout_ref, gf_ref, o_ref, hn_ref, acc_ref, *, final_norm):
    f = pl.program_id(1)

    @pl.when(f == 0)
    def _():
        x = x_ref[...]
        hn_ref[...] = (x * _rms_scale(x, NORM_EPS) * g_ref[...]).astype(hn_ref.dtype)
        acc_ref[...] = jnp.zeros_like(acc_ref)

    u = _dot(hn_ref[...], win_ref[...])
    u = jnp.square(jnp.maximum(u, 0.0)).astype(BF16)
    acc_ref[...] += _dot(u, wout_ref[...])

    @pl.when(f == pl.num_programs(1) - 1)
    def _():
        y = x_ref[...] + acc_ref[...]
        if final_norm:
            y = y * _rms_scale(y, NORM_EPS) * gf_ref[...]
        o_ref[...] = y


def mlp_block(x, g, w_in, w_out, g_final, *, tm, tf, final_norm):
    n, d = x.shape
    dff = w_in.shape[1]
    return pl.pallas_call(
        functools.partial(_mlp_kernel, final_norm=final_norm),
        out_shape=jax.ShapeDtypeStruct((n, d), F32),
        grid=(n // tm, dff // tf),
        in_specs=[
            pl.BlockSpec((tm, d), lambda i, f: (i, 0)),
            pl.BlockSpec((1, d), lambda i, f: (0, 0)),
            pl.BlockSpec((d, tf), lambda i, f: (0, f)),
            pl.BlockSpec((tf, d), lambda i, f: (f, 0)),
            pl.BlockSpec((1, d), lambda i, f: (0, 0)),
        ],
        out_specs=pl.BlockSpec((tm, d), lambda i, f: (i, 0)),
        scratch_shapes=[pltpu.VMEM((tm, d), BF16), pltpu.VMEM((tm, d), F32)],
        compiler_params=_params("parallel", "arbitrary"),
        name="mlp_block",
    )(x, g, w_in, w_out, g_final)


def _t5_bias_kernel(table_ref, o_ref, *, t):
    h = pl.program_id(0)
    half = REL_BUCKETS // 2
    max_exact = half // 2
    ki = lax.broadcasted_iota(jnp.int32, (t, t), 0)
    qi = lax.broadcasted_iota(jnp.int32, (t, t), 1)
    far = table_ref[half - 1, h]
    for d in range(2):
        rel = ki - d * t - qi
        ret = jnp.where(rel > 0, half, 0)
        n = jnp.abs(rel)
        nf = jnp.maximum(n, 1).astype(F32)
        large = max_exact + (jnp.log(nf / max_exact) / math.log(REL_MAX_DIST / max_exact)
                             * (half - max_exact)).astype(jnp.int32)
        large = jnp.minimum(large, half - 1)
        bucket = ret + jnp.where(n < max_exact, n, large)
        val = jnp.zeros((t, t), F32)
        for b in range(REL_BUCKETS):
            val = jnp.where(bucket == b, table_ref[b, h], val)
        val = (val - far) * (1.0 / SCALE)
        if d == 0:
            mask = (ki >> 6) <= (qi >> 6)
            val = jnp.where(mask, val, NEG)
        o_ref[0, d] = val


def t5_bias_tiles(rel_table, *, t):
    nheads = rel_table.shape[1]
    assert CHUNK == 64 and t % CHUNK == 0 and t >= REL_MAX_DIST
    return pl.pallas_call(
        functools.partial(_t5_bias_kernel, t=t),
        out_shape=jax.ShapeDtypeStruct((nheads, 2, t, t), F32),
        grid=(nheads,),
        in_specs=[pl.BlockSpec(memory_space=pltpu.SMEM)],
        out_specs=pl.BlockSpec((1, 2, t, t), lambda h: (h, 0, 0, 0)),
        compiler_params=_params("arbitrary"),
        name="t5_bias_tiles",
    )(rel_table)


def _init_stats(m_ref, l_ref, acc_ref):
    m_ref[...] = jnp.full_like(m_ref, M_INIT)
    l_ref[...] = jnp.zeros_like(l_ref)
    acc_ref[...] = jnp.zeros_like(acc_ref)


def _online_softmax_step(z, vt, m_ref, l_ref, acc_ref):
    m_prev = m_ref[...]
    m_new = jnp.maximum(m_prev, jnp.max(z, axis=0, keepdims=True))
    p = jnp.exp2((z - m_new) * EXP2_SCALE)
    alpha = jnp.exp2((m_prev - m_new) * EXP2_SCALE)
    l_ref[...] = alpha * l_ref[...] + jnp.sum(p, axis=0, keepdims=True)
    acc_ref[...] = alpha * acc_ref[...] + _dot(vt, p.astype(BF16))
    m_ref[...] = m_new


def _diff_attn_kernel(q_ref, k_ref, vt_ref, bias_ref, lamv_ref, g_ref, o_ref,
                      m0, l0, a0, m1, l1, a1, *, t, lambda_init):
    i = pl.program_id(2)
    stats = ((m0, l0, a0), (m1, l1, a1))
    for s in stats:
        _init_stats(*s)
    q = q_ref[0]

    def tile(j, bias):
        start = pl.multiple_of(j * t, t)
        k = k_ref[0, pl.ds(start, t), :]
        vt = vt_ref[:, pl.ds(start, t)]
        for c, s in enumerate(stats):
            half = slice(c * HEAD_DIM, (c + 1) * HEAD_DIM)
            z = _dot_nt(k[:, half], q[:, half])
            if bias is not None:
                z = z + bias
            _online_softmax_step(z, vt, *s)

    def far_body(j, carry):
        tile(j, None)
        return carry

    lax.fori_loop(0, jnp.maximum(i - 1, 0), far_body, 0)

    @pl.when(i >= 1)
    def _():
        tile(i - 1, bias_ref[0, 1])

    tile(i, bias_ref[0, 0])

    lamv = lamv_ref[...]
    lam = (jnp.exp(jnp.sum(lamv[0:1] * lamv[1:2], axis=1, keepdims=True))
           - jnp.exp(jnp.sum(lamv[2:3] * lamv[3:4], axis=1, keepdims=True)) + lambda_init)
    o = a0[...] * (1.0 / l0[...]) - lam * (a1[...] * (1.0 / l1[...]))
    ms = jnp.mean(o * o, axis=0, keepdims=True)
    o = o * lax.rsqrt(ms + SUBLN_EPS) * g_ref[...] * (1.0 - lambda_init)
    o_ref[0] = o.T.astype(o_ref.dtype)


def diff_attention(qk, vt, bias_tiles, lamv, subln_g_col, *, batch, nheads, t, lambda_init):
    b, s, d2 = qk.shape
    d = d2 // 2
    hw = 2 * HEAD_DIM
    assert d == nheads * hw and s % t == 0 and b == batch
    kernel = functools.partial(_diff_attn_kernel, t=t, lambda_init=lambda_init)
    stat = lambda: pltpu.VMEM((1, t), F32)
    acc = lambda: pltpu.VMEM((hw, t), F32)
    return pl.pallas_call(
        kernel,
        out_shape=jax.ShapeDtypeStruct((b, s, d), BF16),
        grid=(b, nheads, s // t),
        in_specs=[
            pl.BlockSpec((1, t, hw), lambda bi, h, i: (bi, i, h)),
            pl.BlockSpec((1, s, hw), lambda bi, h, i: (bi, 0, nheads + h)),
            pl.BlockSpec((hw, s), lambda bi, h, i: (h, bi)),
            pl.BlockSpec((1, 2, t, t), lambda bi, h, i: (h, 0, 0, 0)),
            pl.BlockSpec((4, HEAD_DIM), lambda bi, h, i: (0, 0)),
            pl.BlockSpec((hw, 1), lambda bi, h, i: (0, 0)),
        ],
        out_specs=pl.BlockSpec((1, t, hw), lambda bi, h, i: (bi, i, h)),
        scratch_shapes=[stat(), stat(), acc(), stat(), stat(), acc()],
        compiler_params=_params("parallel", "parallel", "arbitrary"),
        name="diff_attention",
    )(qk, qk, vt, bias_tiles, lamv, subln_g_col)


def _fox_attn_kernel(q_ref, k_ref, vt_ref, kc_ref, cq_ref, o_ref, m_ref, l_ref, a_ref, *, t, nheads):
    h = pl.program_id(1)
    i = pl.program_id(2)
    _init_stats(m_ref, l_ref, a_ref)

    cq = cq_ref[0]
    lane = lax.broadcasted_iota(jnp.int32, cq.shape, 1)
    ct = jnp.sum(jnp.where(lane == h, cq, 0.0), axis=1, keepdims=True) * (1.0 / SCALE)
    npc = C_PIECES * nheads
    qc = jnp.where(lane < npc, jnp.where((lane & (nheads - 1)) == h, 1.0, 0.0), 0.0)
    for p, piece in enumerate(_split_bf16(ct, C_PIECES)):
        qc = jnp.where(lane == npc + p, piece, qc)
    q = jnp.concatenate([q_ref[0], qc.astype(BF16)], axis=1)

    def tile(j, diagonal):
        start = pl.multiple_of(j * t, t)
        k = jnp.concatenate([k_ref[0, pl.ds(start, t), :], kc_ref[0, pl.ds(start, t), :]], axis=1)
        vt = vt_ref[:, pl.ds(start, t)]
        z = _dot_nt(k, q)
        if diagonal:
            ki = lax.broadcasted_iota(jnp.int32, (t, t), 0)
            qi = lax.broadcasted_iota(jnp.int32, (t, t), 1)
            z = jnp.where(ki <= qi, z, NEG)
        _online_softmax_step(z, vt, m_ref, l_ref, a_ref)

    def far_body(j, carry):
        tile(j, False)
        return carry

    lax.fori_loop(0, i, far_body, 0)
    tile(i, True)
    o = a_ref[...] * (1.0 / l_ref[...])
    o_ref[0] = o.T.astype(o_ref.dtype)


def fox_attention(q, k, vt, kc, c_cols, *, nheads, t):
    b, s, d = q.shape
    assert d == nheads * HEAD_DIM and s % t == 0
    assert nheads & (nheads - 1) == 0 and C_PIECES * nheads + C_PIECES <= LANES
    return pl.pallas_call(
        functools.partial(_fox_attn_kernel, t=t, nheads=nheads),
        out_shape=jax.ShapeDtypeStruct((b, s, d), BF16),
        grid=(b, nheads, s // t),
        in_specs=[
            pl.BlockSpec((1, t, HEAD_DIM), lambda bi, h, i: (bi, i, h)),
            pl.BlockSpec((1, s, HEAD_DIM), lambda bi, h, i: (bi, 0, h)),
            pl.BlockSpec((HEAD_DIM, s), lambda bi, h, i: (h, bi)),
            pl.BlockSpec((1, s, LANES), lambda bi, h, i: (bi, 0, 0)),
            pl.BlockSpec((1, t, LANES), lambda bi, h, i: (bi, i, 0)),
        ],
        out_specs=pl.BlockSpec((1, t, HEAD_DIM), lambda bi, h, i: (bi, i, h)),
        scratch_shapes=[pltpu.VMEM((1, t), F32), pltpu.VMEM((1, t), F32), pltpu.VMEM((HEAD_DIM, t), F32)],
        compiler_params=_params("parallel", "parallel", "arbitrary"),
        name="fox_attention",
    )(q, k, vt, kc, c_cols)


def _forget_scan_kernel(x_ref, g_ref, wt_ref, b_ref, kc_ref, ccol_ref, carry_ref, *, ts, nheads):
    @pl.when(pl.program_id(1) == 0)
    def _():
        carry_ref[...] = jnp.zeros_like(carry_ref)

    x = x_ref[0]
    hn = x * _rms_scale(x, NORM_EPS) * g_ref[...]
    hn_hi, hn_lo = (p.astype(BF16) for p in _split_bf16(hn, 2))
    w_hi, w_lo = (p.astype(BF16) for p in _split_bf16(wt_ref[...], 2))
    f = _dot_nt(w_hi, hn_hi) + _dot_nt(w_hi, hn_lo) + _dot_nt(w_lo, hn_hi) + b_ref[...]
    logf = -(jnp.maximum(-f, 0.0) + jnp.log1p(jnp.exp(-jnp.abs(f))))
    row = lax.broadcasted_iota(jnp.int32, (ts, ts), 0)
    col = lax.broadcasted_iota(jnp.int32, (ts, ts), 1)
    upper = jnp.where(row <= col, 1.0, 0.0).astype(BF16)
    c = carry_ref[:, 0:1]
    for piece in _split_bf16(logf, C_PIECES):
        c = c + _dot(piece.astype(BF16), upper)
    carry_ref[...] = jnp.broadcast_to(c[:, ts - 1:ts], carry_ref.shape)
    ccol_ref[0] = c.T

    hrow = lax.broadcasted_iota(jnp.int32, c.shape, 0)
    npc = C_PIECES * nheads
    aug = jnp.where(hrow < npc + C_PIECES, 1.0, 0.0)
    for p, piece in reversed(list(enumerate(_split_bf16(c * (-1.0 / SCALE), C_PIECES)))):
        aug = jnp.where(hrow < (p + 1) * nheads, piece, aug)
    kc_ref[0] = aug.T.astype(BF16)


def forget_scan(h, g, w_f_t, b_f, *, nheads, ts):
    b, s, d = h.shape
    kernel = functools.partial(_forget_scan_kernel, ts=ts, nheads=nheads)
    return pl.pallas_call(
        kernel,
        out_shape=(jax.ShapeDtypeStruct((b, s, LANES), BF16), jax.ShapeDtypeStruct((b, s, LANES), F32)),
        grid=(b, s // ts),
        in_specs=[
            pl.BlockSpec((1, ts, d), lambda bi, i: (bi, i, 0)),
            pl.BlockSpec((1, d), lambda bi, i: (0, 0)),
            pl.BlockSpec((LANES, d), lambda bi, i: (0, 0)),
            pl.BlockSpec((LANES, 1), lambda bi, i: (0, 0)),
        ],
        out_specs=(pl.BlockSpec((1, ts, LANES), lambda bi, i: (bi, i, 0)),
                   pl.BlockSpec((1, ts, LANES), lambda bi, i: (bi, i, 0))),
        scratch_shapes=[pltpu.VMEM((LANES, LANES), F32)],
        compiler_params=_params("parallel", "arbitrary"),
        name="forget_scan",
    )(h, g, w_f_t, b_f)


def _pick(n, target):
    t = min(n, target)
    while n % t:
        t //= 2
    return t


def kernel(x, rel_bias_table, attn_norm_g, mlp_norm_g, w_qkv_a, lam_q1, lam_k1, lam_q2, lam_k2, subln_g,
           w_o_a, kv_norm_g, w_k_b, w_v_b, w_f_b, b_f_b, w_q_b, w_o_b, w_mlp_in, w_mlp_out, final_norm_g):
    b, s, d = x.shape
    n = b * s
    depth = attn_norm_g.shape[0]
    assert depth == 2 and w_qkv_a.shape[0] == 1 and w_q_b.shape[0] == 1
    diff_heads = d // (2 * HEAD_DIM)
    fox_heads = d // HEAD_DIM
    dff = w_mlp_in.shape[2]

    t_attn = _pick(s, 512)
    tm = _pick(n, 512)
    tn = _pick(d, 1024)
    tf = _pick(dff, 512)
    ts = _pick(s, 512)

    row = lambda v: v.reshape(1, -1).astype(F32)
    x2 = x.reshape(n, d)

    bias_tiles = t5_bias_tiles(rel_bias_table.astype(F32), t=t_attn)
    g0 = row(attn_norm_g[0])
    qk = norm_matmul(x2, g0, w_qkv_a[0, :, :2 * d].astype(BF16), tm=tm, tn=tn)
    vt = norm_matmul(x2, g0, w_qkv_a[0, :, 2 * d:].T.astype(BF16), tm=tm, tn=tn, transpose_out=True)
    lamv = jnp.stack([lam_q1[0], lam_k1[0], lam_q2[0], lam_k2[0]]).astype(F32)
    lambda_init = 0.8 - 0.6 * math.exp(-0.3 * 0)
    o = diff_attention(qk.reshape(b, s, 2 * d), vt, bias_tiles, lamv, subln_g[0].reshape(-1, 1).astype(F32),
                       batch=b, nheads=diff_heads, t=t_attn, lambda_init=lambda_init)
    h = matmul_residual(o.reshape(n, d), w_o_a[0].astype(BF16), x2, tm=tm, tn=tn)
    h = mlp_block(h, row(mlp_norm_g[0]), w_mlp_in[0].astype(BF16), w_mlp_out[0].astype(BF16),
                  row(final_norm_g), tm=tm, tf=tf, final_norm=False)

    gkv = row(kv_norm_g)
    w_f_t = jnp.zeros((LANES, d), F32)
    b_f = jnp.zeros((LANES, 1), F32)
    for p in range(C_PIECES):
        w_f_t = w_f_t.at[p * fox_heads:(p + 1) * fox_heads].set(w_f_b.T.astype(F32))
        b_f = b_f.at[p * fox_heads:(p + 1) * fox_heads, 0].set(b_f_b.astype(F32))
    kc, c_cols = forget_scan(h.reshape(b, s, d), gkv, w_f_t, b_f, nheads=fox_heads, ts=ts)
    k = norm_matmul(h, gkv, w_k_b.astype(BF16), tm=tm, tn=tn)
    vt = norm_matmul(h, gkv, w_v_b.T.astype(BF16), tm=tm, tn=tn, transpose_out=True)

    q = norm_matmul(h, row(attn_norm_g[1]), w_q_b[0].astype(BF16), tm=tm, tn=tn)
    o = fox_attention(q.reshape(b, s, d), k.reshape(b, s, d), vt, kc, c_cols, nheads=fox_heads, t=t_attn)
    h = matmul_residual(o.reshape(n, d), w_o_b[0].astype(BF16), h, tm=tm, tn=tn)
    h = mlp_block(h, row(mlp_norm_g[1]), w_mlp_in[1].astype(BF16), w_mlp_out[1].astype(BF16),
                  row(final_norm_g), tm=tm, tf=tf, final_norm=True)
    return h.reshape(b, s, d)
```

```python
import functools
import math

import jax
import jax.numpy as jnp
from jax import lax
from jax.experimental import pallas as pl
from jax.experimental.pallas import tpu as pltpu

HEAD_DIM = 128
CHUNK = 64
REL_BUCKETS = 32
REL_MAX_DIST = 128
NORM_EPS = 1e-6
SUBLN_EPS = 1e-5
NEG = -1e30
SCALE = HEAD_DIM ** -0.5
LOG2E = math.log2(math.e)
Q_SCALE = SCALE * LOG2E
M_INIT = -1e30
FAR_UNROLL = 4
ONES_ROWS = 16

V7X_VMEM_LIMIT_BYTES = 56 * 1024 * 1024
LANES = 128
C_PIECES = 3

F32 = jnp.float32
BF16 = jnp.bfloat16


def _params(*semantics):
    return pltpu.CompilerParams(dimension_semantics=semantics, vmem_limit_bytes=V7X_VMEM_LIMIT_BYTES)


def _rms_scale(x, eps):
    return lax.rsqrt(jnp.mean(x * x, axis=-1, keepdims=True) + eps)


def _dot(a, b):
    return jnp.dot(a, b, preferred_element_type=F32)


def _dot_nt(a, b):
    return lax.dot_general(a, b, (((1,), (1,)), ((), ())), preferred_element_type=F32)


def _split_bf16(x, terms):
    pieces = []
    for _ in range(terms - 1):
        p = x.astype(BF16).astype(x.dtype)
        pieces.append(p)
        x = x - p
    pieces.append(x.astype(BF16).astype(x.dtype))
    return pieces


def _norm_matmul_kernel(x_ref, g_ref, w_ref, o_ref, hn_ref, *, transpose_out, scaled_blocks):
    @pl.when(pl.program_id(1) == 0)
    def _():
        x = x_ref[...]
        hn_ref[...] = (x * _rms_scale(x, NORM_EPS) * g_ref[...]).astype(hn_ref.dtype)

    if transpose_out:
        y = _dot_nt(w_ref[...], hn_ref[...])
    else:
        y = _dot(hn_ref[...], w_ref[...])
    if scaled_blocks:
        y = y * jnp.where(pl.program_id(1) < scaled_blocks, Q_SCALE, 1.0)
    o_ref[...] = y.astype(o_ref.dtype)


def norm_matmul(x, g, w, *, tm, tn, transpose_out=False, scaled_cols=0):
    assert scaled_cols % tn == 0
    n, d = x.shape
    if transpose_out:
        nout = w.shape[0]
        w_spec = pl.BlockSpec((tn, d), lambda i, j: (j, 0))
        out_shape = jax.ShapeDtypeStruct((nout, n), BF16)
        out_spec = pl.BlockSpec((tn, tm), lambda i, j: (j, i))
    else:
        nout = w.shape[1]
        w_spec = pl.BlockSpec((d, tn), lambda i, j: (0, j))
        out_shape = jax.ShapeDtypeStruct((n, nout), BF16)
        out_spec = pl.BlockSpec((tm, tn), lambda i, j: (i, j))
    return pl.pallas_call(
        functools.partial(_norm_matmul_kernel, transpose_out=transpose_out, scaled_blocks=scaled_cols // tn),
        out_shape=out_shape,
        grid=(n // tm, nout // tn),
        in_specs=[
            pl.BlockSpec((tm, d), lambda i, j: (i, 0)),
            pl.BlockSpec((1, d), lambda i, j: (0, 0)),
            w_spec,
        ],
        out_specs=out_spec,
        scratch_shapes=[pltpu.VMEM((tm, d), BF16)],
        compiler_params=_params("parallel", "arbitrary"),
        name="norm_matmul_t" if transpose_out else "norm_matmul",
    )(x, g, w)


def _matmul_residual_kernel(a_ref, w_ref, r_ref, o_ref):
    o_ref[...] = r_ref[...] + _dot(a_ref[...], w_ref[...])


def matmul_residual(a, w, res, *, tm, tn):
    n, k = a.shape
    nout = w.shape[1]
    return pl.pallas_call(
        _matmul_residual_kernel,
        out_shape=jax.ShapeDtypeStruct((n, nout), F32),
        grid=(n // tm, nout // tn),
        in_specs=[
            pl.BlockSpec((tm, k), lambda i, j: (i, 0)),
            pl.BlockSpec((k, tn), lambda i, j: (0, j)),
            pl.BlockSpec((tm, tn), lambda i, j: (i, j)),
        ],
        out_specs=pl.BlockSpec((tm, tn), lambda i, j: (i, j)),
        compiler_params=_params("parallel", "arbitrary"),
        name="matmul_residual",
    )(a, w, res)


def _mlp_kernel(x_ref, g_ref, win_ref, wout_ref, gf_ref, o_ref, hn_ref, acc_ref, *, final_norm):
    f = pl.program_id(1)

    @pl.when(f == 0)
    def _():
        x = x_ref[...]
        hn_ref[...] = (x * _rms_scale(x, NORM_EPS) * g_ref[...]).astype(hn_ref.dtype)
        acc_ref[...] = jnp.zeros_like(acc_ref)

    u = _dot(hn_ref[...], win_ref[...])
    u = jnp.square(jnp.maximum(u, 0.0)).astype(BF16)
    acc_ref[...] += _dot(u, wout_ref[...])

    @pl.when(f == pl.num_programs(1) - 1)
    def _():
        y = x_ref[...] + acc_ref[...]
        if final_norm:
            y = y * _rms_scale(y, NORM_EPS) * gf_ref[...]
        o_ref[...] = y


def mlp_block(x, g, w_in, w_out, g_final, *, tm, tf, final_norm):
    n, d = x.shape
    dff = w_in.shape[1]
    return pl.pallas_call(
        functools.partial(_mlp_kernel, final_norm=final_norm),
        out_shape=jax.ShapeDtypeStruct((n, d), F32),
        grid=(n // tm, dff // tf),
        in_specs=[
            pl.BlockSpec((tm, d), lambda i, f: (i, 0)),
            pl.BlockSpec((1, d), lambda i, f: (0, 0)),
            pl.BlockSpec((d, tf), lambda i, f: (0, f)),
            pl.BlockSpec((tf, d), lambda i, f: (f, 0)),
            pl.BlockSpec((1, d), lambda i, f: (0, 0)),
        ],
        out_specs=pl.BlockSpec((tm, d), lambda i, f: (i, 0)),
        scratch_shapes=[pltpu.VMEM((tm, d), BF16), pltpu.VMEM((tm, d), F32)],
        compiler_params=_params("parallel", "arbitrary"),
        name="mlp_block",
    )(x, g, w_in, w_out, g_final)


def _t5_bias_kernel(table_ref, o_ref, *, t):
    h = pl.program_id(0)
    half = REL_BUCKETS // 2
    max_exact = half // 2
    ki = lax.broadcasted_iota(jnp.int32, (t, t), 0)
    qi = lax.broadcasted_iota(jnp.int32, (t, t), 1)
    far = table_ref[half - 1, h]
    for d in range(2):
        rel = ki - d * t - qi
        ret = jnp.where(rel > 0, half, 0)
        n = jnp.abs(rel)
        nf = jnp.maximum(n, 1).astype(F32)
        large = max_exact + (jnp.log(nf / max_exact) / math.log(REL_MAX_DIST / max_exact)
                             * (half - max_exact)).astype(jnp.int32)
        large = jnp.minimum(large, half - 1)
        bucket = ret + jnp.where(n < max_exact, n, large)
        val = jnp.zeros((t, t), F32)
        for b in range(REL_BUCKETS):
            val = jnp.where(bucket == b, table_ref[b, h], val)
        val = (val - far) * LOG2E
        if d == 0:
            mask = (ki >> 6) <= (qi >> 6)
            val = jnp.where(mask, val, NEG)
        o_ref[0, d] = val


def t5_bias_tiles(rel_table, *, t):
    nheads = rel_table.shape[1]
    assert CHUNK == 64 and t % CHUNK == 0 and t >= REL_MAX_DIST
    return pl.pallas_call(
        functools.partial(_t5_bias_kernel, t=t),
        out_shape=jax.ShapeDtypeStruct((nheads, 2, t, t), F32),
        grid=(nheads,),
        in_specs=[pl.BlockSpec(memory_space=pltpu.SMEM)],
        out_specs=pl.BlockSpec((1, 2, t, t), lambda h: (h, 0, 0, 0)),
        compiler_params=_params("arbitrary"),
        name="t5_bias_tiles",
    )(rel_table)


def _init_stats(m_ref, l_ref, acc_ref):
    m_ref[...] = jnp.full_like(m_ref, M_INIT)
    if l_ref is not None:
        l_ref[...] = jnp.zeros_like(l_ref)
    acc_ref[...] = jnp.zeros_like(acc_ref)


def _online_softmax_step(z, vt, m_ref, l_ref, acc_ref):
    m_prev = m_ref[...]
    m_new = jnp.maximum(m_prev, jnp.max(z, axis=0, keepdims=True))
    p = jnp.exp2(z - m_new)
    alpha = jnp.exp2(m_prev - m_new)
    if l_ref is not None:
        l_ref[...] = alpha * l_ref[...] + jnp.sum(p, axis=0, keepdims=True)
    acc_ref[...] = alpha * acc_ref[...] + _dot(vt, p.astype(BF16))
    m_ref[...] = m_new


def _far_loop(consume, n_far):
    def body(u):
        def run(jj, carry):
            for r in range(u):
                consume(jj * u + r, jj * u + r + 1)
            return carry
        return run

    n_main = n_far // FAR_UNROLL
    lax.fori_loop(0, n_main, body(FAR_UNROLL), 0)
    lax.fori_loop(n_main * FAR_UNROLL, n_far, body(1), 0)


def _diff_attn_kernel(q_ref, k_ref, vt_ref, bias_ref, lamv_ref, g_ref, o_ref,
                      m0, l0, a0, m1, l1, a1, z_ref, *, t, lambda_init):
    i = pl.program_id(2)
    stats = ((m0, l0, a0), (m1, l1, a1))
    for s in stats:
        _init_stats(*s)
    q = q_ref[0]

    def scores(j):
        start = pl.multiple_of(j * t, t)
        k = k_ref[0, pl.ds(start, t), :]
        for c in range(2):
            half = slice(c * HEAD_DIM, (c + 1) * HEAD_DIM)
            z_ref[c] = _dot_nt(k[:, half], q[:, half])

    def consume(j, bias, j_next):
        zs = [z_ref[c] for c in range(2)]
        if j_next is not None:
            scores(j_next)
        start = pl.multiple_of(j * t, t)
        vt = vt_ref[:, pl.ds(start, t)]
        for z, s in zip(zs, stats):
            if bias is not None:
                z = z + bias
            _online_softmax_step(z, vt, *s)

    scores(0)
    _far_loop(lambda j, j_next: consume(j, None, j_next), jnp.maximum(i - 1, 0))

    @pl.when(i >= 1)
    def _():
        consume(i - 1, bias_ref[0, 1], i)

    consume(i, bias_ref[0, 0], None)

    lamv = lamv_ref[...]
    lam = (jnp.exp(jnp.sum(lamv[0:1] * lamv[1:2], axis=1, keepdims=True))
           - jnp.exp(jnp.sum(lamv[2:3] * lamv[3:4], axis=1, keepdims=True)) + lambda_init)
    o = a0[...] * (1.0 / l0[...]) - lam * (a1[...] * (1.0 / l1[...]))
    ms = jnp.mean(o * o, axis=0, keepdims=True)
    o = o * lax.rsqrt(ms + SUBLN_EPS) * g_ref[...] * (1.0 - lambda_init)
    o_ref[0] = o.T.astype(o_ref.dtype)


def diff_attention(qk, vt, bias_tiles, lamv, subln_g_col, *, batch, nheads, t, lambda_init):
    b, s, d2 = qk.shape
    d = d2 // 2
    hw = 2 * HEAD_DIM
    assert d == nheads * hw and s % t == 0 and b == batch
    kernel = functools.partial(_diff_attn_kernel, t=t, lambda_init=lambda_init)
    stat = lambda: pltpu.VMEM((1, t), F32)
    acc = lambda: pltpu.VMEM((hw, t), F32)
    return pl.pallas_call(
        kernel,
        out_shape=jax.ShapeDtypeStruct((b, s, d), BF16),
        grid=(b, nheads, s // t),
        in_specs=[
            pl.BlockSpec((1, t, hw), lambda bi, h, i: (bi, i, h)),
            pl.BlockSpec((1, s, hw), lambda bi, h, i: (bi, 0, nheads + h)),
            pl.BlockSpec((hw, s), lambda bi, h, i: (h, bi)),
            pl.BlockSpec((1, 2, t, t), lambda bi, h, i: (h, 0, 0, 0)),
            pl.BlockSpec((4, HEAD_DIM), lambda bi, h, i: (0, 0)),
            pl.BlockSpec((hw, 1), lambda bi, h, i: (0, 0)),
        ],
        out_specs=pl.BlockSpec((1, t, hw), lambda bi, h, i: (bi, i, h)),
        scratch_shapes=[stat(), stat(), acc(), stat(), stat(), acc(), pltpu.VMEM((2, t, t), F32)],
        compiler_params=_params("parallel", "parallel", "arbitrary"),
        name="diff_attention",
    )(qk, qk, vt, bias_tiles, lamv, subln_g_col)


def _fox_attn_kernel(q_ref, k_ref, vt_ref, kc_ref, cq_ref, o_ref, m_ref, a_ref, z_ref, *, t, nheads):
    h = pl.program_id(1)
    i = pl.program_id(2)
    _init_stats(m_ref, None, a_ref)

    cq = cq_ref[0]
    lane = lax.broadcasted_iota(jnp.int32, cq.shape, 1)
    ct = jnp.sum(jnp.where(lane == h, cq, 0.0), axis=1, keepdims=True) * LOG2E
    npc = C_PIECES * nheads
    qc = jnp.where(lane < npc, jnp.where((lane & (nheads - 1)) == h, 1.0, 0.0), 0.0)
    for p, piece in enumerate(_split_bf16(ct, C_PIECES)):
        qc = jnp.where(lane == npc + p, piece, qc)
    q = jnp.concatenate([q_ref[0], qc.astype(BF16)], axis=1)

    def scores(j):
        start = pl.multiple_of(j * t, t)
        k = jnp.concatenate([k_ref[0, pl.ds(start, t), :], kc_ref[0, pl.ds(start, t), :]], axis=1)
        z_ref[...] = _dot_nt(k, q)

    def consume(j, diagonal, j_next):
        z = z_ref[...]
        if j_next is not None:
            scores(j_next)
        start = pl.multiple_of(j * t, t)
        vt = jnp.concatenate([vt_ref[:, pl.ds(start, t)], jnp.ones((ONES_ROWS, t), BF16)], axis=0)
        if diagonal:
            ki = lax.broadcasted_iota(jnp.int32, (t, t), 0)
            qi = lax.broadcasted_iota(jnp.int32, (t, t), 1)
            z = jnp.where(ki <= qi, z, NEG)
        _online_softmax_step(z, vt, m_ref, None, a_ref)

    scores(0)
    _far_loop(lambda j, j_next: consume(j, False, j_next), i)
    consume(i, True, None)
    o = a_ref[:HEAD_DIM] * (1.0 / a_ref[HEAD_DIM:HEAD_DIM + 1])
    o_ref[0] = o.T.astype(o_ref.dtype)


def fox_attention(q, k, vt, kc, c_cols, *, nheads, t):
    b, s, d = q.shape
    assert d == nheads * HEAD_DIM and s % t == 0
    assert nheads & (nheads - 1) == 0 and C_PIECES * nheads + C_PIECES <= LANES
    return pl.pallas_call(
        functools.partial(_fox_attn_kernel, t=t, nheads=nheads),
        out_shape=jax.ShapeDtypeStruct((b, s, d), BF16),
        grid=(b, nheads, s // t),
        in_specs=[
            pl.BlockSpec((1, t, HEAD_DIM), lambda bi, h, i: (bi, i, h)),
            pl.BlockSpec((1, s, HEAD_DIM), lambda bi, h, i: (bi, 0, h)),
            pl.BlockSpec((HEAD_DIM, s), lambda bi, h, i: (h, bi)),
            pl.BlockSpec((1, s, LANES), lambda bi, h, i: (bi, 0, 0)),
            pl.BlockSpec((1, t, LANES), lambda bi, h, i: (bi, i, 0)),
        ],
        out_specs=pl.BlockSpec((1, t, HEAD_DIM), lambda bi, h, i: (bi, i, h)),
        scratch_shapes=[pltpu.VMEM((1, t), F32), pltpu.VMEM((HEAD_DIM + ONES_ROWS, t), F32),
                        pltpu.VMEM((t, t), F32)],
        compiler_params=_params("parallel", "parallel", "arbitrary"),
        name="fox_attention",
    )(q, k, vt, kc, c_cols)


def _forget_scan_kernel(x_ref, g_ref, wt_ref, b_ref, kc_ref, ccol_ref, carry_ref, *, ts, nheads):
    @pl.when(pl.program_id(1) == 0)
    def _():
        carry_ref[...] = jnp.zeros_like(carry_ref)

    x = x_ref[0]
    hn = x * _rms_scale(x, NORM_EPS) * g_ref[...]
    hn_hi, hn_lo = (p.astype(BF16) for p in _split_bf16(hn, 2))
    w_hi, w_lo = (p.astype(BF16) for p in _split_bf16(wt_ref[...], 2))
    f = _dot_nt(w_hi, hn_hi) + _dot_nt(w_hi, hn_lo) + _dot_nt(w_lo, hn_hi) + b_ref[...]
    logf = -(jnp.maximum(-f, 0.0) + jnp.log1p(jnp.exp(-jnp.abs(f))))
    row = lax.broadcasted_iota(jnp.int32, (ts, ts), 0)
    col = lax.broadcasted_iota(jnp.int32, (ts, ts), 1)
    upper = jnp.where(row <= col, 1.0, 0.0).astype(BF16)
    c = carry_ref[:, 0:1]
    for piece in _split_bf16(logf, C_PIECES):
        c = c + _dot(piece.astype(BF16), upper)
    carry_ref[...] = jnp.broadcast_to(c[:, ts - 1:ts], carry_ref.shape)
    ccol_ref[0] = c.T

    hrow = lax.broadcasted_iota(jnp.int32, c.shape, 0)
    npc = C_PIECES * nheads
    aug = jnp.where(hrow < npc + C_PIECES, 1.0, 0.0)
    for p, piece in reversed(list(enumerate(_split_bf16(c * (-LOG2E), C_PIECES)))):
        aug = jnp.where(hrow < (p + 1) * nheads, piece, aug)
    kc_ref[0] = aug.T.astype(BF16)


def forget_scan(h, g, w_f_t, b_f, *, nheads, ts):
    b, s, d = h.shape
    kernel = functools.partial(_forget_scan_kernel, ts=ts, nheads=nheads)
    return pl.pallas_call(
        kernel,
        out_shape=(jax.ShapeDtypeStruct((b, s, LANES), BF16), jax.ShapeDtypeStruct((b, s, LANES), F32)),
        grid=(b, s // ts),
        in_specs=[
            pl.BlockSpec((1, ts, d), lambda bi, i: (bi, i, 0)),
            pl.BlockSpec((1, d), lambda bi, i: (0, 0)),
            pl.BlockSpec((LANES, d), lambda bi, i: (0, 0)),
            pl.BlockSpec((LANES, 1), lambda bi, i: (0, 0)),
        ],
        out_specs=(pl.BlockSpec((1, ts, LANES), lambda bi, i: (bi, i, 0)),
                   pl.BlockSpec((1, ts, LANES), lambda bi, i: (bi, i, 0))),
        scratch_shapes=[pltpu.VMEM((LANES, LANES), F32)],
        compiler_params=_params("parallel", "arbitrary"),
        name="forget_scan",
    )(h, g, w_f_t, b_f)


def _pick(n, target):
    t = min(n, target)
    while n % t:
        t //= 2
    return t


def kernel(x, rel_bias_table, attn_norm_g, mlp_norm_g, w_qkv_a, lam_q1, lam_k1, lam_q2, lam_k2, subln_g,
           w_o_a, kv_norm_g, w_k_b, w_v_b, w_f_b, b_f_b, w_q_b, w_o_b, w_mlp_in, w_mlp_out, final_norm_g):
    b, s, d = x.shape
    n = b * s
    depth = attn_norm_g.shape[0]
    assert depth == 2 and w_qkv_a.shape[0] == 1 and w_q_b.shape[0] == 1
    diff_heads = d // (2 * HEAD_DIM)
    fox_heads = d // HEAD_DIM
    dff = w_mlp_in.shape[2]

    t_attn = _pick(s, 512)
    tm = _pick(n, 512)
    tn = _pick(d, 1024)
    tf = _pick(dff, 512)
    ts = _pick(s, 512)

    row = lambda v: v.reshape(1, -1).astype(F32)
    x2 = x.reshape(n, d)

    bias_tiles = t5_bias_tiles(rel_bias_table.astype(F32), t=t_attn)
    g0 = row(attn_norm_g[0])
    qk = norm_matmul(x2, g0, w_qkv_a[0, :, :2 * d].astype(BF16), tm=tm, tn=tn, scaled_cols=d)
    vt = norm_matmul(x2, g0, w_qkv_a[0, :, 2 * d:].T.astype(BF16), tm=tm, tn=tn, transpose_out=True)
    lamv = jnp.stack([lam_q1[0], lam_k1[0], lam_q2[0], lam_k2[0]]).astype(F32)
    lambda_init = 0.8 - 0.6 * math.exp(-0.3 * 0)
    o = diff_attention(qk.reshape(b, s, 2 * d), vt, bias_tiles, lamv, subln_g[0].reshape(-1, 1).astype(F32),
                       batch=b, nheads=diff_heads, t=t_attn, lambda_init=lambda_init)
    h = matmul_residual(o.reshape(n, d), w_o_a[0].astype(BF16), x2, tm=tm, tn=tn)
    h = mlp_block(h, row(mlp_norm_g[0]), w_mlp_in[0].astype(BF16), w_mlp_out[0].astype(BF16),
                  row(final_norm_g), tm=tm, tf=tf, final_norm=False)

    gkv = row(kv_norm_g)
    w_f_t = jnp.zeros((LANES, d), F32)
    b_f = jnp.zeros((LANES, 1), F32)
    for p in range(C_PIECES):
        w_f_t = w_f_t.at[p * fox_heads:(p + 1) * fox_heads].set(w_f_b.T.astype(F32))
        b_f = b_f.at[p * fox_heads:(p + 1) * fox_heads, 0].set(b_f_b.astype(F32))
    kc, c_cols = forget_scan(h.reshape(b, s, d), gkv, w_f_t, b_f, nheads=fox_heads, ts=ts)
    k = norm_matmul(h, gkv, w_k_b.astype(BF16), tm=tm, tn=tn)
    vt = norm_matmul(h, gkv, w_v_b.T.astype(BF16), tm=tm, tn=tn, transpose_out=True)

    q = norm_matmul(h, row(attn_norm_g[1]), w_q_b[0].astype(BF16), tm=tm, tn=tn, scaled_cols=d)
    o = fox_attention(q.reshape(b, s, d), k.reshape(b, s, d), vt, kc, c_cols, nheads=fox_heads, t=t_attn)
    h = matmul_residual(o.reshape(n, d), w_o_b[0].astype(BF16), h, tm=tm, tn=tn)
    h = mlp_block(h, row(mlp_norm_g[1]), w_mlp_in[1].astype(BF16), w_mlp_out[1].astype(BF16),
                  row(final_norm_g), tm=tm, tf=tf, final_norm=True)
    return h.reshape(b, s, d)
```

```python
import functools
import math

import jax
import jax.numpy as jnp
from jax import lax
from jax.experimental import pallas as pl
from jax.experimental.pallas import tpu as pltpu

HEAD_DIM = 128
CHUNK = 64
REL_BUCKETS = 32
REL_MAX_DIST = 128
NORM_EPS = 1e-6
SUBLN_EPS = 1e-5
NEG = -1e30
SCALE = HEAD_DIM ** -0.5
LOG2E = math.log2(math.e)
Q_SCALE = SCALE * LOG2E
M_INIT = -1e30
FAR_UNROLL = 4
ONES_ROWS = 16
PRUNE_LOG2 = -170.0

V7X_VMEM_LIMIT_BYTES = 56 * 1024 * 1024
LANES = 128
C_PIECES = 3

F32 = jnp.float32
BF16 = jnp.bfloat16


def _params(*semantics):
    return pltpu.CompilerParams(dimension_semantics=semantics, vmem_limit_bytes=V7X_VMEM_LIMIT_BYTES)


def _rms_scale(x, eps):
    return lax.rsqrt(jnp.mean(x * x, axis=-1, keepdims=True) + eps)


def _dot(a, b):
    return jnp.dot(a, b, preferred_element_type=F32)


def _dot_nt(a, b):
    return lax.dot_general(a, b, (((1,), (1,)), ((), ())), preferred_element_type=F32)


def _split_bf16(x, terms):
    pieces = []
    for _ in range(terms - 1):
        p = x.astype(BF16).astype(x.dtype)
        pieces.append(p)
        x = x - p
    pieces.append(x.astype(BF16).astype(x.dtype))
    return pieces


def _norm_matmul_kernel(x_ref, g_ref, w_ref, o_ref, hn_ref, *, transpose_out, scaled_blocks):
    @pl.when(pl.program_id(1) == 0)
    def _():
        x = x_ref[...]
        hn_ref[...] = (x * _rms_scale(x, NORM_EPS) * g_ref[...]).astype(hn_ref.dtype)

    if transpose_out:
        y = _dot_nt(w_ref[...], hn_ref[...])
    else:
        y = _dot(hn_ref[...], w_ref[...])
    if scaled_blocks:
        y = y * jnp.where(pl.program_id(1) < scaled_blocks, Q_SCALE, 1.0)
    o_ref[...] = y.astype(o_ref.dtype)


def norm_matmul(x, g, w, *, tm, tn, transpose_out=False, scaled_cols=0):
    assert scaled_cols % tn == 0
    n, d = x.shape
    if transpose_out:
        nout = w.shape[0]
        w_spec = pl.BlockSpec((tn, d), lambda i, j: (j, 0))
        out_shape = jax.ShapeDtypeStruct((nout, n), BF16)
        out_spec = pl.BlockSpec((tn, tm), lambda i, j: (j, i))
    else:
        nout = w.shape[1]
        w_spec = pl.BlockSpec((d, tn), lambda i, j: (0, j))
        out_shape = jax.ShapeDtypeStruct((n, nout), BF16)
        out_spec = pl.BlockSpec((tm, tn), lambda i, j: (i, j))
    return pl.pallas_call(
        functools.partial(_norm_matmul_kernel, transpose_out=transpose_out, scaled_blocks=scaled_cols // tn),
        out_shape=out_shape,
        grid=(n // tm, nout // tn),
        in_specs=[
            pl.BlockSpec((tm, d), lambda i, j: (i, 0)),
            pl.BlockSpec((1, d), lambda i, j: (0, 0)),
            w_spec,
        ],
        out_specs=out_spec,
        scratch_shapes=[pltpu.VMEM((tm, d), BF16)],
        compiler_params=_params("parallel", "arbitrary"),
        name="norm_matmul_t" if transpose_out else "norm_matmul",
    )(x, g, w)


def _matmul_residual_kernel(a_ref, w_ref, r_ref, o_ref):
    o_ref[...] = r_ref[...] + _dot(a_ref[...], w_ref[...])


def matmul_residual(a, w, res, *, tm, tn):
    n, k = a.shape
    nout = w.shape[1]
    return pl.pallas_call(
        _matmul_residual_kernel,
        out_shape=jax.ShapeDtypeStruct((n, nout), F32),
        grid=(n // tm, nout // tn),
        in_specs=[
            pl.BlockSpec((tm, k), lambda i, j: (i, 0)),
            pl.BlockSpec((k, tn), lambda i, j: (0, j)),
            pl.BlockSpec((tm, tn), lambda i, j: (i, j)),
        ],
        out_specs=pl.BlockSpec((tm, tn), lambda i, j: (i, j)),
        compiler_params=_params("parallel", "arbitrary"),
        name="matmul_residual",
    )(a, w, res)


def _mlp_kernel(x_ref, g_ref, win_ref, wout_ref, gf_ref, o_ref, hn_ref, acc_ref, *, final_norm):
    f = pl.program_id(1)

    @pl.when(f == 0)
    def _():
        x = x_ref[...]
        hn_ref[...] = (x * _rms_scale(x, NORM_EPS) * g_ref[...]).astype(hn_ref.dtype)
        acc_ref[...] = jnp.zeros_like(acc_ref)

    u = _dot(hn_ref[...], win_ref[...])
    u = jnp.square(jnp.maximum(u, 0.0)).astype(BF16)
    acc_ref[...] += _dot(u, wout_ref[...])

    @pl.when(f == pl.num_programs(1) - 1)
    def _():
        y = x_ref[...] + acc_ref[...]
        if final_norm:
            y = y * _rms_scale(y, NORM_EPS) * gf_ref[...]
        o_ref[...] = y


def mlp_block(x, g, w_in, w_out, g_final, *, tm, tf, final_norm):
    n, d = x.shape
    dff = w_in.shape[1]
    return pl.pallas_call(
        functools.partial(_mlp_kernel, final_norm=final_norm),
        out_shape=jax.ShapeDtypeStruct((n, d), F32),
        grid=(n // tm, dff // tf),
        in_specs=[
            pl.BlockSpec((tm, d), lambda i, f: (i, 0)),
            pl.BlockSpec((1, d), lambda i, f: (0, 0)),
            pl.BlockSpec((d, tf), lambda i, f: (0, f)),
            pl.BlockSpec((tf, d), lambda i, f: (f, 0)),
            pl.BlockSpec((1, d), lambda i, f: (0, 0)),
        ],
        out_specs=pl.BlockSpec((tm, d), lambda i, f: (i, 0)),
        scratch_shapes=[pltpu.VMEM((tm, d), BF16), pltpu.VMEM((tm, d), F32)],
        compiler_params=_params("parallel", "arbitrary"),
        name="mlp_block",
    )(x, g, w_in, w_out, g_final)


def _t5_bias_kernel(table_ref, o_ref, *, t):
    h = pl.program_id(0)
    half = REL_BUCKETS // 2
    max_exact = half // 2
    ki = lax.broadcasted_iota(jnp.int32, (t, t), 0)
    qi = lax.broadcasted_iota(jnp.int32, (t, t), 1)
    far = table_ref[half - 1, h]
    for d in range(2):
        rel = ki - d * t - qi
        ret = jnp.where(rel > 0, half, 0)
        n = jnp.abs(rel)
        nf = jnp.maximum(n, 1).astype(F32)
        large = max_exact + (jnp.log(nf / max_exact) / math.log(REL_MAX_DIST / max_exact)
                             * (half - max_exact)).astype(jnp.int32)
        large = jnp.minimum(large, half - 1)
        bucket = ret + jnp.where(n < max_exact, n, large)
        val = jnp.zeros((t, t), F32)
        for b in range(REL_BUCKETS):
            val = jnp.where(bucket == b, table_ref[b, h], val)
        val = (val - far) * LOG2E
        if d == 0:
            mask = (ki >> 6) <= (qi >> 6)
            val = jnp.where(mask, val, NEG)
        o_ref[0, d] = val


def t5_bias_tiles(rel_table, *, t):
    nheads = rel_table.shape[1]
    assert CHUNK == 64 and t % CHUNK == 0 and t >= REL_MAX_DIST
    return pl.pallas_call(
        functools.partial(_t5_bias_kernel, t=t),
        out_shape=jax.ShapeDtypeStruct((nheads, 2, t, t), F32),
        grid=(nheads,),
        in_specs=[pl.BlockSpec(memory_space=pltpu.SMEM)],
        out_specs=pl.BlockSpec((1, 2, t, t), lambda h: (h, 0, 0, 0)),
        compiler_params=_params("arbitrary"),
        name="t5_bias_tiles",
    )(rel_table)


def _init_stats(m_ref, l_ref, acc_ref):
    m_ref[...] = jnp.full_like(m_ref, M_INIT)
    if l_ref is not None:
        l_ref[...] = jnp.zeros_like(l_ref)
    acc_ref[...] = jnp.zeros_like(acc_ref)


def _online_softmax_step(z, vt, m_ref, l_ref, acc_ref):
    m_prev = m_ref[...]
    m_new = jnp.maximum(m_prev, jnp.max(z, axis=0, keepdims=True))
    p = jnp.exp2(z - m_new)
    alpha = jnp.exp2(m_prev - m_new)
    if l_ref is not None:
        l_ref[...] = alpha * l_ref[...] + jnp.sum(p, axis=0, keepdims=True)
    acc_ref[...] = alpha * acc_ref[...] + _dot(vt, p.astype(BF16))
    m_ref[...] = m_new


def _far_loop(consume, j_start, j_end):
    def body(u):
        def run(jj, carry):
            for r in range(u):
                consume(j_start + jj * u + r, j_start + jj * u + r + 1)
            return carry
        return run

    n_far = j_end - j_start
    n_main = n_far // FAR_UNROLL
    lax.fori_loop(0, n_main, body(FAR_UNROLL), 0)
    lax.fori_loop(n_main * FAR_UNROLL, n_far, body(1), 0)


def _diff_attn_kernel(q_ref, k_ref, vt_ref, bias_ref, lamv_ref, g_ref, o_ref,
                      m0, l0, a0, m1, l1, a1, z_ref, *, t, lambda_init):
    i = pl.program_id(2)
    stats = ((m0, l0, a0), (m1, l1, a1))
    for s in stats:
        _init_stats(*s)
    q = q_ref[0]

    def scores(j):
        start = pl.multiple_of(j * t, t)
        k = k_ref[0, pl.ds(start, t), :]
        for c in range(2):
            half = slice(c * HEAD_DIM, (c + 1) * HEAD_DIM)
            z_ref[c] = _dot_nt(k[:, half], q[:, half])

    def consume(j, bias, j_next):
        zs = [z_ref[c] for c in range(2)]
        if j_next is not None:
            scores(j_next)
        start = pl.multiple_of(j * t, t)
        vt = vt_ref[:, pl.ds(start, t)]
        for z, s in zip(zs, stats):
            if bias is not None:
                z = z + bias
            _online_softmax_step(z, vt, *s)

    scores(0)
    _far_loop(lambda j, j_next: consume(j, None, j_next), 0, jnp.maximum(i - 1, 0))

    @pl.when(i >= 1)
    def _():
        consume(i - 1, bias_ref[0, 1], i)

    consume(i, bias_ref[0, 0], None)

    lamv = lamv_ref[...]
    lam = (jnp.exp(jnp.sum(lamv[0:1] * lamv[1:2], axis=1, keepdims=True))
           - jnp.exp(jnp.sum(lamv[2:3] * lamv[3:4], axis=1, keepdims=True)) + lambda_init)
    o = a0[...] * (1.0 / l0[...]) - lam * (a1[...] * (1.0 / l1[...]))
    ms = jnp.mean(o * o, axis=0, keepdims=True)
    o = o * lax.rsqrt(ms + SUBLN_EPS) * g_ref[...] * (1.0 - lambda_init)
    o_ref[0] = o.T.astype(o_ref.dtype)


def diff_attention(qk, vt, bias_tiles, lamv, subln_g_col, *, batch, nheads, t, lambda_init):
    b, s, d2 = qk.shape
    d = d2 // 2
    hw = 2 * HEAD_DIM
    assert d == nheads * hw and s % t == 0 and b == batch
    kernel = functools.partial(_diff_attn_kernel, t=t, lambda_init=lambda_init)
    stat = lambda: pltpu.VMEM((1, t), F32)
    acc = lambda: pltpu.VMEM((hw, t), F32)
    return pl.pallas_call(
        kernel,
        out_shape=jax.ShapeDtypeStruct((b, s, d), BF16),
        grid=(b, nheads, s // t),
        in_specs=[
            pl.BlockSpec((1, t, hw), lambda bi, h, i: (bi, i, h)),
            pl.BlockSpec((1, s, hw), lambda bi, h, i: (bi, 0, nheads + h)),
            pl.BlockSpec((hw, s), lambda bi, h, i: (h, bi)),
            pl.BlockSpec((1, 2, t, t), lambda bi, h, i: (h, 0, 0, 0)),
            pl.BlockSpec((4, HEAD_DIM), lambda bi, h, i: (0, 0)),
            pl.BlockSpec((hw, 1), lambda bi, h, i: (0, 0)),
        ],
        out_specs=pl.BlockSpec((1, t, hw), lambda bi, h, i: (bi, i, h)),
        scratch_shapes=[stat(), stat(), acc(), stat(), stat(), acc(), pltpu.VMEM((2, t, t), F32)],
        compiler_params=_params("parallel", "parallel", "arbitrary"),
        name="diff_attention",
    )(qk, qk, vt, bias_tiles, lamv, subln_g_col)


def _row_sq_norm_max(x):
    xf = x.astype(F32)
    return jnp.max(jnp.sum(xf * xf, axis=1, keepdims=True))


def _fox_attn_kernel(cend_ref, q_ref, k_ref, vt_ref, kc_ref, cq_ref, o_ref, m_ref, a_ref, z_ref, kn_ref,
                     *, t, nheads):
    bi = pl.program_id(0)
    h = pl.program_id(1)
    i = pl.program_id(2)
    nk = pl.num_programs(2)
    _init_stats(m_ref, None, a_ref)

    @pl.when(i == 0)
    def _():
        def norm_body(j, running):
            start = pl.multiple_of(j * t, t)
            running = jnp.maximum(running, _row_sq_norm_max(k_ref[0, pl.ds(start, t), :]))
            kn_ref[j] = running
            return running

        lax.fori_loop(0, nk, norm_body, jnp.float32(0.0))

    cq = cq_ref[0]
    lane = lax.broadcasted_iota(jnp.int32, cq.shape, 1)
    ct = jnp.sum(jnp.where(lane == h, cq, 0.0), axis=1, keepdims=True) * LOG2E
    npc = C_PIECES * nheads
    qc = jnp.where(lane < npc, jnp.where((lane & (nheads - 1)) == h, 1.0, 0.0), 0.0)
    for p, piece in enumerate(_split_bf16(ct, C_PIECES)):
        qc = jnp.where(lane == npc + p, piece, qc)
    q = jnp.concatenate([q_ref[0], qc.astype(BF16)], axis=1)

    q2 = _row_sq_norm_max(q_ref[0])
    k_diag = k_ref[0, pl.ds(pl.multiple_of(i * t, t), t), :]
    m_lo = jnp.min(jnp.sum(q_ref[0].astype(F32) * k_diag.astype(F32), axis=1, keepdims=True))
    c_hi = jnp.max(ct)

    def count_body(j, n_skip):
        r = PRUNE_LOG2 + m_lo - (c_hi - cend_ref[(bi * nk + j) * nheads + h] * LOG2E)
        skippable = jnp.logical_and(r > 0.0, q2 * kn_ref[j] < r * r)
        return jnp.where(jnp.logical_and(skippable, n_skip == j), n_skip + 1, n_skip)

    j_start = lax.fori_loop(0, i, count_body, jnp.int32(0))

    def scores(j):
        start = pl.multiple_of(j * t, t)
        k = jnp.concatenate([k_ref[0, pl.ds(start, t), :], kc_ref[0, pl.ds(start, t), :]], axis=1)
        z_ref[...] = _dot_nt(k, q)

    def consume(j, diagonal, j_next):
        z = z_ref[...]
        if j_next is not None:
            scores(j_next)
        start = pl.multiple_of(j * t, t)
        vt = jnp.concatenate([vt_ref[:, pl.ds(start, t)], jnp.ones((ONES_ROWS, t), BF16)], axis=0)
        if diagonal:
            ki = lax.broadcasted_iota(jnp.int32, (t, t), 0)
            qi = lax.broadcasted_iota(jnp.int32, (t, t), 1)
            z = jnp.where(ki <= qi, z, NEG)
        _online_softmax_step(z, vt, m_ref, None, a_ref)

    scores(j_start)
    _far_loop(lambda j, j_next: consume(j, False, j_next), j_start, i)
    consume(i, True, None)
    o = a_ref[:HEAD_DIM] * (1.0 / a_ref[HEAD_DIM:HEAD_DIM + 1])
    o_ref[0] = o.T.astype(o_ref.dtype)


def fox_attention(q, k, vt, kc, c_cols, c_end, *, nheads, t):
    b, s, d = q.shape
    assert d == nheads * HEAD_DIM and s % t == 0 and c_end.shape == (b * (s // t) * nheads,)
    assert nheads & (nheads - 1) == 0 and C_PIECES * nheads + C_PIECES <= LANES
    return pl.pallas_call(
        functools.partial(_fox_attn_kernel, t=t, nheads=nheads),
        out_shape=jax.ShapeDtypeStruct((b, s, d), BF16),
        grid=(b, nheads, s // t),
        in_specs=[
            pl.BlockSpec(memory_space=pltpu.SMEM),
            pl.BlockSpec((1, t, HEAD_DIM), lambda bi, h, i: (bi, i, h)),
            pl.BlockSpec((1, s, HEAD_DIM), lambda bi, h, i: (bi, 0, h)),
            pl.BlockSpec((HEAD_DIM, s), lambda bi, h, i: (h, bi)),
            pl.BlockSpec((1, s, LANES), lambda bi, h, i: (bi, 0, 0)),
            pl.BlockSpec((1, t, LANES), lambda bi, h, i: (bi, i, 0)),
        ],
        out_specs=pl.BlockSpec((1, t, HEAD_DIM), lambda bi, h, i: (bi, i, h)),
        scratch_shapes=[pltpu.VMEM((1, t), F32), pltpu.VMEM((HEAD_DIM + ONES_ROWS, t), F32),
                        pltpu.VMEM((t, t), F32), pltpu.SMEM((s // t,), F32)],
        compiler_params=_params("parallel", "parallel", "arbitrary"),
        name="fox_attention",
    )(c_end, q, k, vt, kc, c_cols)


def _forget_scan_kernel(x_ref, g_ref, wt_ref, b_ref, kc_ref, ccol_ref, carry_ref, *, ts, nheads):
    @pl.when(pl.program_id(1) == 0)
    def _():
        carry_ref[...] = jnp.zeros_like(carry_ref)

    x = x_ref[0]
    hn = x * _rms_scale(x, NORM_EPS) * g_ref[...]
    hn_hi, hn_lo = (p.astype(BF16) for p in _split_bf16(hn, 2))
    w_hi, w_lo = (p.astype(BF16) for p in _split_bf16(wt_ref[...], 2))
    f = _dot_nt(w_hi, hn_hi) + _dot_nt(w_hi, hn_lo) + _dot_nt(w_lo, hn_hi) + b_ref[...]
    logf = -(jnp.maximum(-f, 0.0) + jnp.log1p(jnp.exp(-jnp.abs(f))))
    row = lax.broadcasted_iota(jnp.int32, (ts, ts), 0)
    col = lax.broadcasted_iota(jnp.int32, (ts, ts), 1)
    upper = jnp.where(row <= col, 1.0, 0.0).astype(BF16)
    c = carry_ref[:, 0:1]
    for piece in _split_bf16(logf, C_PIECES):
        c = c + _dot(piece.astype(BF16), upper)
    carry_ref[...] = jnp.broadcast_to(c[:, ts - 1:ts], carry_ref.shape)
    ccol_ref[0] = c.T

    hrow = lax.broadcasted_iota(jnp.int32, c.shape, 0)
    npc = C_PIECES * nheads
    aug = jnp.where(hrow < npc + C_PIECES, 1.0, 0.0)
    for p, piece in reversed(list(enumerate(_split_bf16(c * (-LOG2E), C_PIECES)))):
        aug = jnp.where(hrow < (p + 1) * nheads, piece, aug)
    kc_ref[0] = aug.T.astype(BF16)


def forget_scan(h, g, w_f_t, b_f, *, nheads, ts):
    b, s, d = h.shape
    kernel = functools.partial(_forget_scan_kernel, ts=ts, nheads=nheads)
    return pl.pallas_call(
        kernel,
        out_shape=(jax.ShapeDtypeStruct((b, s, LANES), BF16), jax.ShapeDtypeStruct((b, s, LANES), F32)),
        grid=(b, s // ts),
        in_specs=[
            pl.BlockSpec((1, ts, d), lambda bi, i: (bi, i, 0)),
            pl.BlockSpec((1, d), lambda bi, i: (0, 0)),
            pl.BlockSpec((LANES, d), lambda bi, i: (0, 0)),
            pl.BlockSpec((LANES, 1), lambda bi, i: (0, 0)),
        ],
        out_specs=(pl.BlockSpec((1, ts, LANES), lambda bi, i: (bi, i, 0)),
                   pl.BlockSpec((1, ts, LANES), lambda bi, i: (bi, i, 0))),
        scratch_shapes=[pltpu.VMEM((LANES, LANES), F32)],
        compiler_params=_params("parallel", "arbitrary"),
        name="forget_scan",
    )(h, g, w_f_t, b_f)


def _pick(n, target):
    t = min(n, target)
    while n % t:
        t //= 2
    return t


def kernel(x, rel_bias_table, attn_norm_g, mlp_norm_g, w_qkv_a, lam_q1, lam_k1, lam_q2, lam_k2, subln_g,
           w_o_a, kv_norm_g, w_k_b, w_v_b, w_f_b, b_f_b, w_q_b, w_o_b, w_mlp_in, w_mlp_out, final_norm_g):
    b, s, d = x.shape
    n = b * s
    depth = attn_norm_g.shape[0]
    assert depth == 2 and w_qkv_a.shape[0] == 1 and w_q_b.shape[0] == 1
    diff_heads = d // (2 * HEAD_DIM)
    fox_heads = d // HEAD_DIM
    dff = w_mlp_in.shape[2]

    t_attn = _pick(s, 512)
    tm = _pick(n, 512)
    tn = _pick(d, 1024)
    tf = _pick(dff, 512)
    ts = _pick(s, 512)

    row = lambda v: v.reshape(1, -1).astype(F32)
    x2 = x.reshape(n, d)

    bias_tiles = t5_bias_tiles(rel_bias_table.astype(F32), t=t_attn)
    g0 = row(attn_norm_g[0])
    qk = norm_matmul(x2, g0, w_qkv_a[0, :, :2 * d].astype(BF16), tm=tm, tn=tn, scaled_cols=d)
    vt = norm_matmul(x2, g0, w_qkv_a[0, :, 2 * d:].T.astype(BF16), tm=tm, tn=tn, transpose_out=True)
    lamv = jnp.stack([lam_q1[0], lam_k1[0], lam_q2[0], lam_k2[0]]).astype(F32)
    lambda_init = 0.8 - 0.6 * math.exp(-0.3 * 0)
    o = diff_attention(qk.reshape(b, s, 2 * d), vt, bias_tiles, lamv, subln_g[0].reshape(-1, 1).astype(F32),
                       batch=b, nheads=diff_heads, t=t_attn, lambda_init=lambda_init)
    h = matmul_residual(o.reshape(n, d), w_o_a[0].astype(BF16), x2, tm=tm, tn=tn)
    h = mlp_block(h, row(mlp_norm_g[0]), w_mlp_in[0].astype(BF16), w_mlp_out[0].astype(BF16),
                  row(final_norm_g), tm=tm, tf=tf, final_norm=False)

    gkv = row(kv_norm_g)
    w_f_t = jnp.zeros((LANES, d), F32)
    b_f = jnp.zeros((LANES, 1), F32)
    for p in range(C_PIECES):
        w_f_t = w_f_t.at[p * fox_heads:(p + 1) * fox_heads].set(w_f_b.T.astype(F32))
        b_f = b_f.at[p * fox_heads:(p + 1) * fox_heads, 0].set(b_f_b.astype(F32))
    kc, c_cols = forget_scan(h.reshape(b, s, d), gkv, w_f_t, b_f, nheads=fox_heads, ts=ts)
    k = norm_matmul(h, gkv, w_k_b.astype(BF16), tm=tm, tn=tn)
    vt = norm_matmul(h, gkv, w_v_b.T.astype(BF16), tm=tm, tn=tn, transpose_out=True)

    q = norm_matmul(h, row(attn_norm_g[1]), w_q_b[0].astype(BF16), tm=tm, tn=tn, scaled_cols=d)
    c_end = c_cols[:, t_attn - 1::t_attn, :fox_heads].reshape(-1)
    o = fox_attention(q.reshape(b, s, d), k.reshape(b, s, d), vt, kc, c_cols, c_end, nheads=fox_heads, t=t_attn)
    h = matmul_residual(o.reshape(n, d), w_o_b[0].astype(BF16), h, tm=tm, tn=tn)
    h = mlp_block(h, row(mlp_norm_g[1]), w_mlp_in[1].astype(BF16), w_mlp_out[1].astype(BF16),
                  row(final_norm_g), tm=tm, tf=tf, final_norm=True)
    return h.reshape(b, s, d)
```

```python
import functools
import math

import jax
import jax.numpy as jnp
from jax import lax
from jax.experimental import pallas as pl
from jax.experimental.pallas import tpu as pltpu

HEAD_DIM = 128
CHUNK = 64
REL_BUCKETS = 32
REL_MAX_DIST = 128
NORM_EPS = 1e-6
SUBLN_EPS = 1e-5
NEG = -1e30
SCALE = HEAD_DIM ** -0.5
LOG2E = math.log2(math.e)
Q_SCALE = SCALE * LOG2E
M_INIT = -1e30
FAR_UNROLL = 4
ONES_ROWS = 16
PRUNE_LOG2 = -170.0
FOX_HEADS_PER_STEP = 2

V7X_VMEM_LIMIT_BYTES = 56 * 1024 * 1024
LANES = 128
C_PIECES = 3

F32 = jnp.float32
BF16 = jnp.bfloat16


def _params(*semantics):
    return pltpu.CompilerParams(dimension_semantics=semantics, vmem_limit_bytes=V7X_VMEM_LIMIT_BYTES)


def _rms_scale(x, eps):
    return lax.rsqrt(jnp.mean(x * x, axis=-1, keepdims=True) + eps)


def _dot(a, b):
    return jnp.dot(a, b, preferred_element_type=F32)


def _dot_nt(a, b):
    return lax.dot_general(a, b, (((1,), (1,)), ((), ())), preferred_element_type=F32)


def _split_bf16(x, terms):
    pieces = []
    for _ in range(terms - 1):
        p = x.astype(BF16).astype(x.dtype)
        pieces.append(p)
        x = x - p
    pieces.append(x.astype(BF16).astype(x.dtype))
    return pieces


def _norm_matmul_kernel(x_ref, g_ref, w_ref, o_ref, hn_ref, *, transpose_out, scaled_blocks):
    @pl.when(pl.program_id(1) == 0)
    def _():
        x = x_ref[...]
        hn_ref[...] = (x * _rms_scale(x, NORM_EPS) * g_ref[...]).astype(hn_ref.dtype)

    if transpose_out:
        y = _dot_nt(w_ref[...], hn_ref[...])
    else:
        y = _dot(hn_ref[...], w_ref[...])
    if scaled_blocks:
        y = y * jnp.where(pl.program_id(1) < scaled_blocks, Q_SCALE, 1.0)
    o_ref[...] = y.astype(o_ref.dtype)


def norm_matmul(x, g, w, *, tm, tn, transpose_out=False, scaled_cols=0):
    assert scaled_cols % tn == 0
    n, d = x.shape
    if transpose_out:
        nout = w.shape[0]
        w_spec = pl.BlockSpec((tn, d), lambda i, j: (j, 0))
        out_shape = jax.ShapeDtypeStruct((nout, n), BF16)
        out_spec = pl.BlockSpec((tn, tm), lambda i, j: (j, i))
    else:
        nout = w.shape[1]
        w_spec = pl.BlockSpec((d, tn), lambda i, j: (0, j))
        out_shape = jax.ShapeDtypeStruct((n, nout), BF16)
        out_spec = pl.BlockSpec((tm, tn), lambda i, j: (i, j))
    return pl.pallas_call(
        functools.partial(_norm_matmul_kernel, transpose_out=transpose_out, scaled_blocks=scaled_cols // tn),
        out_shape=out_shape,
        grid=(n // tm, nout // tn),
        in_specs=[
            pl.BlockSpec((tm, d), lambda i, j: (i, 0)),
            pl.BlockSpec((1, d), lambda i, j: (0, 0)),
            w_spec,
        ],
        out_specs=out_spec,
        scratch_shapes=[pltpu.VMEM((tm, d), BF16)],
        compiler_params=_params("parallel", "arbitrary"),
        name="norm_matmul_t" if transpose_out else "norm_matmul",
    )(x, g, w)


def _matmul_residual_kernel(a_ref, w_ref, r_ref, o_ref):
    o_ref[...] = r_ref[...] + _dot(a_ref[...], w_ref[...])


def matmul_residual(a, w, res, *, tm, tn):
    n, k = a.shape
    nout = w.shape[1]
    return pl.pallas_call(
        _matmul_residual_kernel,
        out_shape=jax.ShapeDtypeStruct((n, nout), F32),
        grid=(n // tm, nout // tn),
        in_specs=[
            pl.BlockSpec((tm, k), lambda i, j: (i, 0)),
            pl.BlockSpec((k, tn), lambda i, j: (0, j)),
            pl.BlockSpec((tm, tn), lambda i, j: (i, j)),
        ],
        out_specs=pl.BlockSpec((tm, tn), lambda i, j: (i, j)),
        compiler_params=_params("parallel", "arbitrary"),
        name="matmul_residual",
    )(a, w, res)


def _mlp_kernel(x_ref, g_ref, win_ref, wout_ref, gf_ref, o_ref, hn_ref, *, final_norm):
    f = pl.program_id(1)

    @pl.when(f == 0)
    def _():
        x = x_ref[...]
        hn_ref[...] = (x * _rms_scale(x, NORM_EPS) * g_ref[...]).astype(hn_ref.dtype)
        o_ref[...] = x

    u = _dot(hn_ref[...], win_ref[...])
    u = jnp.square(jnp.maximum(u, 0.0)).astype(BF16)
    o_ref[...] += _dot(u, wout_ref[...])

    if final_norm:
        @pl.when(f == pl.num_programs(1) - 1)
        def _():
            y = o_ref[...]
            o_ref[...] = y * _rms_scale(y, NORM_EPS) * gf_ref[...]


def mlp_block(x, g, w_in, w_out, g_final, *, tm, tf, final_norm):
    n, d = x.shape
    dff = w_in.shape[1]
    return pl.pallas_call(
        functools.partial(_mlp_kernel, final_norm=final_norm),
        out_shape=jax.ShapeDtypeStruct((n, d), F32),
        grid=(n // tm, dff // tf),
        in_specs=[
            pl.BlockSpec((tm, d), lambda i, f: (i, 0)),
            pl.BlockSpec((1, d), lambda i, f: (0, 0)),
            pl.BlockSpec((d, tf), lambda i, f: (0, f)),
            pl.BlockSpec((tf, d), lambda i, f: (f, 0)),
            pl.BlockSpec((1, d), lambda i, f: (0, 0)),
        ],
        out_specs=pl.BlockSpec((tm, d), lambda i, f: (i, 0)),
        scratch_shapes=[pltpu.VMEM((tm, d), BF16)],
        compiler_params=_params("parallel", "arbitrary"),
        name="mlp_block",
    )(x, g, w_in, w_out, g_final)


def _t5_bias_kernel(table_ref, o_ref, *, t):
    h = pl.program_id(0)
    half = REL_BUCKETS // 2
    max_exact = half // 2
    ki = lax.broadcasted_iota(jnp.int32, (t, t), 0)
    qi = lax.broadcasted_iota(jnp.int32, (t, t), 1)
    far = table_ref[half - 1, h]
    for d in range(2):
        rel = ki - d * t - qi
        ret = jnp.where(rel > 0, half, 0)
        n = jnp.abs(rel)
        nf = jnp.maximum(n, 1).astype(F32)
        large = max_exact + (jnp.log(nf / max_exact) / math.log(REL_MAX_DIST / max_exact)
                             * (half - max_exact)).astype(jnp.int32)
        large = jnp.minimum(large, half - 1)
        bucket = ret + jnp.where(n < max_exact, n, large)
        val = jnp.zeros((t, t), F32)
        for b in range(REL_BUCKETS):
            val = jnp.where(bucket == b, table_ref[b, h], val)
        val = (val - far) * LOG2E
        if d == 0:
            mask = (ki >> 6) <= (qi >> 6)
            val = jnp.where(mask, val, NEG)
        o_ref[0, d] = val


def t5_bias_tiles(rel_table, *, t):
    nheads = rel_table.shape[1]
    assert CHUNK == 64 and t % CHUNK == 0 and t >= REL_MAX_DIST
    return pl.pallas_call(
        functools.partial(_t5_bias_kernel, t=t),
        out_shape=jax.ShapeDtypeStruct((nheads, 2, t, t), F32),
        grid=(nheads,),
        in_specs=[pl.BlockSpec(memory_space=pltpu.SMEM)],
        out_specs=pl.BlockSpec((1, 2, t, t), lambda h: (h, 0, 0, 0)),
        compiler_params=_params("arbitrary"),
        name="t5_bias_tiles",
    )(rel_table)


def _init_stats(m_ref, l_ref, acc_ref):
    m_ref[...] = jnp.full_like(m_ref, M_INIT)
    if l_ref is not None:
        l_ref[...] = jnp.zeros_like(l_ref)
    acc_ref[...] = jnp.zeros_like(acc_ref)


def _online_softmax_step(z, vt, m_ref, l_ref, acc_ref):
    m_prev = m_ref[...]
    m_new = jnp.maximum(m_prev, jnp.max(z, axis=0, keepdims=True))
    p = jnp.exp2(z - m_new)
    alpha = jnp.exp2(m_prev - m_new)
    if l_ref is not None:
        l_ref[...] = alpha * l_ref[...] + jnp.sum(p, axis=0, keepdims=True)
    acc_ref[...] = alpha * acc_ref[...] + _dot(vt, p.astype(BF16))
    m_ref[...] = m_new


def _far_loop(consume, j_start, j_end):
    def body(u):
        def run(jj, carry):
            for r in range(u):
                consume(j_start + jj * u + r, j_start + jj * u + r + 1)
            return carry
        return run

    n_far = j_end - j_start
    n_main = n_far // FAR_UNROLL
    lax.fori_loop(0, n_main, body(FAR_UNROLL), 0)
    lax.fori_loop(n_main * FAR_UNROLL, n_far, body(1), 0)


def _diff_attn_kernel(q_ref, k_ref, vt_ref, bias_ref, lamv_ref, g_ref, o_ref,
                      m0, l0, a0, m1, l1, a1, z_ref, *, t, lambda_init):
    i = pl.program_id(2)
    stats = ((m0, l0, a0), (m1, l1, a1))
    for s in stats:
        _init_stats(*s)
    q = q_ref[0]

    def scores(j):
        start = pl.multiple_of(j * t, t)
        k = k_ref[0, pl.ds(start, t), :]
        for c in range(2):
            half = slice(c * HEAD_DIM, (c + 1) * HEAD_DIM)
            z_ref[c] = _dot_nt(k[:, half], q[:, half])

    def consume(j, bias, j_next):
        zs = [z_ref[c] for c in range(2)]
        if j_next is not None:
            scores(j_next)
        start = pl.multiple_of(j * t, t)
        vt = vt_ref[:, pl.ds(start, t)]
        for z, s in zip(zs, stats):
            if bias is not None:
                z = z + bias
            _online_softmax_step(z, vt, *s)

    scores(0)
    _far_loop(lambda j, j_next: consume(j, None, j_next), 0, jnp.maximum(i - 1, 0))

    @pl.when(i >= 1)
    def _():
        consume(i - 1, bias_ref[0, 1], i)

    consume(i, bias_ref[0, 0], None)

    lamv = lamv_ref[...]
    lam = (jnp.exp(jnp.sum(lamv[0:1] * lamv[1:2], axis=1, keepdims=True))
           - jnp.exp(jnp.sum(lamv[2:3] * lamv[3:4], axis=1, keepdims=True)) + lambda_init)
    o = a0[...] * (1.0 / l0[...]) - lam * (a1[...] * (1.0 / l1[...]))
    ms = jnp.mean(o * o, axis=0, keepdims=True)
    o = o * lax.rsqrt(ms + SUBLN_EPS) * g_ref[...] * (1.0 - lambda_init)
    o_ref[0] = o.T.astype(o_ref.dtype)


def diff_attention(qk, vt, bias_tiles, lamv, subln_g_col, *, batch, nheads, t, lambda_init):
    b, s, d2 = qk.shape
    d = d2 // 2
    hw = 2 * HEAD_DIM
    assert d == nheads * hw and s % t == 0 and b == batch
    kernel = functools.partial(_diff_attn_kernel, t=t, lambda_init=lambda_init)
    stat = lambda: pltpu.VMEM((1, t), F32)
    acc = lambda: pltpu.VMEM((hw, t), F32)
    return pl.pallas_call(
        kernel,
        out_shape=jax.ShapeDtypeStruct((b, s, d), BF16),
        grid=(b, nheads, s // t),
        in_specs=[
            pl.BlockSpec((1, t, hw), lambda bi, h, i: (bi, i, h)),
            pl.BlockSpec((1, s, hw), lambda bi, h, i: (bi, 0, nheads + h)),
            pl.BlockSpec((hw, s), lambda bi, h, i: (h, bi)),
            pl.BlockSpec((1, 2, t, t), lambda bi, h, i: (h, 0, 0, 0)),
            pl.BlockSpec((4, HEAD_DIM), lambda bi, h, i: (0, 0)),
            pl.BlockSpec((hw, 1), lambda bi, h, i: (0, 0)),
        ],
        out_specs=pl.BlockSpec((1, t, hw), lambda bi, h, i: (bi, i, h)),
        scratch_shapes=[stat(), stat(), acc(), stat(), stat(), acc(), pltpu.VMEM((2, t, t), F32)],
        compiler_params=_params("parallel", "parallel", "arbitrary"),
        name="diff_attention",
    )(qk, qk, vt, bias_tiles, lamv, subln_g_col)


def _row_sq_norm_max(x):
    xf = x.astype(F32)
    return jnp.max(jnp.sum(xf * xf, axis=1, keepdims=True))


def _fox_attn_kernel(cend_ref, q_ref, k_ref, vt_ref, kc_ref, cq_ref, o_ref, m_ref, a_ref, z_ref, kn_ref,
                     *, t, nheads):
    bi = pl.program_id(0)
    hp = pl.program_id(1)
    i = pl.program_id(2)
    nk = pl.num_programs(2)
    heads = range(FOX_HEADS_PER_STEP)
    col = lambda e: slice(e * HEAD_DIM, (e + 1) * HEAD_DIM)
    for e in heads:
        _init_stats(m_ref.at[e], None, a_ref.at[e])

    @pl.when(i == 0)
    def _():
        def norm_body(j, running):
            start = pl.multiple_of(j * t, t)
            k = k_ref[0, pl.ds(start, t), :]
            running = tuple(jnp.maximum(r, _row_sq_norm_max(k[:, col(e)])) for e, r in zip(heads, running))
            for e in heads:
                kn_ref[e * nk + j] = running[e]
            return running

        lax.fori_loop(0, nk, norm_body, tuple(jnp.float32(0.0) for _ in heads))

    cq = cq_ref[0]
    lane = lax.broadcasted_iota(jnp.int32, cq.shape, 1)
    npc = C_PIECES * nheads
    k_diag = k_ref[0, pl.ds(pl.multiple_of(i * t, t), t), :]
    qs, j_starts = [], []
    for e in heads:
        h = hp * FOX_HEADS_PER_STEP + e
        q_e = q_ref[0][:, col(e)]
        ct = jnp.sum(jnp.where(lane == h, cq, 0.0), axis=1, keepdims=True) * LOG2E
        qc = jnp.where(lane < npc, jnp.where((lane & (nheads - 1)) == h, 1.0, 0.0), 0.0)
        for p, piece in enumerate(_split_bf16(ct, C_PIECES)):
            qc = jnp.where(lane == npc + p, piece, qc)
        qs.append(jnp.concatenate([q_e, qc.astype(BF16)], axis=1))

        q2 = _row_sq_norm_max(q_e)
        m_lo = jnp.min(jnp.sum(q_e.astype(F32) * k_diag[:, col(e)].astype(F32), axis=1, keepdims=True))
        c_hi = jnp.max(ct)

        def count_body(j, n_skip, e=e, h=h, q2=q2, m_lo=m_lo, c_hi=c_hi):
            r = PRUNE_LOG2 + m_lo - (c_hi - cend_ref[(bi * nk + j) * nheads + h] * LOG2E)
            skippable = jnp.logical_and(r > 0.0, q2 * kn_ref[e * nk + j] < r * r)
            return jnp.where(jnp.logical_and(skippable, n_skip == j), n_skip + 1, n_skip)

        j_starts.append(lax.fori_loop(0, i, count_body, jnp.int32(0)))
    j_start = functools.reduce(jnp.minimum, j_starts)

    def scores(j):
        start = pl.multiple_of(j * t, t)
        k = k_ref[0, pl.ds(start, t), :]
        kc = kc_ref[0, pl.ds(start, t), :]
        for e in heads:
            z_ref[e] = _dot_nt(jnp.concatenate([k[:, col(e)], kc], axis=1), qs[e])

    def consume(j, diagonal, j_next):
        zs = [z_ref[e] for e in heads]
        if j_next is not None:
            scores(j_next)
        start = pl.multiple_of(j * t, t)
        vt = vt_ref[:, pl.ds(start, t)]
        ones = jnp.ones((ONES_ROWS, t), BF16)
        for e, z in zip(heads, zs):
            if diagonal:
                ki = lax.broadcasted_iota(jnp.int32, (t, t), 0)
                qi = lax.broadcasted_iota(jnp.int32, (t, t), 1)
                z = jnp.where(ki <= qi, z, NEG)
            _online_softmax_step(z, jnp.concatenate([vt[col(e)], ones], axis=0), m_ref.at[e], None, a_ref.at[e])

    scores(j_start)
    _far_loop(lambda j, j_next: consume(j, False, j_next), j_start, i)
    consume(i, True, None)
    o = [a_ref[e, :HEAD_DIM] * (1.0 / a_ref[e, HEAD_DIM:HEAD_DIM + 1]) for e in heads]
    o_ref[0] = jnp.concatenate([x.T for x in o], axis=1).astype(o_ref.dtype)


def fox_attention(q, k, vt, kc, c_cols, c_end, *, nheads, t):
    b, s, d = q.shape
    g = FOX_HEADS_PER_STEP
    gw = g * HEAD_DIM
    assert d == nheads * HEAD_DIM and nheads % g == 0 and s % t == 0 and c_end.shape == (b * (s // t) * nheads,)
    assert nheads & (nheads - 1) == 0 and C_PIECES * nheads + C_PIECES <= LANES
    return pl.pallas_call(
        functools.partial(_fox_attn_kernel, t=t, nheads=nheads),
        out_shape=jax.ShapeDtypeStruct((b, s, d), BF16),
        grid=(b, nheads // g, s // t),
        in_specs=[
            pl.BlockSpec(memory_space=pltpu.SMEM),
            pl.BlockSpec((1, t, gw), lambda bi, hp, i: (bi, i, hp)),
            pl.BlockSpec((1, s, gw), lambda bi, hp, i: (bi, 0, hp)),
            pl.BlockSpec((gw, s), lambda bi, hp, i: (hp, bi)),
            pl.BlockSpec((1, s, LANES), lambda bi, hp, i: (bi, 0, 0)),
            pl.BlockSpec((1, t, LANES), lambda bi, hp, i: (bi, i, 0)),
        ],
        out_specs=pl.BlockSpec((1, t, gw), lambda bi, hp, i: (bi, i, hp)),
        scratch_shapes=[pltpu.VMEM((g, 1, t), F32), pltpu.VMEM((g, HEAD_DIM + ONES_ROWS, t), F32),
                        pltpu.VMEM((g, t, t), F32), pltpu.SMEM((g * (s // t),), F32)],
        compiler_params=_params("parallel", "parallel", "arbitrary"),
        name="fox_attention",
    )(c_end, q, k, vt, kc, c_cols)


def _forget_scan_kernel(x_ref, g_ref, wt_ref, b_ref, kc_ref, ccol_ref, carry_ref, *, ts, nheads):
    @pl.when(pl.program_id(1) == 0)
    def _():
        carry_ref[...] = jnp.zeros_like(carry_ref)

    x = x_ref[0]
    hn = x * _rms_scale(x, NORM_EPS) * g_ref[...]
    hn_hi, hn_lo = (p.astype(BF16) for p in _split_bf16(hn, 2))
    w_hi, w_lo = (p.astype(BF16) for p in _split_bf16(wt_ref[...], 2))
    f = _dot_nt(w_hi, hn_hi) + _dot_nt(w_hi, hn_lo) + _dot_nt(w_lo, hn_hi) + b_ref[...]
    logf = -(jnp.maximum(-f, 0.0) + jnp.log1p(jnp.exp(-jnp.abs(f))))
    row = lax.broadcasted_iota(jnp.int32, (ts, ts), 0)
    col = lax.broadcasted_iota(jnp.int32, (ts, ts), 1)
    upper = jnp.where(row <= col, 1.0, 0.0).astype(BF16)
    c = carry_ref[:, 0:1]
    for piece in _split_bf16(logf, C_PIECES):
        c = c + _dot(piece.astype(BF16), upper)
    carry_ref[...] = jnp.broadcast_to(c[:, ts - 1:ts], carry_ref.shape)
    ccol_ref[0] = c.T

    hrow = lax.broadcasted_iota(jnp.int32, c.shape, 0)
    npc = C_PIECES * nheads
    aug = jnp.where(hrow < npc + C_PIECES, 1.0, 0.0)
    for p, piece in reversed(list(enumerate(_split_bf16(c * (-LOG2E), C_PIECES)))):
        aug = jnp.where(hrow < (p + 1) * nheads, piece, aug)
    kc_ref[0] = aug.T.astype(BF16)


def forget_scan(h, g, w_f_t, b_f, *, nheads, ts):
    b, s, d = h.shape
    kernel = functools.partial(_forget_scan_kernel, ts=ts, nheads=nheads)
    return pl.pallas_call(
        kernel,
        out_shape=(jax.ShapeDtypeStruct((b, s, LANES), BF16), jax.ShapeDtypeStruct((b, s, LANES), F32)),
        grid=(b, s // ts),
        in_specs=[
            pl.BlockSpec((1, ts, d), lambda bi, i: (bi, i, 0)),
            pl.BlockSpec((1, d), lambda bi, i: (0, 0)),
            pl.BlockSpec((LANES, d), lambda bi, i: (0, 0)),
            pl.BlockSpec((LANES, 1), lambda bi, i: (0, 0)),
        ],
        out_specs=(pl.BlockSpec((1, ts, LANES), lambda bi, i: (bi, i, 0)),
                   pl.BlockSpec((1, ts, LANES), lambda bi, i: (bi, i, 0))),
        scratch_shapes=[pltpu.VMEM((LANES, LANES), F32)],
        compiler_params=_params("parallel", "arbitrary"),
        name="forget_scan",
    )(h, g, w_f_t, b_f)


def _pick(n, target):
    t = min(n, target)
    while n % t:
        t //= 2
    return t


def kernel(x, rel_bias_table, attn_norm_g, mlp_norm_g, w_qkv_a, lam_q1, lam_k1, lam_q2, lam_k2, subln_g,
           w_o_a, kv_norm_g, w_k_b, w_v_b, w_f_b, b_f_b, w_q_b, w_o_b, w_mlp_in, w_mlp_out, final_norm_g):
    b, s, d = x.shape
    n = b * s
    depth = attn_norm_g.shape[0]
    assert depth == 2 and w_qkv_a.shape[0] == 1 and w_q_b.shape[0] == 1
    diff_heads = d // (2 * HEAD_DIM)
    fox_heads = d // HEAD_DIM
    dff = w_mlp_in.shape[2]

    t_attn = _pick(s, 512)
    tm = _pick(n, 1024)
    tn = _pick(d, 1024)
    tf = _pick(dff, 512)
    ts = _pick(s, 512)

    row = lambda v: v.reshape(1, -1).astype(F32)
    x2 = x.reshape(n, d)

    bias_tiles = t5_bias_tiles(rel_bias_table.astype(F32), t=t_attn)
    g0 = row(attn_norm_g[0])
    qk = norm_matmul(x2, g0, w_qkv_a[0, :, :2 * d].astype(BF16), tm=tm, tn=tn, scaled_cols=d)
    vt = norm_matmul(x2, g0, w_qkv_a[0, :, 2 * d:].T.astype(BF16), tm=tm, tn=tn, transpose_out=True)
    lamv = jnp.stack([lam_q1[0], lam_k1[0], lam_q2[0], lam_k2[0]]).astype(F32)
    lambda_init = 0.8 - 0.6 * math.exp(-0.3 * 0)
    o = diff_attention(qk.reshape(b, s, 2 * d), vt, bias_tiles, lamv, subln_g[0].reshape(-1, 1).astype(F32),
                       batch=b, nheads=diff_heads, t=t_attn, lambda_init=lambda_init)
    h = matmul_residual(o.reshape(n, d), w_o_a[0].astype(BF16), x2, tm=tm, tn=tn)
    h = mlp_block(h, row(mlp_norm_g[0]), w_mlp_in[0].astype(BF16), w_mlp_out[0].astype(BF16),
                  row(final_norm_g), tm=tm, tf=tf, final_norm=False)

    gkv = row(kv_norm_g)
    w_f_t = jnp.zeros((LANES, d), F32)
    b_f = jnp.zeros((LANES, 1), F32)
    for p in range(C_PIECES):
        w_f_t = w_f_t.at[p * fox_heads:(p + 1) * fox_heads].set(w_f_b.T.astype(F32))
        b_f = b_f.at[p * fox_heads:(p + 1) * fox_heads, 0].set(b_f_b.astype(F32))
    kc, c_cols = forget_scan(h.reshape(b, s, d), gkv, w_f_t, b_f, nheads=fox_heads, ts=ts)
    k = norm_matmul(h, gkv, w_k_b.astype(BF16), tm=tm, tn=tn)
    vt = norm_matmul(h, gkv, w_v_b.T.astype(BF16), tm=tm, tn=tn, transpose_out=True)

    q = norm_matmul(h, row(attn_norm_g[1]), w_q_b[0].astype(BF16), tm=tm, tn=tn, scaled_cols=d)
    c_end = c_cols[:, t_attn - 1::t_attn, :fox_heads].reshape(-1)
    o = fox_attention(q.reshape(b, s, d), k.reshape(b, s, d), vt, kc, c_cols, c_end, nheads=fox_heads, t=t_attn)
    h = matmul_residual(o.reshape(n, d), w_o_b[0].astype(BF16), h, tm=tm, tn=tn)
    h = mlp_block(h, row(mlp_norm_g[1]), w_mlp_in[1].astype(BF16), w_mlp_out[1].astype(BF16),
                  row(final_norm_g), tm=tm, tf=tf, final_norm=True)
    return h.reshape(b, s, d)
```

```python
import functools
import math

import jax
import jax.numpy as jnp
from jax import lax
from jax.experimental import pallas as pl
from jax.experimental.pallas import tpu as pltpu

HEAD_DIM = 128
CHUNK = 64
REL_BUCKETS = 32
REL_MAX_DIST = 128
NORM_EPS = 1e-6
SUBLN_EPS = 1e-5
NEG = -1e30
SCALE = HEAD_DIM ** -0.5
LOG2E = math.log2(math.e)
Q_SCALE = SCALE * LOG2E
M_INIT = -1e30
DIFF_FAR_UNROLL = 4
FOX_FAR_UNROLL = 2
ONES_ROWS = 16
PRUNE_LOG2 = -170.0
FOX_HEADS_PER_STEP = 2

V7X_VMEM_LIMIT_BYTES = 56 * 1024 * 1024
LANES = 128
C_PIECES = 3

F32 = jnp.float32
BF16 = jnp.bfloat16


def _params(*semantics):
    return pltpu.CompilerParams(dimension_semantics=semantics, vmem_limit_bytes=V7X_VMEM_LIMIT_BYTES)


def _rms_scale(x, eps):
    return lax.rsqrt(jnp.mean(x * x, axis=-1, keepdims=True) + eps)


def _dot(a, b):
    return jnp.dot(a, b, preferred_element_type=F32)


def _dot_nt(a, b):
    return lax.dot_general(a, b, (((1,), (1,)), ((), ())), preferred_element_type=F32)


def _split_bf16(x, terms):
    pieces = []
    for _ in range(terms - 1):
        p = x.astype(BF16).astype(x.dtype)
        pieces.append(p)
        x = x - p
    pieces.append(x.astype(BF16).astype(x.dtype))
    return pieces


def _norm_matmul_kernel(x_ref, g_ref, w_ref, o_ref, hn_ref, *, transpose_out, scaled_blocks):
    @pl.when(pl.program_id(1) == 0)
    def _():
        x = x_ref[...]
        hn_ref[...] = (x * _rms_scale(x, NORM_EPS) * g_ref[...]).astype(hn_ref.dtype)

    if transpose_out:
        y = _dot_nt(w_ref[...], hn_ref[...])
    else:
        y = _dot(hn_ref[...], w_ref[...])
    if scaled_blocks:
        y = y * jnp.where(pl.program_id(1) < scaled_blocks, Q_SCALE, 1.0)
    o_ref[...] = y.astype(o_ref.dtype)


def norm_matmul(x, g, w, *, tm, tn, transpose_out=False, scaled_cols=0):
    assert scaled_cols % tn == 0
    n, d = x.shape
    if transpose_out:
        nout = w.shape[0]
        w_spec = pl.BlockSpec((tn, d), lambda i, j: (j, 0))
        out_shape = jax.ShapeDtypeStruct((nout, n), BF16)
        out_spec = pl.BlockSpec((tn, tm), lambda i, j: (j, i))
    else:
        nout = w.shape[1]
        w_spec = pl.BlockSpec((d, tn), lambda i, j: (0, j))
        out_shape = jax.ShapeDtypeStruct((n, nout), BF16)
        out_spec = pl.BlockSpec((tm, tn), lambda i, j: (i, j))
    return pl.pallas_call(
        functools.partial(_norm_matmul_kernel, transpose_out=transpose_out, scaled_blocks=scaled_cols // tn),
        out_shape=out_shape,
        grid=(n // tm, nout // tn),
        in_specs=[
            pl.BlockSpec((tm, d), lambda i, j: (i, 0)),
            pl.BlockSpec((1, d), lambda i, j: (0, 0)),
            w_spec,
        ],
        out_specs=out_spec,
        scratch_shapes=[pltpu.VMEM((tm, d), BF16)],
        compiler_params=_params("parallel", "arbitrary"),
        name="norm_matmul_t" if transpose_out else "norm_matmul",
    )(x, g, w)


def _matmul_residual_kernel(a_ref, w_ref, r_ref, o_ref):
    o_ref[...] = r_ref[...] + _dot(a_ref[...], w_ref[...])


def matmul_residual(a, w, res, *, tm, tn):
    n, k = a.shape
    nout = w.shape[1]
    return pl.pallas_call(
        _matmul_residual_kernel,
        out_shape=jax.ShapeDtypeStruct((n, nout), F32),
        grid=(n // tm, nout // tn),
        in_specs=[
            pl.BlockSpec((tm, k), lambda i, j: (i, 0)),
            pl.BlockSpec((k, tn), lambda i, j: (0, j)),
            pl.BlockSpec((tm, tn), lambda i, j: (i, j)),
        ],
        out_specs=pl.BlockSpec((tm, tn), lambda i, j: (i, j)),
        compiler_params=_params("parallel", "arbitrary"),
        name="matmul_residual",
    )(a, w, res)


def _mlp_kernel(x_ref, g_ref, win_ref, wout_ref, gf_ref, o_ref, hn_ref, *, final_norm):
    f = pl.program_id(1)

    @pl.when(f == 0)
    def _():
        x = x_ref[...]
        hn_ref[...] = (x * _rms_scale(x, NORM_EPS) * g_ref[...]).astype(hn_ref.dtype)
        o_ref[...] = x

    u = _dot(hn_ref[...], win_ref[...])
    u = jnp.square(jnp.maximum(u, 0.0)).astype(BF16)
    o_ref[...] += _dot(u, wout_ref[...])

    if final_norm:
        @pl.when(f == pl.num_programs(1) - 1)
        def _():
            y = o_ref[...]
            o_ref[...] = y * _rms_scale(y, NORM_EPS) * gf_ref[...]


def mlp_block(x, g, w_in, w_out, g_final, *, tm, tf, final_norm):
    n, d = x.shape
    dff = w_in.shape[1]
    return pl.pallas_call(
        functools.partial(_mlp_kernel, final_norm=final_norm),
        out_shape=jax.ShapeDtypeStruct((n, d), F32),
        grid=(n // tm, dff // tf),
        in_specs=[
            pl.BlockSpec((tm, d), lambda i, f: (i, 0)),
            pl.BlockSpec((1, d), lambda i, f: (0, 0)),
            pl.BlockSpec((d, tf), lambda i, f: (0, f)),
            pl.BlockSpec((tf, d), lambda i, f: (f, 0)),
            pl.BlockSpec((1, d), lambda i, f: (0, 0)),
        ],
        out_specs=pl.BlockSpec((tm, d), lambda i, f: (i, 0)),
        scratch_shapes=[pltpu.VMEM((tm, d), BF16)],
        compiler_params=_params("parallel", "arbitrary"),
        name="mlp_block",
    )(x, g, w_in, w_out, g_final)


def _t5_bias_kernel(table_ref, o_ref, *, t):
    h = pl.program_id(0)
    half = REL_BUCKETS // 2
    max_exact = half // 2
    ki = lax.broadcasted_iota(jnp.int32, (t, t), 0)
    qi = lax.broadcasted_iota(jnp.int32, (t, t), 1)
    far = table_ref[half - 1, h]
    for d in range(2):
        rel = ki - d * t - qi
        ret = jnp.where(rel > 0, half, 0)
        n = jnp.abs(rel)
        nf = jnp.maximum(n, 1).astype(F32)
        large = max_exact + (jnp.log(nf / max_exact) / math.log(REL_MAX_DIST / max_exact)
                             * (half - max_exact)).astype(jnp.int32)
        large = jnp.minimum(large, half - 1)
        bucket = ret + jnp.where(n < max_exact, n, large)
        val = jnp.zeros((t, t), F32)
        for b in range(REL_BUCKETS):
            val = jnp.where(bucket == b, table_ref[b, h], val)
        val = (val - far) * LOG2E
        if d == 0:
            mask = (ki >> 6) <= (qi >> 6)
            val = jnp.where(mask, val, NEG)
        o_ref[0, d] = val


def t5_bias_tiles(rel_table, *, t):
    nheads = rel_table.shape[1]
    assert CHUNK == 64 and t % CHUNK == 0 and t >= REL_MAX_DIST
    return pl.pallas_call(
        functools.partial(_t5_bias_kernel, t=t),
        out_shape=jax.ShapeDtypeStruct((nheads, 2, t, t), F32),
        grid=(nheads,),
        in_specs=[pl.BlockSpec(memory_space=pltpu.SMEM)],
        out_specs=pl.BlockSpec((1, 2, t, t), lambda h: (h, 0, 0, 0)),
        compiler_params=_params("arbitrary"),
        name="t5_bias_tiles",
    )(rel_table)


def _init_stats(m_ref, l_ref, acc_ref):
    m_ref[...] = jnp.full_like(m_ref, M_INIT)
    if l_ref is not None:
        l_ref[...] = jnp.zeros_like(l_ref)
    acc_ref[...] = jnp.zeros_like(acc_ref)


def _online_softmax_step(load_z, vt, m_ref, l_ref, acc_ref):
    m_prev = m_ref[...]
    m_new = jnp.maximum(m_prev, jnp.max(load_z(), axis=0, keepdims=True))
    p = jnp.exp2(load_z() - m_new)
    alpha = jnp.exp2(m_prev - m_new)
    if l_ref is not None:
        l_ref[...] = alpha * l_ref[...] + jnp.sum(p, axis=0, keepdims=True)
    acc_ref[...] = alpha * acc_ref[...] + _dot(vt, p.astype(BF16))
    m_ref[...] = m_new


def _far_loop(consume, j_start, j_end, unroll):
    assert unroll % 2 == 0

    def body(u):
        def run(jj, carry):
            for r in range(u):
                slot = r % 2 if u > 1 else 0
                next_slot = (r + 1) % 2 if u > 1 else 0
                consume(j_start + jj * u + r, j_start + jj * u + r + 1, slot, next_slot)
            return carry
        return run

    n_far = j_end - j_start
    n_main = n_far // unroll
    lax.fori_loop(0, n_main, body(unroll), 0)
    lax.fori_loop(n_main * unroll, n_far, body(1), 0)


def _diff_attn_kernel(q_ref, k_ref, vt_ref, bias_ref, lamv_ref, g_ref, o_ref,
                      m0, l0, a0, m1, l1, a1, z_ref, *, t, lambda_init):
    i = pl.program_id(2)
    stats = ((m0, l0, a0), (m1, l1, a1))
    for s in stats:
        _init_stats(*s)
    q = q_ref[0]

    def scores(j, slot):
        start = pl.multiple_of(j * t, t)
        k = k_ref[0, pl.ds(start, t), :]
        for c in range(2):
            half = slice(c * HEAD_DIM, (c + 1) * HEAD_DIM)
            z_ref[slot, c] = _dot_nt(k[:, half], q[:, half])

    def consume(j, bias_idx, j_next, slot, next_slot):
        if j_next is not None and next_slot == slot:
            held = [z_ref[slot, c] for c in range(2)]
            load = lambda c: held[c]
        else:
            load = lambda c: z_ref[slot, c]
        if j_next is not None:
            scores(j_next, next_slot)
        start = pl.multiple_of(j * t, t)
        vt = vt_ref[:, pl.ds(start, t)]
        for c, s in enumerate(stats):
            if bias_idx is None:
                load_z = functools.partial(load, c)
            else:
                load_z = lambda c=c: load(c) + bias_ref[0, bias_idx]
            _online_softmax_step(load_z, vt, *s)

    scores(0, 0)
    _far_loop(lambda j, j_next, slot, next_slot: consume(j, None, j_next, slot, next_slot),
              0, jnp.maximum(i - 1, 0), DIFF_FAR_UNROLL)

    @pl.when(i >= 1)
    def _():
        consume(i - 1, 1, i, 0, 1)
        consume(i, 0, None, 1, None)

    @pl.when(i == 0)
    def _():
        consume(i, 0, None, 0, None)

    lamv = lamv_ref[...]
    lam = (jnp.exp(jnp.sum(lamv[0:1] * lamv[1:2], axis=1, keepdims=True))
           - jnp.exp(jnp.sum(lamv[2:3] * lamv[3:4], axis=1, keepdims=True)) + lambda_init)
    o = a0[...] * (1.0 / l0[...]) - lam * (a1[...] * (1.0 / l1[...]))
    ms = jnp.mean(o * o, axis=0, keepdims=True)
    o = o * lax.rsqrt(ms + SUBLN_EPS) * g_ref[...] * (1.0 - lambda_init)
    o_ref[0] = o.T.astype(o_ref.dtype)


def diff_attention(qk, vt, bias_tiles, lamv, subln_g_col, *, batch, nheads, t, lambda_init):
    b, s, d2 = qk.shape
    d = d2 // 2
    hw = 2 * HEAD_DIM
    assert d == nheads * hw and s % t == 0 and b == batch
    kernel = functools.partial(_diff_attn_kernel, t=t, lambda_init=lambda_init)
    stat = lambda: pltpu.VMEM((1, t), F32)
    acc = lambda: pltpu.VMEM((hw, t), F32)
    return pl.pallas_call(
        kernel,
        out_shape=jax.ShapeDtypeStruct((b, s, d), BF16),
        grid=(b, nheads, s // t),
        in_specs=[
            pl.BlockSpec((1, t, hw), lambda bi, h, i: (bi, i, h)),
            pl.BlockSpec((1, s, hw), lambda bi, h, i: (bi, 0, nheads + h)),
            pl.BlockSpec((hw, s), lambda bi, h, i: (h, bi)),
            pl.BlockSpec((1, 2, t, t), lambda bi, h, i: (h, 0, 0, 0)),
            pl.BlockSpec((4, HEAD_DIM), lambda bi, h, i: (0, 0)),
            pl.BlockSpec((hw, 1), lambda bi, h, i: (0, 0)),
        ],
        out_specs=pl.BlockSpec((1, t, hw), lambda bi, h, i: (bi, i, h)),
        scratch_shapes=[stat(), stat(), acc(), stat(), stat(), acc(), pltpu.VMEM((2, 2, t, t), F32)],
        compiler_params=_params("parallel", "parallel", "arbitrary"),
        name="diff_attention",
    )(qk, qk, vt, bias_tiles, lamv, subln_g_col)


def _row_sq_norm_max(x):
    xf = x.astype(F32)
    return jnp.max(jnp.sum(xf * xf, axis=1, keepdims=True))


def _fox_attn_kernel(cend_ref, q_ref, k_ref, vt_ref, kc_ref, cq_ref, o_ref, m_ref, a_ref, z_ref, kn_ref,
                     *, t, nheads):
    bi = pl.program_id(0)
    hp = pl.program_id(1)
    i = pl.program_id(2)
    nk = pl.num_programs(2)
    heads = range(FOX_HEADS_PER_STEP)
    col = lambda e: slice(e * HEAD_DIM, (e + 1) * HEAD_DIM)
    for e in heads:
        _init_stats(m_ref.at[e], None, a_ref.at[e])

    @pl.when(i == 0)
    def _():
        def norm_body(j, running):
            start = pl.multiple_of(j * t, t)
            k = k_ref[0, pl.ds(start, t), :]
            running = tuple(jnp.maximum(r, _row_sq_norm_max(k[:, col(e)])) for e, r in zip(heads, running))
            for e in heads:
                kn_ref[e * nk + j] = running[e]
            return running

        lax.fori_loop(0, nk, norm_body, tuple(jnp.float32(0.0) for _ in heads))

    cq = cq_ref[0]
    lane = lax.broadcasted_iota(jnp.int32, cq.shape, 1)
    npc = C_PIECES * nheads
    k_diag = k_ref[0, pl.ds(pl.multiple_of(i * t, t), t), :]
    qs, j_starts = [], []
    for e in heads:
        h = hp * FOX_HEADS_PER_STEP + e
        q_e = q_ref[0][:, col(e)]
        ct = jnp.sum(jnp.where(lane == h, cq, 0.0), axis=1, keepdims=True) * LOG2E
        qc = jnp.where(lane < npc, jnp.where((lane & (nheads - 1)) == h, 1.0, 0.0), 0.0)
        for p, piece in enumerate(_split_bf16(ct, C_PIECES)):
            qc = jnp.where(lane == npc + p, piece, qc)
        qs.append(jnp.concatenate([q_e, qc.astype(BF16)], axis=1))

        q2 = _row_sq_norm_max(q_e)
        m_lo = jnp.min(jnp.sum(q_e.astype(F32) * k_diag[:, col(e)].astype(F32), axis=1, keepdims=True))
        c_hi = jnp.max(ct)

        def needed(j, e=e, h=h, q2=q2, m_lo=m_lo, c_hi=c_hi):
            r = PRUNE_LOG2 + m_lo - (c_hi - cend_ref[(bi * nk + j) * nheads + h] * LOG2E)
            return jnp.logical_or(r <= 0.0, q2 * kn_ref[e * nk + j] >= r * r)

        last_needed = lax.while_loop(lambda j: jnp.logical_and(j >= 0, needed(jnp.maximum(j, 0))),
                                     lambda j: j - 1, i - 1)
        j_starts.append(last_needed + 1)
    j_start = functools.reduce(jnp.minimum, j_starts)

    def scores(j, slot):
        start = pl.multiple_of(j * t, t)
        k = k_ref[0, pl.ds(start, t), :]
        kc = kc_ref[0, pl.ds(start, t), :]
        for e in heads:
            z_ref[slot, e] = _dot_nt(jnp.concatenate([k[:, col(e)], kc], axis=1), qs[e])

    def consume(j, diagonal, j_next, slot, next_slot):
        if j_next is not None and next_slot == slot:
            held = [z_ref[slot, e] for e in heads]
            load = lambda e: held[e]
        else:
            load = lambda e: z_ref[slot, e]
        if j_next is not None:
            scores(j_next, next_slot)
        start = pl.multiple_of(j * t, t)
        vt = vt_ref[:, pl.ds(start, t)]
        ones = jnp.ones((ONES_ROWS, t), BF16)
        for e in heads:
            if diagonal:
                def load_z(e=e):
                    ki = lax.broadcasted_iota(jnp.int32, (t, t), 0)
                    qi = lax.broadcasted_iota(jnp.int32, (t, t), 1)
                    return jnp.where(ki <= qi, load(e), NEG)
            else:
                load_z = functools.partial(load, e)
            _online_softmax_step(load_z, jnp.concatenate([vt[col(e)], ones], axis=0),
                                 m_ref.at[e], None, a_ref.at[e])

    scores(j_start, 0)
    _far_loop(lambda j, j_next, slot, next_slot: consume(j, False, j_next, slot, next_slot), j_start, i, FOX_FAR_UNROLL)
    consume(i, True, None, 0, None)
    o = [a_ref[e, :HEAD_DIM] * (1.0 / a_ref[e, HEAD_DIM:HEAD_DIM + 1]) for e in heads]
    o_ref[0] = jnp.concatenate([x.T for x in o], axis=1).astype(o_ref.dtype)


def fox_attention(q, k, vt, kc, c_cols, c_end, *, nheads, t):
    b, s, d = q.shape
    g = FOX_HEADS_PER_STEP
    gw = g * HEAD_DIM
    assert d == nheads * HEAD_DIM and nheads % g == 0 and s % t == 0 and c_end.shape == (b * (s // t) * nheads,)
    assert nheads & (nheads - 1) == 0 and C_PIECES * nheads + C_PIECES <= LANES
    return pl.pallas_call(
        functools.partial(_fox_attn_kernel, t=t, nheads=nheads),
        out_shape=jax.ShapeDtypeStruct((b, s, d), BF16),
        grid=(b, nheads // g, s // t),
        in_specs=[
            pl.BlockSpec(memory_space=pltpu.SMEM),
            pl.BlockSpec((1, t, gw), lambda bi, hp, i: (bi, i, hp)),
            pl.BlockSpec((1, s, gw), lambda bi, hp, i: (bi, 0, hp)),
            pl.BlockSpec((gw, s), lambda bi, hp, i: (hp, bi)),
            pl.BlockSpec((1, s, LANES), lambda bi, hp, i: (bi, 0, 0)),
            pl.BlockSpec((1, t, LANES), lambda bi, hp, i: (bi, i, 0)),
        ],
        out_specs=pl.BlockSpec((1, t, gw), lambda bi, hp, i: (bi, i, hp)),
        scratch_shapes=[pltpu.VMEM((g, 1, t), F32), pltpu.VMEM((g, HEAD_DIM + ONES_ROWS, t), F32),
                        pltpu.VMEM((2, g, t, t), F32), pltpu.SMEM((g * (s // t),), F32)],
        compiler_params=_params("parallel", "parallel", "arbitrary"),
        name="fox_attention",
    )(c_end, q, k, vt, kc, c_cols)


def _forget_scan_kernel(x_ref, g_ref, wt_ref, b_ref, kc_ref, ccol_ref, carry_ref, *, ts, nheads):
    @pl.when(pl.program_id(1) == 0)
    def _():
        carry_ref[...] = jnp.zeros_like(carry_ref)

    x = x_ref[0]
    hn = x * _rms_scale(x, NORM_EPS) * g_ref[...]
    hn_hi, hn_lo = (p.astype(BF16) for p in _split_bf16(hn, 2))
    w_hi, w_lo = (p.astype(BF16) for p in _split_bf16(wt_ref[...], 2))
    f = _dot_nt(w_hi, hn_hi) + _dot_nt(w_hi, hn_lo) + _dot_nt(w_lo, hn_hi) + b_ref[...]
    logf = -(jnp.maximum(-f, 0.0) + jnp.log1p(jnp.exp(-jnp.abs(f))))
    row = lax.broadcasted_iota(jnp.int32, (ts, ts), 0)
    col = lax.broadcasted_iota(jnp.int32, (ts, ts), 1)
    upper = jnp.where(row <= col, 1.0, 0.0).astype(BF16)
    c = carry_ref[:, 0:1]
    for piece in _split_bf16(logf, C_PIECES):
        c = c + _dot(piece.astype(BF16), upper)
    carry_ref[...] = jnp.broadcast_to(c[:, ts - 1:ts], carry_ref.shape)
    ccol_ref[0] = c.T

    hrow = lax.broadcasted_iota(jnp.int32, c.shape, 0)
    npc = C_PIECES * nheads
    aug = jnp.where(hrow < npc + C_PIECES, 1.0, 0.0)
    for p, piece in reversed(list(enumerate(_split_bf16(c * (-LOG2E), C_PIECES)))):
        aug = jnp.where(hrow < (p + 1) * nheads, piece, aug)
    kc_ref[0] = aug.T.astype(BF16)


def forget_scan(h, g, w_f_t, b_f, *, nheads, ts):
    b, s, d = h.shape
    kernel = functools.partial(_forget_scan_kernel, ts=ts, nheads=nheads)
    return pl.pallas_call(
        kernel,
        out_shape=(jax.ShapeDtypeStruct((b, s, LANES), BF16), jax.ShapeDtypeStruct((b, s, LANES), F32)),
        grid=(b, s // ts),
        in_specs=[
            pl.BlockSpec((1, ts, d), lambda bi, i: (bi, i, 0)),
            pl.BlockSpec((1, d), lambda bi, i: (0, 0)),
            pl.BlockSpec((LANES, d), lambda bi, i: (0, 0)),
            pl.BlockSpec((LANES, 1), lambda bi, i: (0, 0)),
        ],
        out_specs=(pl.BlockSpec((1, ts, LANES), lambda bi, i: (bi, i, 0)),
                   pl.BlockSpec((1, ts, LANES), lambda bi, i: (bi, i, 0))),
        scratch_shapes=[pltpu.VMEM((LANES, LANES), F32)],
        compiler_params=_params("parallel", "arbitrary"),
        name="forget_scan",
    )(h, g, w_f_t, b_f)


def _pick(n, target):
    t = min(n, target)
    while n % t:
        t //= 2
    return t


def kernel(x, rel_bias_table, attn_norm_g, mlp_norm_g, w_qkv_a, lam_q1, lam_k1, lam_q2, lam_k2, subln_g,
           w_o_a, kv_norm_g, w_k_b, w_v_b, w_f_b, b_f_b, w_q_b, w_o_b, w_mlp_in, w_mlp_out, final_norm_g):
    b, s, d = x.shape
    n = b * s
    depth = attn_norm_g.shape[0]
    assert depth == 2 and w_qkv_a.shape[0] == 1 and w_q_b.shape[0] == 1
    diff_heads = d // (2 * HEAD_DIM)
    fox_heads = d // HEAD_DIM
    dff = w_mlp_in.shape[2]

    t_attn = _pick(s, 512)
    tm = _pick(n, 1024)
    tn = _pick(d, 1024)
    tf = _pick(dff, 512)
    ts = _pick(s, 512)

    row = lambda v: v.reshape(1, -1).astype(F32)
    x2 = x.reshape(n, d)

    bias_tiles = t5_bias_tiles(rel_bias_table.astype(F32), t=t_attn)
    g0 = row(attn_norm_g[0])
    qk = norm_matmul(x2, g0, w_qkv_a[0, :, :2 * d].astype(BF16), tm=tm, tn=tn, scaled_cols=d)
    vt = norm_matmul(x2, g0, w_qkv_a[0, :, 2 * d:].T.astype(BF16), tm=tm, tn=tn, transpose_out=True)
    lamv = jnp.stack([lam_q1[0], lam_k1[0], lam_q2[0], lam_k2[0]]).astype(F32)
    lambda_init = 0.8 - 0.6 * math.exp(-0.3 * 0)
    o = diff_attention(qk.reshape(b, s, 2 * d), vt, bias_tiles, lamv, subln_g[0].reshape(-1, 1).astype(F32),
                       batch=b, nheads=diff_heads, t=t_attn, lambda_init=lambda_init)
    h = matmul_residual(o.reshape(n, d), w_o_a[0].astype(BF16), x2, tm=tm, tn=tn)
    h = mlp_block(h, row(mlp_norm_g[0]), w_mlp_in[0].astype(BF16), w_mlp_out[0].astype(BF16),
                  row(final_norm_g), tm=tm, tf=tf, final_norm=False)

    gkv = row(kv_norm_g)
    w_f_t = jnp.zeros((LANES, d), F32)
    b_f = jnp.zeros((LANES, 1), F32)
    for p in range(C_PIECES):
        w_f_t = w_f_t.at[p * fox_heads:(p + 1) * fox_heads].set(w_f_b.T.astype(F32))
        b_f = b_f.at[p * fox_heads:(p + 1) * fox_heads, 0].set(b_f_b.astype(F32))
    kc, c_cols = forget_scan(h.reshape(b, s, d), gkv, w_f_t, b_f, nheads=fox_heads, ts=ts)
    k = norm_matmul(h, gkv, w_k_b.astype(BF16), tm=tm, tn=tn)
    vt = norm_matmul(h, gkv, w_v_b.T.astype(BF16), tm=tm, tn=tn, transpose_out=True)

    q = norm_matmul(h, row(attn_norm_g[1]), w_q_b[0].astype(BF16), tm=tm, tn=tn, scaled_cols=d)
    c_end = c_cols[:, t_attn - 1::t_attn, :fox_heads].reshape(-1)
    o = fox_attention(q.reshape(b, s, d), k.reshape(b, s, d), vt, kc, c_cols, c_end, nheads=fox_heads, t=t_attn)
    h = matmul_residual(o.reshape(n, d), w_o_b[0].astype(BF16), h, tm=tm, tn=tn)
    h = mlp_block(h, row(mlp_norm_g[1]), w_mlp_in[1].astype(BF16), w_mlp_out[1].astype(BF16),
                  row(final_norm_g), tm=tm, tf=tf, final_norm=True)
    return h.reshape(b, s, d)
```

```python
import functools
import math

import jax
import jax.numpy as jnp
from jax import lax
from jax.experimental import pallas as pl
from jax.experimental.pallas import tpu as pltpu

HEAD_DIM = 128
CHUNK = 64
REL_BUCKETS = 32
REL_MAX_DIST = 128
NORM_EPS = 1e-6
SUBLN_EPS = 1e-5
NEG = -1e30
SCALE = HEAD_DIM ** -0.5
LOG2E = math.log2(math.e)
Q_SCALE = SCALE * LOG2E
M_INIT = -1e30
DIFF_FAR_UNROLL = 4
FOX_FAR_UNROLL = 2
ONES_ROWS = 16
PRUNE_LOG2 = -170.0
FOX_HEADS_PER_STEP = 2

V7X_VMEM_LIMIT_BYTES = 56 * 1024 * 1024
LANES = 128
C_PIECES = 3

F32 = jnp.float32
BF16 = jnp.bfloat16


def _params(*semantics):
    return pltpu.CompilerParams(dimension_semantics=semantics, vmem_limit_bytes=V7X_VMEM_LIMIT_BYTES)


def _rms_scale(x, eps):
    return lax.rsqrt(jnp.mean(x * x, axis=-1, keepdims=True) + eps)


def _dot(a, b):
    return jnp.dot(a, b, preferred_element_type=F32)


def _dot_nt(a, b):
    return lax.dot_general(a, b, (((1,), (1,)), ((), ())), preferred_element_type=F32)


def _split_bf16(x, terms):
    pieces = []
    for _ in range(terms - 1):
        p = x.astype(BF16).astype(x.dtype)
        pieces.append(p)
        x = x - p
    pieces.append(x.astype(BF16).astype(x.dtype))
    return pieces


def _norm_matmul_kernel(x_ref, g_ref, w_ref, o_ref, *, transpose_out, out_scale):
    x = x_ref[...]
    hn = (x * _rms_scale(x, NORM_EPS) * g_ref[...]).astype(BF16)
    if transpose_out:
        y = _dot_nt(w_ref[...], hn)
    else:
        y = _dot(hn, w_ref[...])
    if out_scale != 1.0:
        y = y * out_scale
    o_ref[...] = y.astype(o_ref.dtype)


def norm_matmul(x, g, w, *, tm, transpose_out=False, out_scale=1.0):
    n, d = x.shape
    if transpose_out:
        nout = w.shape[0]
        out_shape = jax.ShapeDtypeStruct((nout, n), BF16)
        out_spec = pl.BlockSpec((nout, tm), lambda i: (0, i))
    else:
        nout = w.shape[1]
        out_shape = jax.ShapeDtypeStruct((n, nout), BF16)
        out_spec = pl.BlockSpec((tm, nout), lambda i: (i, 0))
    return pl.pallas_call(
        functools.partial(_norm_matmul_kernel, transpose_out=transpose_out, out_scale=out_scale),
        out_shape=out_shape,
        grid=(n // tm,),
        in_specs=[
            pl.BlockSpec((tm, d), lambda i: (i, 0)),
            pl.BlockSpec((1, d), lambda i: (0, 0)),
            pl.BlockSpec(w.shape, lambda i: (0, 0)),
        ],
        out_specs=out_spec,
        compiler_params=_params("parallel"),
        name="norm_matmul_t" if transpose_out else "norm_matmul",
    )(x, g, w)


def _matmul_residual_kernel(a_ref, w_ref, r_ref, o_ref):
    o_ref[...] = r_ref[...] + _dot(a_ref[...], w_ref[...])


def matmul_residual(a, w, res, *, tm):
    n, k = a.shape
    nout = w.shape[1]
    return pl.pallas_call(
        _matmul_residual_kernel,
        out_shape=jax.ShapeDtypeStruct((n, nout), F32),
        grid=(n // tm,),
        in_specs=[
            pl.BlockSpec((tm, k), lambda i: (i, 0)),
            pl.BlockSpec((k, nout), lambda i: (0, 0)),
            pl.BlockSpec((tm, nout), lambda i: (i, 0)),
        ],
        out_specs=pl.BlockSpec((tm, nout), lambda i: (i, 0)),
        compiler_params=_params("parallel"),
        name="matmul_residual",
    )(a, w, res)


def _mlp_kernel(x_ref, g_ref, win_ref, wout_ref, gf_ref, o_ref, hn_ref, *, final_norm):
    f = pl.program_id(1)

    @pl.when(f == 0)
    def _():
        x = x_ref[...]
        hn_ref[...] = (x * _rms_scale(x, NORM_EPS) * g_ref[...]).astype(hn_ref.dtype)
        o_ref[...] = x

    u = _dot(hn_ref[...], win_ref[...])
    u = jnp.square(jnp.maximum(u, 0.0)).astype(BF16)
    o_ref[...] += _dot(u, wout_ref[...])

    if final_norm:
        @pl.when(f == pl.num_programs(1) - 1)
        def _():
            y = o_ref[...]
            o_ref[...] = y * _rms_scale(y, NORM_EPS) * gf_ref[...]


def mlp_block(x, g, w_in, w_out, g_final, *, tm, tf, final_norm):
    n, d = x.shape
    dff = w_in.shape[1]
    return pl.pallas_call(
        functools.partial(_mlp_kernel, final_norm=final_norm),
        out_shape=jax.ShapeDtypeStruct((n, d), F32),
        grid=(n // tm, dff // tf),
        in_specs=[
            pl.BlockSpec((tm, d), lambda i, f: (i, 0)),
            pl.BlockSpec((1, d), lambda i, f: (0, 0)),
            pl.BlockSpec((d, tf), lambda i, f: (0, f)),
            pl.BlockSpec((tf, d), lambda i, f: (f, 0)),
            pl.BlockSpec((1, d), lambda i, f: (0, 0)),
        ],
        out_specs=pl.BlockSpec((tm, d), lambda i, f: (i, 0)),
        scratch_shapes=[pltpu.VMEM((tm, d), BF16)],
        compiler_params=_params("parallel", "arbitrary"),
        name="mlp_block",
    )(x, g, w_in, w_out, g_final)


def _t5_bias_kernel(table_ref, o_ref, *, t):
    h = pl.program_id(0)
    half = REL_BUCKETS // 2
    max_exact = half // 2
    ki = lax.broadcasted_iota(jnp.int32, (t, t), 0)
    qi = lax.broadcasted_iota(jnp.int32, (t, t), 1)
    far = table_ref[half - 1, h]
    for d in range(2):
        rel = ki - d * t - qi
        ret = jnp.where(rel > 0, half, 0)
        n = jnp.abs(rel)
        nf = jnp.maximum(n, 1).astype(F32)
        large = max_exact + (jnp.log(nf / max_exact) / math.log(REL_MAX_DIST / max_exact)
                             * (half - max_exact)).astype(jnp.int32)
        large = jnp.minimum(large, half - 1)
        bucket = ret + jnp.where(n < max_exact, n, large)
        val = jnp.zeros((t, t), F32)
        for b in range(REL_BUCKETS):
            val = jnp.where(bucket == b, table_ref[b, h], val)
        val = (val - far) * LOG2E
        if d == 0:
            mask = (ki >> 6) <= (qi >> 6)
            val = jnp.where(mask, val, NEG)
        o_ref[0, d] = val


def t5_bias_tiles(rel_table, *, t):
    nheads = rel_table.shape[1]
    assert CHUNK == 64 and t % CHUNK == 0 and t >= REL_MAX_DIST
    return pl.pallas_call(
        functools.partial(_t5_bias_kernel, t=t),
        out_shape=jax.ShapeDtypeStruct((nheads, 2, t, t), F32),
        grid=(nheads,),
        in_specs=[pl.BlockSpec(memory_space=pltpu.SMEM)],
        out_specs=pl.BlockSpec((1, 2, t, t), lambda h: (h, 0, 0, 0)),
        compiler_params=_params("arbitrary"),
        name="t5_bias_tiles",
    )(rel_table)


def _init_stats(m_ref, l_ref, acc_ref):
    m_ref[...] = jnp.full_like(m_ref, M_INIT)
    if l_ref is not None:
        l_ref[...] = jnp.zeros_like(l_ref)
    acc_ref[...] = jnp.zeros_like(acc_ref)


def _online_softmax_step(load_z, vt, m_ref, l_ref, acc_ref):
    m_prev = m_ref[...]
    m_new = jnp.maximum(m_prev, jnp.max(load_z(), axis=0, keepdims=True))
    p = jnp.exp2(load_z() - m_new)
    alpha = jnp.exp2(m_prev - m_new)
    if l_ref is not None:
        l_ref[...] = alpha * l_ref[...] + jnp.sum(p, axis=0, keepdims=True)
    acc_ref[...] = alpha * acc_ref[...] + _dot(vt, p.astype(BF16))
    m_ref[...] = m_new


def _far_loop(consume, j_start, j_end, unroll):
    assert unroll % 2 == 0

    def body(u):
        def run(jj, carry):
            for r in range(u):
                slot = r % 2 if u > 1 else 0
                next_slot = (r + 1) % 2 if u > 1 else 0
                consume(j_start + jj * u + r, j_start + jj * u + r + 1, slot, next_slot)
            return carry
        return run

    n_far = j_end - j_start
    n_main = n_far // unroll
    lax.fori_loop(0, n_main, body(unroll), 0)
    lax.fori_loop(n_main * unroll, n_far, body(1), 0)


def _diff_attn_kernel(q_ref, k_ref, vt_ref, bias_ref, lamv_ref, g_ref, o_ref,
                      m0, l0, a0, m1, l1, a1, z_ref, *, t, lambda_init):
    i = pl.program_id(2)
    stats = ((m0, l0, a0), (m1, l1, a1))
    for s in stats:
        _init_stats(*s)
    q = q_ref[0]

    def scores(j, slot):
        start = pl.multiple_of(j * t, t)
        k = k_ref[0, pl.ds(start, t), :]
        for c in range(2):
            half = slice(c * HEAD_DIM, (c + 1) * HEAD_DIM)
            z_ref[slot, c] = _dot_nt(k[:, half], q[:, half])

    def consume(j, bias_idx, j_next, slot, next_slot):
        if j_next is not None and next_slot == slot:
            held = [z_ref[slot, c] for c in range(2)]
            load = lambda c: held[c]
        else:
            load = lambda c: z_ref[slot, c]
        if j_next is not None:
            scores(j_next, next_slot)
        start = pl.multiple_of(j * t, t)
        vt = vt_ref[:, pl.ds(start, t)]
        for c, s in enumerate(stats):
            if bias_idx is None:
                load_z = functools.partial(load, c)
            else:
                load_z = lambda c=c: load(c) + bias_ref[0, bias_idx]
            _online_softmax_step(load_z, vt, *s)

    scores(0, 0)
    _far_loop(lambda j, j_next, slot, next_slot: consume(j, None, j_next, slot, next_slot),
              0, jnp.maximum(i - 1, 0), DIFF_FAR_UNROLL)

    @pl.when(i >= 1)
    def _():
        consume(i - 1, 1, i, 0, 1)
        consume(i, 0, None, 1, None)

    @pl.when(i == 0)
    def _():
        consume(i, 0, None, 0, None)

    lamv = lamv_ref[...]
    lam = (jnp.exp(jnp.sum(lamv[0:1] * lamv[1:2], axis=1, keepdims=True))
           - jnp.exp(jnp.sum(lamv[2:3] * lamv[3:4], axis=1, keepdims=True)) + lambda_init)
    o = a0[...] * (1.0 / l0[...]) - lam * (a1[...] * (1.0 / l1[...]))
    ms = jnp.mean(o * o, axis=0, keepdims=True)
    o = o * lax.rsqrt(ms + SUBLN_EPS) * g_ref[...] * (1.0 - lambda_init)
    o_ref[0] = o.T.astype(o_ref.dtype)


def diff_attention(q, k, vt, bias_tiles, lamv, subln_g_col, *, nheads, t, lambda_init):
    b, s, d = q.shape
    hw = 2 * HEAD_DIM
    assert d == nheads * hw and s % t == 0
    kernel = functools.partial(_diff_attn_kernel, t=t, lambda_init=lambda_init)
    stat = lambda: pltpu.VMEM((1, t), F32)
    acc = lambda: pltpu.VMEM((hw, t), F32)
    return pl.pallas_call(
        kernel,
        out_shape=jax.ShapeDtypeStruct((b, s, d), BF16),
        grid=(b, nheads, s // t),
        in_specs=[
            pl.BlockSpec((1, t, hw), lambda bi, h, i: (bi, i, h)),
            pl.BlockSpec((1, s, hw), lambda bi, h, i: (bi, 0, h)),
            pl.BlockSpec((hw, s), lambda bi, h, i: (h, bi)),
            pl.BlockSpec((1, 2, t, t), lambda bi, h, i: (h, 0, 0, 0)),
            pl.BlockSpec((4, HEAD_DIM), lambda bi, h, i: (0, 0)),
            pl.BlockSpec((hw, 1), lambda bi, h, i: (0, 0)),
        ],
        out_specs=pl.BlockSpec((1, t, hw), lambda bi, h, i: (bi, i, h)),
        scratch_shapes=[stat(), stat(), acc(), stat(), stat(), acc(), pltpu.VMEM((2, 2, t, t), F32)],
        compiler_params=_params("parallel", "parallel", "arbitrary"),
        name="diff_attention",
    )(q, k, vt, bias_tiles, lamv, subln_g_col)


def _row_sq_norm_max(x):
    xf = x.astype(F32)
    return jnp.max(jnp.sum(xf * xf, axis=1, keepdims=True))


def _fox_attn_kernel(cend_ref, q_ref, k_ref, vt_ref, kc_ref, cq_ref, o_ref, m_ref, a_ref, z_ref, kn_ref,
                     *, t, nheads):
    bi = pl.program_id(0)
    hp = pl.program_id(1)
    i = pl.program_id(2)
    nk = pl.num_programs(2)
    heads = range(FOX_HEADS_PER_STEP)
    col = lambda e: slice(e * HEAD_DIM, (e + 1) * HEAD_DIM)
    for e in heads:
        _init_stats(m_ref.at[e], None, a_ref.at[e])

    @pl.when(i == 0)
    def _():
        def norm_body(j, running):
            start = pl.multiple_of(j * t, t)
            k = k_ref[0, pl.ds(start, t), :]
            running = tuple(jnp.maximum(r, _row_sq_norm_max(k[:, col(e)])) for e, r in zip(heads, running))
            for e in heads:
                kn_ref[e * nk + j] = running[e]
            return running

        lax.fori_loop(0, nk, norm_body, tuple(jnp.float32(0.0) for _ in heads))

    cq = cq_ref[0]
    lane = lax.broadcasted_iota(jnp.int32, cq.shape, 1)
    npc = C_PIECES * nheads
    k_diag = k_ref[0, pl.ds(pl.multiple_of(i * t, t), t), :]
    qs, j_starts = [], []
    for e in heads:
        h = hp * FOX_HEADS_PER_STEP + e
        q_e = q_ref[0][:, col(e)]
        ct = jnp.sum(jnp.where(lane == h, cq, 0.0), axis=1, keepdims=True) * LOG2E
        qc = jnp.where(lane < npc, jnp.where((lane & (nheads - 1)) == h, 1.0, 0.0), 0.0)
        for p, piece in enumerate(_split_bf16(ct, C_PIECES)):
            qc = jnp.where(lane == npc + p, piece, qc)
        qs.append(jnp.concatenate([q_e, qc.astype(BF16)], axis=1))

        q2 = _row_sq_norm_max(q_e)
        m_lo = jnp.min(jnp.sum(q_e.astype(F32) * k_diag[:, col(e)].astype(F32), axis=1, keepdims=True))
        c_hi = jnp.max(ct)

        def needed(j, e=e, h=h, q2=q2, m_lo=m_lo, c_hi=c_hi):
            r = PRUNE_LOG2 + m_lo - (c_hi - cend_ref[(bi * nk + j) * nheads + h] * LOG2E)
            return jnp.logical_or(r <= 0.0, q2 * kn_ref[e * nk + j] >= r * r)

        last_needed = lax.while_loop(lambda j: jnp.logical_and(j >= 0, needed(jnp.maximum(j, 0))),
                                     lambda j: j - 1, i - 1)
        j_starts.append(last_needed + 1)
    j_start = functools.reduce(jnp.minimum, j_starts)

    def scores(j, slot):
        start = pl.multiple_of(j * t, t)
        k = k_ref[0, pl.ds(start, t), :]
        kc = kc_ref[0, pl.ds(start, t), :]
        for e in heads:
            z_ref[slot, e] = _dot_nt(jnp.concatenate([k[:, col(e)], kc], axis=1), qs[e])

    def consume(j, diagonal, j_next, slot, next_slot):
        if j_next is not None and next_slot == slot:
            held = [z_ref[slot, e] for e in heads]
            load = lambda e: held[e]
        else:
            load = lambda e: z_ref[slot, e]
        if j_next is not None:
            scores(j_next, next_slot)
        start = pl.multiple_of(j * t, t)
        vt = vt_ref[:, pl.ds(start, t)]
        ones = jnp.ones((ONES_ROWS, t), BF16)
        for e in heads:
            if diagonal:
                def load_z(e=e):
                    ki = lax.broadcasted_iota(jnp.int32, (t, t), 0)
                    qi = lax.broadcasted_iota(jnp.int32, (t, t), 1)
                    return jnp.where(ki <= qi, load(e), NEG)
            else:
                load_z = functools.partial(load, e)
            _online_softmax_step(load_z, jnp.concatenate([vt[col(e)], ones], axis=0),
                                 m_ref.at[e], None, a_ref.at[e])

    scores(j_start, 0)
    _far_loop(lambda j, j_next, slot, next_slot: consume(j, False, j_next, slot, next_slot), j_start, i, FOX_FAR_UNROLL)
    consume(i, True, None, 0, None)
    o = [a_ref[e, :HEAD_DIM] * (1.0 / a_ref[e, HEAD_DIM:HEAD_DIM + 1]) for e in heads]
    o_ref[0] = jnp.concatenate([x.T for x in o], axis=1).astype(o_ref.dtype)


def fox_attention(q, k, vt, kc, c_cols, c_end, *, nheads, t):
    b, s, d = q.shape
    g = FOX_HEADS_PER_STEP
    gw = g * HEAD_DIM
    assert d == nheads * HEAD_DIM and nheads % g == 0 and s % t == 0 and c_end.shape == (b * (s // t) * nheads,)
    assert nheads & (nheads - 1) == 0 and C_PIECES * nheads + C_PIECES <= LANES
    return pl.pallas_call(
        functools.partial(_fox_attn_kernel, t=t, nheads=nheads),
        out_shape=jax.ShapeDtypeStruct((b, s, d), BF16),
        grid=(b, nheads // g, s // t),
        in_specs=[
            pl.BlockSpec(memory_space=pltpu.SMEM),
            pl.BlockSpec((1, t, gw), lambda bi, hp, i: (bi, i, hp)),
            pl.BlockSpec((1, s, gw), lambda bi, hp, i: (bi, 0, hp)),
            pl.BlockSpec((gw, s), lambda bi, hp, i: (hp, bi)),
            pl.BlockSpec((1, s, LANES), lambda bi, hp, i: (bi, 0, 0)),
            pl.BlockSpec((1, t, LANES), lambda bi, hp, i: (bi, i, 0)),
        ],
        out_specs=pl.BlockSpec((1, t, gw), lambda bi, hp, i: (bi, i, hp)),
        scratch_shapes=[pltpu.VMEM((g, 1, t), F32), pltpu.VMEM((g, HEAD_DIM + ONES_ROWS, t), F32),
                        pltpu.VMEM((2, g, t, t), F32), pltpu.SMEM((g * (s // t),), F32)],
        compiler_params=_params("parallel", "parallel", "arbitrary"),
        name="fox_attention",
    )(c_end, q, k, vt, kc, c_cols)


def _forget_scan_kernel(x_ref, g_ref, wt_ref, b_ref, kc_ref, ccol_ref, carry_ref, *, ts, nheads):
    @pl.when(pl.program_id(1) == 0)
    def _():
        carry_ref[...] = jnp.zeros_like(carry_ref)

    x = x_ref[0]
    hn = x * _rms_scale(x, NORM_EPS) * g_ref[...]
    hn_hi, hn_lo = (p.astype(BF16) for p in _split_bf16(hn, 2))
    w_hi, w_lo = (p.astype(BF16) for p in _split_bf16(wt_ref[...], 2))
    f = _dot_nt(w_hi, hn_hi) + _dot_nt(w_hi, hn_lo) + _dot_nt(w_lo, hn_hi) + b_ref[...]
    logf = -(jnp.maximum(-f, 0.0) + jnp.log1p(jnp.exp(-jnp.abs(f))))
    row = lax.broadcasted_iota(jnp.int32, (ts, ts), 0)
    col = lax.broadcasted_iota(jnp.int32, (ts, ts), 1)
    upper = jnp.where(row <= col, 1.0, 0.0).astype(BF16)
    c = carry_ref[:, 0:1]
    for piece in _split_bf16(logf, C_PIECES):
        c = c + _dot(piece.astype(BF16), upper)
    carry_ref[...] = jnp.broadcast_to(c[:, ts - 1:ts], carry_ref.shape)
    ccol_ref[0] = c.T

    hrow = lax.broadcasted_iota(jnp.int32, c.shape, 0)
    npc = C_PIECES * nheads
    aug = jnp.where(hrow < npc + C_PIECES, 1.0, 0.0)
    for p, piece in reversed(list(enumerate(_split_bf16(c * (-LOG2E), C_PIECES)))):
        aug = jnp.where(hrow < (p + 1) * nheads, piece, aug)
    kc_ref[0] = aug.T.astype(BF16)


def forget_scan(h, g, w_f_t, b_f, *, nheads, ts):
    b, s, d = h.shape
    kernel = functools.partial(_forget_scan_kernel, ts=ts, nheads=nheads)
    return pl.pallas_call(
        kernel,
        out_shape=(jax.ShapeDtypeStruct((b, s, LANES), BF16), jax.ShapeDtypeStruct((b, s, LANES), F32)),
        grid=(b, s // ts),
        in_specs=[
            pl.BlockSpec((1, ts, d), lambda bi, i: (bi, i, 0)),
            pl.BlockSpec((1, d), lambda bi, i: (0, 0)),
            pl.BlockSpec((LANES, d), lambda bi, i: (0, 0)),
            pl.BlockSpec((LANES, 1), lambda bi, i: (0, 0)),
        ],
        out_specs=(pl.BlockSpec((1, ts, LANES), lambda bi, i: (bi, i, 0)),
                   pl.BlockSpec((1, ts, LANES), lambda bi, i: (bi, i, 0))),
        scratch_shapes=[pltpu.VMEM((LANES, LANES), F32)],
        compiler_params=_params("parallel", "arbitrary"),
        name="forget_scan",
    )(h, g, w_f_t, b_f)


def _pick(n, target):
    t = min(n, target)
    while n % t:
        t //= 2
    return t


def kernel(x, rel_bias_table, attn_norm_g, mlp_norm_g, w_qkv_a, lam_q1, lam_k1, lam_q2, lam_k2, subln_g,
           w_o_a, kv_norm_g, w_k_b, w_v_b, w_f_b, b_f_b, w_q_b, w_o_b, w_mlp_in, w_mlp_out, final_norm_g):
    b, s, d = x.shape
    n = b * s
    depth = attn_norm_g.shape[0]
    assert depth == 2 and w_qkv_a.shape[0] == 1 and w_q_b.shape[0] == 1
    diff_heads = d // (2 * HEAD_DIM)
    fox_heads = d // HEAD_DIM
    dff = w_mlp_in.shape[2]

    t_attn = _pick(s, 512)
    tm = _pick(n, 1024)
    tf = _pick(dff, 512)
    ts = _pick(s, 512)

    row = lambda v: v.reshape(1, -1).astype(F32)
    x2 = x.reshape(n, d)

    bias_tiles = t5_bias_tiles(rel_bias_table.astype(F32), t=t_attn)
    g0 = row(attn_norm_g[0])
    w_q, w_k, w_v = (w_qkv_a[0, :, c * d:(c + 1) * d] for c in range(3))
    q = norm_matmul(x2, g0, w_q.astype(BF16), tm=tm, out_scale=Q_SCALE)
    k = norm_matmul(x2, g0, w_k.astype(BF16), tm=tm)
    vt = norm_matmul(x2, g0, w_v.T.astype(BF16), tm=tm, transpose_out=True)
    lamv = jnp.stack([lam_q1[0], lam_k1[0], lam_q2[0], lam_k2[0]]).astype(F32)
    lambda_init = 0.8 - 0.6 * math.exp(-0.3 * 0)
    o = diff_attention(q.reshape(b, s, d), k.reshape(b, s, d), vt, bias_tiles, lamv,
                       subln_g[0].reshape(-1, 1).astype(F32), nheads=diff_heads, t=t_attn, lambda_init=lambda_init)
    h = matmul_residual(o.reshape(n, d), w_o_a[0].astype(BF16), x2, tm=tm)
    h = mlp_block(h, row(mlp_norm_g[0]), w_mlp_in[0].astype(BF16), w_mlp_out[0].astype(BF16),
                  row(final_norm_g), tm=tm, tf=tf, final_norm=False)

    gkv = row(kv_norm_g)
    w_f_t = jnp.zeros((LANES, d), F32)
    b_f = jnp.zeros((LANES, 1), F32)
    for p in range(C_PIECES):
        w_f_t = w_f_t.at[p * fox_heads:(p + 1) * fox_heads].set(w_f_b.T.astype(F32))
        b_f = b_f.at[p * fox_heads:(p + 1) * fox_heads, 0].set(b_f_b.astype(F32))
    kc, c_cols = forget_scan(h.reshape(b, s, d), gkv, w_f_t, b_f, nheads=fox_heads, ts=ts)
    k = norm_matmul(h, gkv, w_k_b.astype(BF16), tm=tm)
    vt = norm_matmul(h, gkv, w_v_b.T.astype(BF16), tm=tm, transpose_out=True)

    q = norm_matmul(h, row(attn_norm_g[1]), w_q_b[0].astype(BF16), tm=tm, out_scale=Q_SCALE)
    c_end = c_cols[:, t_attn - 1::t_attn, :fox_heads].reshape(-1)
    o = fox_attention(q.reshape(b, s, d), k.reshape(b, s, d), vt, kc, c_cols, c_end, nheads=fox_heads, t=t_attn)
    h = matmul_residual(o.reshape(n, d), w_o_b[0].astype(BF16), h, tm=tm)
    h = mlp_block(h, row(mlp_norm_g[1]), w_mlp_in[1].astype(BF16), w_mlp_out[1].astype(BF16),
                  row(final_norm_g), tm=tm, tf=tf, final_norm=True)
    return h.reshape(b, s, d)
```

```python
import functools
import math

import jax
import jax.numpy as jnp
from jax import lax
from jax.experimental import pallas as pl
from jax.experimental.pallas import tpu as pltpu

HEAD_DIM = 128
CHUNK = 64
REL_BUCKETS = 32
REL_MAX_DIST = 128
NORM_EPS = 1e-6
SUBLN_EPS = 1e-5
NEG = -1e30
SCALE = HEAD_DIM ** -0.5
LOG2E = math.log2(math.e)
Q_SCALE = SCALE * LOG2E
M_INIT = -1e30
DIFF_FAR_UNROLL = 4
FOX_FAR_UNROLL = 2
ONES_ROWS = 16
PRUNE_LOG2 = -170.0
FOX_HEADS_PER_STEP = 2

V7X_VMEM_LIMIT_BYTES = 56 * 1024 * 1024
LANES = 128
C_PIECES = 3

F32 = jnp.float32
BF16 = jnp.bfloat16


def _params(*semantics):
    return pltpu.CompilerParams(dimension_semantics=semantics, vmem_limit_bytes=V7X_VMEM_LIMIT_BYTES)


def _rms_scale(x, eps):
    return lax.rsqrt(jnp.mean(x * x, axis=-1, keepdims=True) + eps)


def _dot(a, b):
    return jnp.dot(a, b, preferred_element_type=F32)


def _dot_nt(a, b):
    return lax.dot_general(a, b, (((1,), (1,)), ((), ())), preferred_element_type=F32)


def _split_bf16(x, terms):
    pieces = []
    for _ in range(terms - 1):
        p = x.astype(BF16).astype(x.dtype)
        pieces.append(p)
        x = x - p
    pieces.append(x.astype(BF16).astype(x.dtype))
    return pieces


def _norm_matmul_kernel(x_ref, g_ref, w_ref, o_ref, *, transpose_out, out_scale):
    x = x_ref[...]
    hn = (x * _rms_scale(x, NORM_EPS) * g_ref[...]).astype(BF16)
    if transpose_out:
        y = _dot_nt(w_ref[...], hn)
    else:
        y = _dot(hn, w_ref[...])
    if out_scale != 1.0:
        y = y * out_scale
    o_ref[...] = y.astype(o_ref.dtype)


def norm_matmul(x, g, w, *, tm, transpose_out=False, out_scale=1.0):
    n, d = x.shape
    if transpose_out:
        nout = w.shape[0]
        out_shape = jax.ShapeDtypeStruct((nout, n), BF16)
        out_spec = pl.BlockSpec((nout, tm), lambda i: (0, i))
    else:
        nout = w.shape[1]
        out_shape = jax.ShapeDtypeStruct((n, nout), BF16)
        out_spec = pl.BlockSpec((tm, nout), lambda i: (i, 0))
    return pl.pallas_call(
        functools.partial(_norm_matmul_kernel, transpose_out=transpose_out, out_scale=out_scale),
        out_shape=out_shape,
        grid=(n // tm,),
        in_specs=[
            pl.BlockSpec((tm, d), lambda i: (i, 0)),
            pl.BlockSpec((1, d), lambda i: (0, 0)),
            pl.BlockSpec(w.shape, lambda i: (0, 0)),
        ],
        out_specs=out_spec,
        compiler_params=_params("parallel"),
        name="norm_matmul_t" if transpose_out else "norm_matmul",
    )(x, g, w)


def _matmul_residual_kernel(a_ref, w_ref, r_ref, o_ref):
    o_ref[...] = r_ref[...] + _dot(a_ref[...], w_ref[...])


def matmul_residual(a, w, res, *, tm):
    n, k = a.shape
    nout = w.shape[1]
    return pl.pallas_call(
        _matmul_residual_kernel,
        out_shape=jax.ShapeDtypeStruct((n, nout), F32),
        grid=(n // tm,),
        in_specs=[
            pl.BlockSpec((tm, k), lambda i: (i, 0)),
            pl.BlockSpec((k, nout), lambda i: (0, 0)),
            pl.BlockSpec((tm, nout), lambda i: (i, 0)),
        ],
        out_specs=pl.BlockSpec((tm, nout), lambda i: (i, 0)),
        compiler_params=_params("parallel"),
        name="matmul_residual",
    )(a, w, res)


def _mlp_kernel(x_ref, g_ref, win_ref, wout_ref, gf_ref, o_ref, hn_ref, *, final_norm):
    f = pl.program_id(1)

    @pl.when(f == 0)
    def _():
        x = x_ref[...]
        hn_ref[...] = (x * _rms_scale(x, NORM_EPS) * g_ref[...]).astype(hn_ref.dtype)
        o_ref[...] = x

    u = _dot(hn_ref[...], win_ref[...])
    u = jnp.square(jnp.maximum(u, 0.0)).astype(BF16)
    o_ref[...] += _dot(u, wout_ref[...])

    if final_norm:
        @pl.when(f == pl.num_programs(1) - 1)
        def _():
            y = o_ref[...]
            o_ref[...] = y * _rms_scale(y, NORM_EPS) * gf_ref[...]


def mlp_block(x, g, w_in, w_out, g_final, *, tm, tf, final_norm):
    n, d = x.shape
    dff = w_in.shape[1]
    return pl.pallas_call(
        functools.partial(_mlp_kernel, final_norm=final_norm),
        out_shape=jax.ShapeDtypeStruct((n, d), F32),
        grid=(n // tm, dff // tf),
        in_specs=[
            pl.BlockSpec((tm, d), lambda i, f: (i, 0)),
            pl.BlockSpec((1, d), lambda i, f: (0, 0)),
            pl.BlockSpec((d, tf), lambda i, f: (0, f)),
            pl.BlockSpec((tf, d), lambda i, f: (f, 0)),
            pl.BlockSpec((1, d), lambda i, f: (0, 0)),
        ],
        out_specs=pl.BlockSpec((tm, d), lambda i, f: (i, 0)),
        scratch_shapes=[pltpu.VMEM((tm, d), BF16)],
        compiler_params=_params("parallel", "arbitrary"),
        name="mlp_block",
    )(x, g, w_in, w_out, g_final)


def _t5_bias_kernel(table_ref, o_ref, *, t):
    h = pl.program_id(0)
    half = REL_BUCKETS // 2
    max_exact = half // 2
    ki = lax.broadcasted_iota(jnp.int32, (t, t), 0)
    qi = lax.broadcasted_iota(jnp.int32, (t, t), 1)
    far = table_ref[half - 1, h]
    for d in range(2):
        rel = ki - d * t - qi
        ret = jnp.where(rel > 0, half, 0)
        n = jnp.abs(rel)
        nf = jnp.maximum(n, 1).astype(F32)
        large = max_exact + (jnp.log(nf / max_exact) / math.log(REL_MAX_DIST / max_exact)
                             * (half - max_exact)).astype(jnp.int32)
        large = jnp.minimum(large, half - 1)
        bucket = ret + jnp.where(n < max_exact, n, large)
        val = jnp.zeros((t, t), F32)
        for b in range(REL_BUCKETS):
            val = jnp.where(bucket == b, table_ref[b, h], val)
        val = (val - far) * LOG2E
        if d == 0:
            mask = (ki >> 6) <= (qi >> 6)
            val = jnp.where(mask, val, NEG)
        o_ref[0, d] = val


def t5_bias_tiles(rel_table, *, t):
    nheads = rel_table.shape[1]
    assert CHUNK == 64 and t % CHUNK == 0 and t >= REL_MAX_DIST
    return pl.pallas_call(
        functools.partial(_t5_bias_kernel, t=t),
        out_shape=jax.ShapeDtypeStruct((nheads, 2, t, t), F32),
        grid=(nheads,),
        in_specs=[pl.BlockSpec(memory_space=pltpu.SMEM)],
        out_specs=pl.BlockSpec((1, 2, t, t), lambda h: (h, 0, 0, 0)),
        compiler_params=_params("arbitrary"),
        name="t5_bias_tiles",
    )(rel_table)


def _init_stats(m_ref, l_ref, acc_ref):
    m_ref[...] = jnp.full_like(m_ref, M_INIT)
    if l_ref is not None:
        l_ref[...] = jnp.zeros_like(l_ref)
    acc_ref[...] = jnp.zeros_like(acc_ref)


def _online_softmax_step(load_z, vt, m_ref, l_ref, acc_ref):
    m_prev = m_ref[...]
    m_new = jnp.maximum(m_prev, jnp.max(load_z(), axis=0, keepdims=True))
    p = jnp.exp2(load_z() - m_new)
    alpha = jnp.exp2(m_prev - m_new)
    if l_ref is not None:
        l_ref[...] = alpha * l_ref[...] + jnp.sum(p, axis=0, keepdims=True)
    acc_ref[...] = alpha * acc_ref[...] + _dot(vt, p.astype(BF16))
    m_ref[...] = m_new


def _far_loop(consume, j_start, j_end, unroll, first_slot=0):
    assert unroll % 2 == 0

    def body(u):
        def run(jj, carry):
            for r in range(u):
                slot = (first_slot + r) % 2 if u > 1 else first_slot
                next_slot = (first_slot + r + 1) % 2 if u > 1 else first_slot
                consume(j_start + jj * u + r, j_start + jj * u + r + 1, slot, next_slot)
            return carry
        return run

    n_far = j_end - j_start
    n_main = n_far // unroll
    lax.fori_loop(0, n_main, body(unroll), 0)
    lax.fori_loop(n_main * unroll, n_far, body(1), 0)


def _diff_attn_kernel(q_ref, k_ref, vt_ref, bias_ref, lamv_ref, g_ref, o_ref,
                      m0, l0, a0, m1, l1, a1, z_ref, *, t, lambda_init):
    i = pl.program_id(2)
    stats = ((m0, l0, a0), (m1, l1, a1))
    for s in stats:
        _init_stats(*s)
    q = q_ref[0]

    def scores(j, slot):
        start = pl.multiple_of(j * t, t)
        k = k_ref[0, pl.ds(start, t), :]
        for c in range(2):
            half = slice(c * HEAD_DIM, (c + 1) * HEAD_DIM)
            z_ref[slot, c] = _dot_nt(k[:, half], q[:, half])

    def consume(j, bias_idx, j_next, slot, next_slot):
        if j_next is not None and next_slot == slot:
            held = [z_ref[slot, c] for c in range(2)]
            load = lambda c: held[c]
        else:
            load = lambda c: z_ref[slot, c]
        if j_next is not None:
            scores(j_next, next_slot)
        start = pl.multiple_of(j * t, t)
        vt = vt_ref[:, pl.ds(start, t)]
        for c, s in enumerate(stats):
            if bias_idx is None:
                load_z = functools.partial(load, c)
            else:
                load_z = lambda c=c: load(c) + bias_ref[0, bias_idx]
            _online_softmax_step(load_z, vt, *s)

    scores(0, 0)
    _far_loop(lambda j, j_next, slot, next_slot: consume(j, None, j_next, slot, next_slot),
              0, jnp.maximum(i - 1, 0), DIFF_FAR_UNROLL)

    @pl.when(i >= 1)
    def _():
        consume(i - 1, 1, i, 0, 1)
        consume(i, 0, None, 1, None)

    @pl.when(i == 0)
    def _():
        consume(i, 0, None, 0, None)

    lamv = lamv_ref[...]
    lam = (jnp.exp(jnp.sum(lamv[0:1] * lamv[1:2], axis=1, keepdims=True))
           - jnp.exp(jnp.sum(lamv[2:3] * lamv[3:4], axis=1, keepdims=True)) + lambda_init)
    o = a0[...] * (1.0 / l0[...]) - lam * (a1[...] * (1.0 / l1[...]))
    ms = jnp.mean(o * o, axis=0, keepdims=True)
    o = o * lax.rsqrt(ms + SUBLN_EPS) * g_ref[...] * (1.0 - lambda_init)
    o_ref[0] = o.T.astype(o_ref.dtype)


def diff_attention(q, k, vt, bias_tiles, lamv, subln_g_col, *, nheads, t, lambda_init):
    b, s, d = q.shape
    hw = 2 * HEAD_DIM
    assert d == nheads * hw and s % t == 0
    kernel = functools.partial(_diff_attn_kernel, t=t, lambda_init=lambda_init)
    stat = lambda: pltpu.VMEM((1, t), F32)
    acc = lambda: pltpu.VMEM((hw, t), F32)
    return pl.pallas_call(
        kernel,
        out_shape=jax.ShapeDtypeStruct((b, s, d), BF16),
        grid=(b, nheads, s // t),
        in_specs=[
            pl.BlockSpec((1, t, hw), lambda bi, h, i: (bi, i, h)),
            pl.BlockSpec((1, s, hw), lambda bi, h, i: (bi, 0, h)),
            pl.BlockSpec((hw, s), lambda bi, h, i: (h, bi)),
            pl.BlockSpec((1, 2, t, t), lambda bi, h, i: (h, 0, 0, 0)),
            pl.BlockSpec((4, HEAD_DIM), lambda bi, h, i: (0, 0)),
            pl.BlockSpec((hw, 1), lambda bi, h, i: (0, 0)),
        ],
        out_specs=pl.BlockSpec((1, t, hw), lambda bi, h, i: (bi, i, h)),
        scratch_shapes=[stat(), stat(), acc(), stat(), stat(), acc(), pltpu.VMEM((2, 2, t, t), F32)],
        compiler_params=_params("parallel", "parallel", "arbitrary"),
        name="diff_attention",
    )(q, k, vt, bias_tiles, lamv, subln_g_col)


def _row_sq_norm_max(x):
    xf = x.astype(F32)
    return jnp.max(jnp.sum(xf * xf, axis=1, keepdims=True), axis=0, keepdims=True)


def _fox_attn_kernel(cend_ref, q_ref, k_ref, vt_ref, kc_ref, cq_ref, o_ref, m_ref, a_ref, z_ref, kn_ref,
                     *, t, nheads):
    hp = pl.program_id(1)
    i = pl.program_id(2)
    nk = pl.num_programs(2)
    heads = range(FOX_HEADS_PER_STEP)
    col = lambda e: slice(e * HEAD_DIM, (e + 1) * HEAD_DIM)
    tile_lane = lax.broadcasted_iota(jnp.int32, (1, kn_ref.shape[-1]), 1)
    for e in heads:
        _init_stats(m_ref.at[e], None, a_ref.at[e])

    @pl.when(i == 0)
    def _():
        def norm_body(j, carry):
            start = pl.multiple_of(j * t, t)
            k = k_ref[0, pl.ds(start, t), :]
            out = []
            for e, (running, kn) in zip(heads, carry):
                running = jnp.maximum(running, _row_sq_norm_max(k[:, col(e)]))
                out.append((running, jnp.where(tile_lane == j, running, kn)))
            return tuple(out)

        init = tuple((jnp.zeros((1, 1), F32), jnp.zeros(tile_lane.shape, F32)) for _ in heads)
        for e, (_, kn) in zip(heads, lax.fori_loop(0, nk, norm_body, init)):
            kn_ref[e] = kn

    cq = cq_ref[0]
    lane = lax.broadcasted_iota(jnp.int32, cq.shape, 1)
    npc = C_PIECES * nheads
    k_diag = k_ref[0, pl.ds(pl.multiple_of(i * t, t), t), :]
    qs, firsts = [], []
    for e in heads:
        h = hp * FOX_HEADS_PER_STEP + e
        q_e = q_ref[0][:, col(e)]
        ct = jnp.sum(jnp.where(lane == h, cq, 0.0), axis=1, keepdims=True) * LOG2E
        qc = jnp.where(lane < npc, jnp.where((lane & (nheads - 1)) == h, 1.0, 0.0), 0.0)
        for p, piece in enumerate(_split_bf16(ct, C_PIECES)):
            qc = jnp.where(lane == npc + p, piece, qc)
        qs.append(jnp.concatenate([q_e, qc.astype(BF16)], axis=1))

        q2 = _row_sq_norm_max(q_e)
        m_lo = jnp.min(jnp.sum(q_e.astype(F32) * k_diag[:, col(e)].astype(F32), axis=1, keepdims=True),
                       axis=0, keepdims=True)
        c_hi = jnp.max(ct, axis=0, keepdims=True)
        r = PRUNE_LOG2 + m_lo - c_hi + cend_ref[0, e:e + 1, :] * LOG2E
        needed = jnp.logical_or(r <= 0.0, q2 * kn_ref[e] >= r * r)
        first = jnp.where(jnp.logical_and(needed, tile_lane < i), tile_lane, i).astype(F32)
        firsts.append(jnp.min(first, axis=1, keepdims=True))
    j_start = functools.reduce(jnp.minimum, firsts)[0, 0].astype(jnp.int32)

    def scores(j, slot):
        start = pl.multiple_of(j * t, t)
        k = k_ref[0, pl.ds(start, t), :]
        kc = kc_ref[0, pl.ds(start, t), :]
        for e in heads:
            z_ref[slot, e] = _dot_nt(jnp.concatenate([k[:, col(e)], kc], axis=1), qs[e])

    def consume(j, diagonal, j_next, slot, next_slot):
        if j_next is not None and next_slot == slot:
            held = [z_ref[slot, e] for e in heads]
            load = lambda e: held[e]
        else:
            load = lambda e: z_ref[slot, e]
        if j_next is not None:
            scores(j_next, next_slot)
        start = pl.multiple_of(j * t, t)
        vt = vt_ref[:, pl.ds(start, t)]
        ones = jnp.ones((ONES_ROWS, t), BF16)
        for e in heads:
            if diagonal:
                def load_z(e=e):
                    ki = lax.broadcasted_iota(jnp.int32, (t, t), 0)
                    qi = lax.broadcasted_iota(jnp.int32, (t, t), 1)
                    return jnp.where(ki <= qi, load(e), NEG)
            else:
                load_z = functools.partial(load, e)
            _online_softmax_step(load_z, jnp.concatenate([vt[col(e)], ones], axis=0),
                                 m_ref.at[e], None, a_ref.at[e])

    scores(i, 0)

    @pl.when(j_start == i)
    def _():
        consume(i, True, None, 0, None)

    @pl.when(j_start < i)
    def _():
        consume(i, True, j_start, 0, 1)
        _far_loop(lambda j, j_next, slot, next_slot: consume(j, False, j_next, slot, next_slot),
                  j_start, i - 1, FOX_FAR_UNROLL, first_slot=1)
        consume(i - 1, False, None, 1, None)

    o = [a_ref[e, :HEAD_DIM] * (1.0 / a_ref[e, HEAD_DIM:HEAD_DIM + 1]) for e in heads]
    o_ref[0] = jnp.concatenate([x.T for x in o], axis=1).astype(o_ref.dtype)


def fox_attention(q, k, vt, kc, c_cols, c_end, *, nheads, t):
    b, s, d = q.shape
    g = FOX_HEADS_PER_STEP
    gw = g * HEAD_DIM
    ng = nheads // g
    assert d == nheads * HEAD_DIM and nheads % g == 0 and s % t == 0 and c_end.shape == (b * ng, g, s // t)
    assert nheads & (nheads - 1) == 0 and C_PIECES * nheads + C_PIECES <= LANES
    return pl.pallas_call(
        functools.partial(_fox_attn_kernel, t=t, nheads=nheads),
        out_shape=jax.ShapeDtypeStruct((b, s, d), BF16),
        grid=(b, nheads // g, s // t),
        in_specs=[
            pl.BlockSpec((1, g, s // t), lambda bi, hp, i: (bi * ng + hp, 0, 0)),
            pl.BlockSpec((1, t, gw), lambda bi, hp, i: (bi, i, hp)),
            pl.BlockSpec((1, s, gw), lambda bi, hp, i: (bi, 0, hp)),
            pl.BlockSpec((gw, s), lambda bi, hp, i: (hp, bi)),
            pl.BlockSpec((1, s, LANES), lambda bi, hp, i: (bi, 0, 0)),
            pl.BlockSpec((1, t, LANES), lambda bi, hp, i: (bi, i, 0)),
        ],
        out_specs=pl.BlockSpec((1, t, gw), lambda bi, hp, i: (bi, i, hp)),
        scratch_shapes=[pltpu.VMEM((g, 1, t), F32), pltpu.VMEM((g, HEAD_DIM + ONES_ROWS, t), F32),
                        pltpu.VMEM((2, g, t, t), F32), pltpu.VMEM((g, 1, s // t), F32)],
        compiler_params=_params("parallel", "parallel", "arbitrary"),
        name="fox_attention",
    )(c_end, q, k, vt, kc, c_cols)


def _forget_scan_kernel(x_ref, g_ref, wt_ref, b_ref, kc_ref, ccol_ref, carry_ref, *, ts, nheads):
    @pl.when(pl.program_id(1) == 0)
    def _():
        carry_ref[...] = jnp.zeros_like(carry_ref)

    x = x_ref[0]
    hn = x * _rms_scale(x, NORM_EPS) * g_ref[...]
    hn_hi, hn_lo = (p.astype(BF16) for p in _split_bf16(hn, 2))
    w_hi, w_lo = (p.astype(BF16) for p in _split_bf16(wt_ref[...], 2))
    f = _dot_nt(w_hi, hn_hi) + _dot_nt(w_hi, hn_lo) + _dot_nt(w_lo, hn_hi) + b_ref[...]
    logf = -(jnp.maximum(-f, 0.0) + jnp.log1p(jnp.exp(-jnp.abs(f))))
    row = lax.broadcasted_iota(jnp.int32, (ts, ts), 0)
    col = lax.broadcasted_iota(jnp.int32, (ts, ts), 1)
    upper = jnp.where(row <= col, 1.0, 0.0).astype(BF16)
    c = carry_ref[:, 0:1]
    for piece in _split_bf16(logf, C_PIECES):
        c = c + _dot(piece.astype(BF16), upper)
    carry_ref[...] = jnp.broadcast_to(c[:, ts - 1:ts], carry_ref.shape)
    ccol_ref[0] = c.T

    hrow = lax.broadcasted_iota(jnp.int32, c.shape, 0)
    npc = C_PIECES * nheads
    aug = jnp.where(hrow < npc + C_PIECES, 1.0, 0.0)
    for p, piece in reversed(list(enumerate(_split_bf16(c * (-LOG2E), C_PIECES)))):
        aug = jnp.where(hrow < (p + 1) * nheads, piece, aug)
    kc_ref[0] = aug.T.astype(BF16)


def forget_scan(h, g, w_f_t, b_f, *, nheads, ts):
    b, s, d = h.shape
    kernel = functools.partial(_forget_scan_kernel, ts=ts, nheads=nheads)
    return pl.pallas_call(
        kernel,
        out_shape=(jax.ShapeDtypeStruct((b, s, LANES), BF16), jax.ShapeDtypeStruct((b, s, LANES), F32)),
        grid=(b, s // ts),
        in_specs=[
            pl.BlockSpec((1, ts, d), lambda bi, i: (bi, i, 0)),
            pl.BlockSpec((1, d), lambda bi, i: (0, 0)),
            pl.BlockSpec((LANES, d), lambda bi, i: (0, 0)),
            pl.BlockSpec((LANES, 1), lambda bi, i: (0, 0)),
        ],
        out_specs=(pl.BlockSpec((1, ts, LANES), lambda bi, i: (bi, i, 0)),
                   pl.BlockSpec((1, ts, LANES), lambda bi, i: (bi, i, 0))),
        scratch_shapes=[pltpu.VMEM((LANES, LANES), F32)],
        compiler_params=_params("parallel", "arbitrary"),
        name="forget_scan",
    )(h, g, w_f_t, b_f)


def _pick(n, target):
    t = min(n, target)
    while n % t:
        t //= 2
    return t


def kernel(x, rel_bias_table, attn_norm_g, mlp_norm_g, w_qkv_a, lam_q1, lam_k1, lam_q2, lam_k2, subln_g,
           w_o_a, kv_norm_g, w_k_b, w_v_b, w_f_b, b_f_b, w_q_b, w_o_b, w_mlp_in, w_mlp_out, final_norm_g):
    b, s, d = x.shape
    n = b * s
    depth = attn_norm_g.shape[0]
    assert depth == 2 and w_qkv_a.shape[0] == 1 and w_q_b.shape[0] == 1
    diff_heads = d // (2 * HEAD_DIM)
    fox_heads = d // HEAD_DIM
    dff = w_mlp_in.shape[2]

    t_attn = _pick(s, 512)
    tm = _pick(n, 1024)
    tf = _pick(dff, 512)
    ts = _pick(s, 512)

    row = lambda v: v.reshape(1, -1).astype(F32)
    x2 = x.reshape(n, d)

    bias_tiles = t5_bias_tiles(rel_bias_table.astype(F32), t=t_attn)
    g0 = row(attn_norm_g[0])
    w_q, w_k, w_v = (w_qkv_a[0, :, c * d:(c + 1) * d] for c in range(3))
    q = norm_matmul(x2, g0, w_q.astype(BF16), tm=tm, out_scale=Q_SCALE)
    k = norm_matmul(x2, g0, w_k.astype(BF16), tm=tm)
    vt = norm_matmul(x2, g0, w_v.T.astype(BF16), tm=tm, transpose_out=True)
    lamv = jnp.stack([lam_q1[0], lam_k1[0], lam_q2[0], lam_k2[0]]).astype(F32)
    lambda_init = 0.8 - 0.6 * math.exp(-0.3 * 0)
    o = diff_attention(q.reshape(b, s, d), k.reshape(b, s, d), vt, bias_tiles, lamv,
                       subln_g[0].reshape(-1, 1).astype(F32), nheads=diff_heads, t=t_attn, lambda_init=lambda_init)
    h = matmul_residual(o.reshape(n, d), w_o_a[0].astype(BF16), x2, tm=tm)
    h = mlp_block(h, row(mlp_norm_g[0]), w_mlp_in[0].astype(BF16), w_mlp_out[0].astype(BF16),
                  row(final_norm_g), tm=tm, tf=tf, final_norm=False)

    gkv = row(kv_norm_g)
    w_f_t = jnp.zeros((LANES, d), F32)
    b_f = jnp.zeros((LANES, 1), F32)
    for p in range(C_PIECES):
        w_f_t = w_f_t.at[p * fox_heads:(p + 1) * fox_heads].set(w_f_b.T.astype(F32))
        b_f = b_f.at[p * fox_heads:(p + 1) * fox_heads, 0].set(b_f_b.astype(F32))
    kc, c_cols = forget_scan(h.reshape(b, s, d), gkv, w_f_t, b_f, nheads=fox_heads, ts=ts)
    k = norm_matmul(h, gkv, w_k_b.astype(BF16), tm=tm)
    vt = norm_matmul(h, gkv, w_v_b.T.astype(BF16), tm=tm, transpose_out=True)

    q = norm_matmul(h, row(attn_norm_g[1]), w_q_b[0].astype(BF16), tm=tm, out_scale=Q_SCALE)
    c_end = c_cols[:, t_attn - 1::t_attn, :fox_heads].transpose(0, 2, 1).reshape(-1, FOX_HEADS_PER_STEP, s // t_attn)
    o = fox_attention(q.reshape(b, s, d), k.reshape(b, s, d), vt, kc, c_cols, c_end, nheads=fox_heads, t=t_attn)
    h = matmul_residual(o.reshape(n, d), w_o_b[0].astype(BF16), h, tm=tm)
    h = mlp_block(h, row(mlp_norm_g[1]), w_mlp_in[1].astype(BF16), w_mlp_out[1].astype(BF16),
                  row(final_norm_g), tm=tm, tf=tf, final_norm=True)
    return h.reshape(b, s, d)
```

```python
import functools
import math

import jax
import jax.numpy as jnp
from jax import lax
from jax.experimental import pallas as pl
from jax.experimental.pallas import tpu as pltpu

HEAD_DIM = 128
CHUNK = 64
REL_BUCKETS = 32
REL_MAX_DIST = 128
NORM_EPS = 1e-6
SUBLN_EPS = 1e-5
NEG = -1e30
SCALE = HEAD_DIM ** -0.5
LOG2E = math.log2(math.e)
Q_SCALE = SCALE * LOG2E
M_INIT = -1e30
DIFF_FAR_UNROLL = 4
FOX_FAR_UNROLL = 2
ONES_ROWS = 16
PRUNE_LOG2 = -170.0
FOX_HEADS_PER_STEP = 2

V7X_VMEM_BYTES = 64 * 1024 * 1024
V7X_VMEM_RESERVE_BYTES = 8 * 1024 * 1024
V7X_VMEM_LIMIT_BYTES = V7X_VMEM_BYTES - V7X_VMEM_RESERVE_BYTES
LANES = 128
CHUNK_SHIFT = CHUNK.bit_length() - 1
assert CHUNK == 1 << CHUNK_SHIFT
C_PIECES = 3

F32 = jnp.float32
BF16 = jnp.bfloat16


def _params(*semantics):
    return pltpu.CompilerParams(dimension_semantics=semantics, vmem_limit_bytes=V7X_VMEM_LIMIT_BYTES)


def _rms_scale(x, eps):
    return lax.rsqrt(jnp.mean(x * x, axis=-1, keepdims=True) + eps)


def _dot(a, b):
    return jnp.dot(a, b, preferred_element_type=F32)


def _dot_nt(a, b):
    return lax.dot_general(a, b, (((1,), (1,)), ((), ())), preferred_element_type=F32)


def _split_bf16(x, terms):
    pieces = []
    for _ in range(terms - 1):
        p = x.astype(BF16).astype(x.dtype)
        pieces.append(p)
        x = x - p
    pieces.append(x.astype(BF16).astype(x.dtype))
    return pieces


def _norm_matmul_kernel(x_ref, g_ref, w_ref, o_ref, *, transpose_out, out_scale):
    x = x_ref[...]
    hn = (x * _rms_scale(x, NORM_EPS) * g_ref[...]).astype(BF16)
    if transpose_out:
        y = _dot_nt(w_ref[...], hn)
    else:
        y = _dot(hn, w_ref[...])
    if out_scale != 1.0:
        y = y * out_scale
    o_ref[...] = y.astype(o_ref.dtype)


def norm_matmul(x, g, w, *, tm, transpose_out=False, out_scale=1.0):
    n, d = x.shape
    if transpose_out:
        nout = w.shape[0]
        out_shape = jax.ShapeDtypeStruct((nout, n), BF16)
        out_spec = pl.BlockSpec((nout, tm), lambda i: (0, i))
    else:
        nout = w.shape[1]
        out_shape = jax.ShapeDtypeStruct((n, nout), BF16)
        out_spec = pl.BlockSpec((tm, nout), lambda i: (i, 0))
    return pl.pallas_call(
        functools.partial(_norm_matmul_kernel, transpose_out=transpose_out, out_scale=out_scale),
        out_shape=out_shape,
        grid=(n // tm,),
        in_specs=[
            pl.BlockSpec((tm, d), lambda i: (i, 0)),
            pl.BlockSpec((1, d), lambda i: (0, 0)),
            pl.BlockSpec(w.shape, lambda i: (0, 0)),
        ],
        out_specs=out_spec,
        compiler_params=_params("parallel"),
        name="norm_matmul_t" if transpose_out else "norm_matmul",
    )(x, g, w)


def _matmul_residual_kernel(a_ref, w_ref, r_ref, o_ref):
    o_ref[...] = r_ref[...] + _dot(a_ref[...], w_ref[...])


def matmul_residual(a, w, res, *, tm):
    n, k = a.shape
    nout = w.shape[1]
    return pl.pallas_call(
        _matmul_residual_kernel,
        out_shape=jax.ShapeDtypeStruct((n, nout), F32),
        grid=(n // tm,),
        in_specs=[
            pl.BlockSpec((tm, k), lambda i: (i, 0)),
            pl.BlockSpec((k, nout), lambda i: (0, 0)),
            pl.BlockSpec((tm, nout), lambda i: (i, 0)),
        ],
        out_specs=pl.BlockSpec((tm, nout), lambda i: (i, 0)),
        compiler_params=_params("parallel"),
        name="matmul_residual",
    )(a, w, res)


def _mlp_kernel(x_ref, g_ref, win_ref, wout_ref, gf_ref, o_ref, hn_ref, *, final_norm):
    f = pl.program_id(1)

    @pl.when(f == 0)
    def _():
        x = x_ref[...]
        hn_ref[...] = (x * _rms_scale(x, NORM_EPS) * g_ref[...]).astype(hn_ref.dtype)
        o_ref[...] = x

    u = _dot(hn_ref[...], win_ref[...])
    u = jnp.square(jnp.maximum(u, 0.0)).astype(BF16)
    o_ref[...] += _dot(u, wout_ref[...])

    if final_norm:
        @pl.when(f == pl.num_programs(1) - 1)
        def _():
            y = o_ref[...]
            o_ref[...] = y * _rms_scale(y, NORM_EPS) * gf_ref[...]


def mlp_block(x, g, w_in, w_out, g_final, *, tm, tf, final_norm):
    n, d = x.shape
    dff = w_in.shape[1]
    return pl.pallas_call(
        functools.partial(_mlp_kernel, final_norm=final_norm),
        out_shape=jax.ShapeDtypeStruct((n, d), F32),
        grid=(n // tm, dff // tf),
        in_specs=[
            pl.BlockSpec((tm, d), lambda i, f: (i, 0)),
            pl.BlockSpec((1, d), lambda i, f: (0, 0)),
            pl.BlockSpec((d, tf), lambda i, f: (0, f)),
            pl.BlockSpec((tf, d), lambda i, f: (f, 0)),
            pl.BlockSpec((1, d), lambda i, f: (0, 0)),
        ],
        out_specs=pl.BlockSpec((tm, d), lambda i, f: (i, 0)),
        scratch_shapes=[pltpu.VMEM((tm, d), BF16)],
        compiler_params=_params("parallel", "arbitrary"),
        name="mlp_block",
    )(x, g, w_in, w_out, g_final)


def _t5_bias_kernel(table_ref, o_ref, *, t):
    h = pl.program_id(0)
    half = REL_BUCKETS // 2
    max_exact = half // 2
    ki = lax.broadcasted_iota(jnp.int32, (t, t), 0)
    qi = lax.broadcasted_iota(jnp.int32, (t, t), 1)
    far = table_ref[half - 1, h]
    for d in range(2):
        rel = ki - d * t - qi
        ret = jnp.where(rel > 0, half, 0)
        n = jnp.abs(rel)
        nf = jnp.maximum(n, 1).astype(F32)
        large = max_exact + (jnp.log(nf / max_exact) / math.log(REL_MAX_DIST / max_exact)
                             * (half - max_exact)).astype(jnp.int32)
        large = jnp.minimum(large, half - 1)
        bucket = ret + jnp.where(n < max_exact, n, large)
        val = jnp.zeros((t, t), F32)
        for b in range(REL_BUCKETS):
            val = jnp.where(bucket == b, table_ref[b, h], val)
        val = (val - far) * LOG2E
        if d == 0:
            mask = (ki >> CHUNK_SHIFT) <= (qi >> CHUNK_SHIFT)
            val = jnp.where(mask, val, NEG)
        o_ref[0, d] = val


def t5_bias_tiles(rel_table, *, t):
    nheads = rel_table.shape[1]
    assert t % CHUNK == 0 and t >= REL_MAX_DIST
    return pl.pallas_call(
        functools.partial(_t5_bias_kernel, t=t),
        out_shape=jax.ShapeDtypeStruct((nheads, 2, t, t), F32),
        grid=(nheads,),
        in_specs=[pl.BlockSpec(memory_space=pltpu.SMEM)],
        out_specs=pl.BlockSpec((1, 2, t, t), lambda h: (h, 0, 0, 0)),
        compiler_params=_params("arbitrary"),
        name="t5_bias_tiles",
    )(rel_table)


def _init_stats(m_ref, l_ref, acc_ref):
    m_ref[...] = jnp.full_like(m_ref, M_INIT)
    if l_ref is not None:
        l_ref[...] = jnp.zeros_like(l_ref)
    acc_ref[...] = jnp.zeros_like(acc_ref)


def _online_softmax_step(load_z, vt, m_ref, l_ref, acc_ref):
    m_prev = m_ref[...]
    m_new = jnp.maximum(m_prev, jnp.max(load_z(), axis=0, keepdims=True))
    p = jnp.exp2(load_z() - m_new)
    alpha = jnp.exp2(m_prev - m_new)
    if l_ref is not None:
        l_ref[...] = alpha * l_ref[...] + jnp.sum(p, axis=0, keepdims=True)
    acc_ref[...] = alpha * acc_ref[...] + _dot(vt, p.astype(BF16))
    m_ref[...] = m_new


def _far_loop(consume, j_start, j_end, unroll, first_slot=0):
    assert unroll % 2 == 0

    def body(u):
        def run(jj, carry):
            for r in range(u):
                slot = (first_slot + r) % 2 if u > 1 else first_slot
                next_slot = (first_slot + r + 1) % 2 if u > 1 else first_slot
                consume(j_start + jj * u + r, j_start + jj * u + r + 1, slot, next_slot)
            return carry
        return run

    n_far = j_end - j_start
    n_main = n_far // unroll
    lax.fori_loop(0, n_main, body(unroll), 0)
    lax.fori_loop(n_main * unroll, n_far, body(1), 0)


def _diff_attn_kernel(q_ref, k_ref, vt_ref, bias_ref, lamv_ref, g_ref, o_ref,
                      m0, l0, a0, m1, l1, a1, z_ref, *, t, lambda_init):
    i = pl.program_id(2)
    stats = ((m0, l0, a0), (m1, l1, a1))
    for s in stats:
        _init_stats(*s)
    q = q_ref[0]

    def scores(j, slot):
        start = pl.multiple_of(j * t, t)
        k = k_ref[0, pl.ds(start, t), :]
        for c in range(2):
            half = slice(c * HEAD_DIM, (c + 1) * HEAD_DIM)
            z_ref[slot, c] = _dot_nt(k[:, half], q[:, half])

    def consume(j, bias_idx, j_next, slot, next_slot):
        if j_next is not None and next_slot == slot:
            held = [z_ref[slot, c] for c in range(2)]
            load = lambda c: held[c]
        else:
            load = lambda c: z_ref[slot, c]
        if j_next is not None:
            scores(j_next, next_slot)
        start = pl.multiple_of(j * t, t)
        vt = vt_ref[:, pl.ds(start, t)]
        for c, s in enumerate(stats):
            if bias_idx is None:
                load_z = functools.partial(load, c)
            else:
                load_z = lambda c=c: load(c) + bias_ref[0, bias_idx]
            _online_softmax_step(load_z, vt, *s)

    scores(0, 0)
    _far_loop(lambda j, j_next, slot, next_slot: consume(j, None, j_next, slot, next_slot),
              0, jnp.maximum(i - 1, 0), DIFF_FAR_UNROLL)

    @pl.when(i >= 1)
    def _():
        consume(i - 1, 1, i, 0, 1)
        consume(i, 0, None, 1, None)

    @pl.when(i == 0)
    def _():
        consume(i, 0, None, 0, None)

    lamv = lamv_ref[...]
    lam = (jnp.exp(jnp.sum(lamv[0:1] * lamv[1:2], axis=1, keepdims=True))
           - jnp.exp(jnp.sum(lamv[2:3] * lamv[3:4], axis=1, keepdims=True)) + lambda_init)
    o = a0[...] * (1.0 / l0[...]) - lam * (a1[...] * (1.0 / l1[...]))
    ms = jnp.mean(o * o, axis=0, keepdims=True)
    o = o * lax.rsqrt(ms + SUBLN_EPS) * g_ref[...] * (1.0 - lambda_init)
    o_ref[0] = o.T.astype(o_ref.dtype)


def diff_attention(q, k, vt, bias_tiles, lamv, subln_g_col, *, nheads, t, lambda_init):
    b, s, d = q.shape
    hw = 2 * HEAD_DIM
    assert d == nheads * hw and s % t == 0
    kernel = functools.partial(_diff_attn_kernel, t=t, lambda_init=lambda_init)
    stat = lambda: pltpu.VMEM((1, t), F32)
    acc = lambda: pltpu.VMEM((hw, t), F32)
    return pl.pallas_call(
        kernel,
        out_shape=jax.ShapeDtypeStruct((b, s, d), BF16),
        grid=(b, nheads, s // t),
        in_specs=[
            pl.BlockSpec((1, t, hw), lambda bi, h, i: (bi, i, h)),
            pl.BlockSpec((1, s, hw), lambda bi, h, i: (bi, 0, h)),
            pl.BlockSpec((hw, s), lambda bi, h, i: (h, bi)),
            pl.BlockSpec((1, 2, t, t), lambda bi, h, i: (h, 0, 0, 0)),
            pl.BlockSpec((4, HEAD_DIM), lambda bi, h, i: (0, 0)),
            pl.BlockSpec((hw, 1), lambda bi, h, i: (0, 0)),
        ],
        out_specs=pl.BlockSpec((1, t, hw), lambda bi, h, i: (bi, i, h)),
        scratch_shapes=[stat(), stat(), acc(), stat(), stat(), acc(), pltpu.VMEM((2, 2, t, t), F32)],
        compiler_params=_params("parallel", "parallel", "arbitrary"),
        name="diff_attention",
    )(q, k, vt, bias_tiles, lamv, subln_g_col)


def _row_sq_norm_max(x):
    xf = x.astype(F32)
    return jnp.max(jnp.sum(xf * xf, axis=1, keepdims=True), axis=0, keepdims=True)


def _fox_attn_kernel(cend_ref, q_ref, k_ref, vt_ref, kc_ref, cq_ref, o_ref, m_ref, a_ref, z_ref, kn_ref,
                     *, t, nheads):
    hp = pl.program_id(1)
    i = pl.program_id(2)
    nk = pl.num_programs(2)
    heads = range(FOX_HEADS_PER_STEP)
    col = lambda e: slice(e * HEAD_DIM, (e + 1) * HEAD_DIM)
    tile_lane = lax.broadcasted_iota(jnp.int32, (1, kn_ref.shape[-1]), 1)
    for e in heads:
        _init_stats(m_ref.at[e], None, a_ref.at[e])

    @pl.when(i == 0)
    def _():
        def norm_body(j, carry):
            start = pl.multiple_of(j * t, t)
            k = k_ref[0, pl.ds(start, t), :]
            out = []
            for e, (running, kn) in zip(heads, carry):
                running = jnp.maximum(running, _row_sq_norm_max(k[:, col(e)]))
                out.append((running, jnp.where(tile_lane == j, running, kn)))
            return tuple(out)

        init = tuple((jnp.zeros((1, 1), F32), jnp.zeros(tile_lane.shape, F32)) for _ in heads)
        for e, (_, kn) in zip(heads, lax.fori_loop(0, nk, norm_body, init)):
            kn_ref[e] = kn

    cq = cq_ref[0]
    lane = lax.broadcasted_iota(jnp.int32, cq.shape, 1)
    npc = C_PIECES * nheads
    k_diag = k_ref[0, pl.ds(pl.multiple_of(i * t, t), t), :]
    qs, firsts = [], []
    for e in heads:
        h = hp * FOX_HEADS_PER_STEP + e
        q_e = q_ref[0][:, col(e)]
        ct = jnp.sum(jnp.where(lane == h, cq, 0.0), axis=1, keepdims=True) * LOG2E
        qc = jnp.where(lane < npc, jnp.where((lane & (nheads - 1)) == h, 1.0, 0.0), 0.0)
        for p, piece in enumerate(_split_bf16(ct, C_PIECES)):
            qc = jnp.where(lane == npc + p, piece, qc)
        qs.append(jnp.concatenate([q_e, qc.astype(BF16)], axis=1))

        q2 = _row_sq_norm_max(q_e)
        m_lo = jnp.min(jnp.sum(q_e.astype(F32) * k_diag[:, col(e)].astype(F32), axis=1, keepdims=True),
                       axis=0, keepdims=True)
        c_hi = jnp.max(ct, axis=0, keepdims=True)
        r = PRUNE_LOG2 + m_lo - c_hi + cend_ref[0, e:e + 1, :] * LOG2E
        needed = jnp.logical_or(r <= 0.0, q2 * kn_ref[e] >= r * r)
        first = jnp.where(jnp.logical_and(needed, tile_lane < i), tile_lane, i).astype(F32)
        firsts.append(jnp.min(first, axis=1, keepdims=True))
    j_start = functools.reduce(jnp.minimum, firsts)[0, 0].astype(jnp.int32)

    def scores(j, slot):
        start = pl.multiple_of(j * t, t)
        k = k_ref[0, pl.ds(start, t), :]
        kc = kc_ref[0, pl.ds(start, t), :]
        for e in heads:
            z_ref[slot, e] = _dot_nt(jnp.concatenate([k[:, col(e)], kc], axis=1), qs[e])

    def consume(j, diagonal, j_next, slot, next_slot):
        if j_next is not None and next_slot == slot:
            held = [z_ref[slot, e] for e in heads]
            load = lambda e: held[e]
        else:
            load = lambda e: z_ref[slot, e]
        if j_next is not None:
            scores(j_next, next_slot)
        start = pl.multiple_of(j * t, t)
        vt = vt_ref[:, pl.ds(start, t)]
        ones = jnp.ones((ONES_ROWS, t), BF16)
        for e in heads:
            if diagonal:
                def load_z(e=e):
                    ki = lax.broadcasted_iota(jnp.int32, (t, t), 0)
                    qi = lax.broadcasted_iota(jnp.int32, (t, t), 1)
                    return jnp.where(ki <= qi, load(e), NEG)
            else:
                load_z = functools.partial(load, e)
            _online_softmax_step(load_z, jnp.concatenate([vt[col(e)], ones], axis=0),
                                 m_ref.at[e], None, a_ref.at[e])

    scores(i, 0)

    @pl.when(j_start == i)
    def _():
        consume(i, True, None, 0, None)

    @pl.when(j_start < i)
    def _():
        consume(i, True, j_start, 0, 1)
        _far_loop(lambda j, j_next, slot, next_slot: consume(j, False, j_next, slot, next_slot),
                  j_start, i - 1, FOX_FAR_UNROLL, first_slot=1)
        consume(i - 1, False, None, 1, None)

    o = [a_ref[e, :HEAD_DIM] * (1.0 / a_ref[e, HEAD_DIM:HEAD_DIM + 1]) for e in heads]
    o_ref[0] = jnp.concatenate([x.T for x in o], axis=1).astype(o_ref.dtype)


def fox_attention(q, k, vt, kc, c_cols, c_end, *, nheads, t):
    b, s, d = q.shape
    g = FOX_HEADS_PER_STEP
    gw = g * HEAD_DIM
    ng = nheads // g
    assert d == nheads * HEAD_DIM and nheads % g == 0 and s % t == 0 and c_end.shape == (b * ng, g, s // t)
    assert nheads & (nheads - 1) == 0 and C_PIECES * nheads + C_PIECES <= LANES
    return pl.pallas_call(
        functools.partial(_fox_attn_kernel, t=t, nheads=nheads),
        out_shape=jax.ShapeDtypeStruct((b, s, d), BF16),
        grid=(b, nheads // g, s // t),
        in_specs=[
            pl.BlockSpec((1, g, s // t), lambda bi, hp, i: (bi * ng + hp, 0, 0)),
            pl.BlockSpec((1, t, gw), lambda bi, hp, i: (bi, i, hp)),
            pl.BlockSpec((1, s, gw), lambda bi, hp, i: (bi, 0, hp)),
            pl.BlockSpec((gw, s), lambda bi, hp, i: (hp, bi)),
            pl.BlockSpec((1, s, LANES), lambda bi, hp, i: (bi, 0, 0)),
            pl.BlockSpec((1, t, LANES), lambda bi, hp, i: (bi, i, 0)),
        ],
        out_specs=pl.BlockSpec((1, t, gw), lambda bi, hp, i: (bi, i, hp)),
        scratch_shapes=[pltpu.VMEM((g, 1, t), F32), pltpu.VMEM((g, HEAD_DIM + ONES_ROWS, t), F32),
                        pltpu.VMEM((2, g, t, t), F32), pltpu.VMEM((g, 1, s // t), F32)],
        compiler_params=_params("parallel", "parallel", "arbitrary"),
        name="fox_attention",
    )(c_end, q, k, vt, kc, c_cols)


def _forget_scan_kernel(x_ref, g_ref, wt_ref, b_ref, kc_ref, ccol_ref, carry_ref, *, ts, nheads):
    @pl.when(pl.program_id(1) == 0)
    def _():
        carry_ref[...] = jnp.zeros_like(carry_ref)

    x = x_ref[0]
    hn = x * _rms_scale(x, NORM_EPS) * g_ref[...]
    hn_hi, hn_lo = (p.astype(BF16) for p in _split_bf16(hn, 2))
    w_hi, w_lo = (p.astype(BF16) for p in _split_bf16(wt_ref[...], 2))
    f = _dot_nt(w_hi, hn_hi) + _dot_nt(w_hi, hn_lo) + _dot_nt(w_lo, hn_hi) + b_ref[...]
    logf = -(jnp.maximum(-f, 0.0) + jnp.log1p(jnp.exp(-jnp.abs(f))))
    row = lax.broadcasted_iota(jnp.int32, (ts, ts), 0)
    col = lax.broadcasted_iota(jnp.int32, (ts, ts), 1)
    upper = jnp.where(row <= col, 1.0, 0.0).astype(BF16)
    c = carry_ref[:, 0:1]
    for piece in _split_bf16(logf, C_PIECES):
        c = c + _dot(piece.astype(BF16), upper)
    carry_ref[...] = jnp.broadcast_to(c[:, ts - 1:ts], carry_ref.shape)
    ccol_ref[0] = c.T

    hrow = lax.broadcasted_iota(jnp.int32, c.shape, 0)
    npc = C_PIECES * nheads
    aug = jnp.where(hrow < npc + C_PIECES, 1.0, 0.0)
    for p, piece in reversed(list(enumerate(_split_bf16(c * (-LOG2E), C_PIECES)))):
        aug = jnp.where(hrow < (p + 1) * nheads, piece, aug)
    kc_ref[0] = aug.T.astype(BF16)


def forget_scan(h, g, w_f_t, b_f, *, nheads, ts):
    b, s, d = h.shape
    kernel = functools.partial(_forget_scan_kernel, ts=ts, nheads=nheads)
    return pl.pallas_call(
        kernel,
        out_shape=(jax.ShapeDtypeStruct((b, s, LANES), BF16), jax.ShapeDtypeStruct((b, s, LANES), F32)),
        grid=(b, s // ts),
        in_specs=[
            pl.BlockSpec((1, ts, d), lambda bi, i: (bi, i, 0)),
            pl.BlockSpec((1, d), lambda bi, i: (0, 0)),
            pl.BlockSpec((LANES, d), lambda bi, i: (0, 0)),
            pl.BlockSpec((LANES, 1), lambda bi, i: (0, 0)),
        ],
        out_specs=(pl.BlockSpec((1, ts, LANES), lambda bi, i: (bi, i, 0)),
                   pl.BlockSpec((1, ts, LANES), lambda bi, i: (bi, i, 0))),
        scratch_shapes=[pltpu.VMEM((LANES, LANES), F32)],
        compiler_params=_params("parallel", "arbitrary"),
        name="forget_scan",
    )(h, g, w_f_t, b_f)


def _pick(n, target):
    t = min(n, target)
    while n % t:
        t //= 2
    return t


def kernel(x, rel_bias_table, attn_norm_g, mlp_norm_g, w_qkv_a, lam_q1, lam_k1, lam_q2, lam_k2, subln_g,
           w_o_a, kv_norm_g, w_k_b, w_v_b, w_f_b, b_f_b, w_q_b, w_o_b, w_mlp_in, w_mlp_out, final_norm_g):
    b, s, d = x.shape
    n = b * s
    depth = attn_norm_g.shape[0]
    assert depth == 2 and w_qkv_a.shape[0] == 1 and w_q_b.shape[0] == 1
    diff_heads = d // (2 * HEAD_DIM)
    fox_heads = d // HEAD_DIM
    dff = w_mlp_in.shape[2]

    t_attn = _pick(s, 512)
    tm = _pick(n, 1024)
    tf = _pick(dff, 512)
    ts = _pick(s, 512)

    row = lambda v: v.reshape(1, -1).astype(F32)
    x2 = x.reshape(n, d)

    bias_tiles = t5_bias_tiles(rel_bias_table.astype(F32), t=t_attn)
    g0 = row(attn_norm_g[0])
    w_qkv = w_qkv_a[0].astype(BF16)
    w_q, w_k, w_v = (w_qkv[:, c * d:(c + 1) * d] for c in range(3))
    q = norm_matmul(x2, g0, w_q, tm=tm, out_scale=Q_SCALE)
    k = norm_matmul(x2, g0, w_k, tm=tm)
    vt = norm_matmul(x2, g0, w_v.T, tm=tm, transpose_out=True)
    lamv = jnp.stack([lam_q1[0], lam_k1[0], lam_q2[0], lam_k2[0]]).astype(F32)
    lambda_init = 0.8 - 0.6 * math.exp(-0.3 * 0)
    o = diff_attention(q.reshape(b, s, d), k.reshape(b, s, d), vt, bias_tiles, lamv,
                       subln_g[0].reshape(-1, 1).astype(F32), nheads=diff_heads, t=t_attn, lambda_init=lambda_init)
    h = matmul_residual(o.reshape(n, d), w_o_a[0].astype(BF16), x2, tm=tm)
    h = mlp_block(h, row(mlp_norm_g[0]), w_mlp_in[0].astype(BF16), w_mlp_out[0].astype(BF16),
                  row(final_norm_g), tm=tm, tf=tf, final_norm=False)

    gkv = row(kv_norm_g)
    w_f_t = jnp.zeros((LANES, d), F32)
    b_f = jnp.zeros((LANES, 1), F32)
    for p in range(C_PIECES):
        w_f_t = w_f_t.at[p * fox_heads:(p + 1) * fox_heads].set(w_f_b.T.astype(F32))
        b_f = b_f.at[p * fox_heads:(p + 1) * fox_heads, 0].set(b_f_b.astype(F32))
    kc, c_cols = forget_scan(h.reshape(b, s, d), gkv, w_f_t, b_f, nheads=fox_heads, ts=ts)
    k = norm_matmul(h, gkv, w_k_b.astype(BF16), tm=tm)
    vt = norm_matmul(h, gkv, w_v_b.astype(BF16).T, tm=tm, transpose_out=True)

    q = norm_matmul(h, row(attn_norm_g[1]), w_q_b[0].astype(BF16), tm=tm, out_scale=Q_SCALE)
    c_end = c_cols[:, t_attn - 1::t_attn, :fox_heads].transpose(0, 2, 1).reshape(-1, FOX_HEADS_PER_STEP, s // t_attn)
    o = fox_attention(q.reshape(b, s, d), k.reshape(b, s, d), vt, kc, c_cols, c_end, nheads=fox_heads, t=t_attn)
    h = matmul_residual(o.reshape(n, d), w_o_b[0].astype(BF16), h, tm=tm)
    h = mlp_block(h, row(mlp_norm_g[1]), w_mlp_in[1].astype(BF16), w_mlp_out[1].astype(BF16),
                  row(final_norm_g), tm=tm, tf=tf, final_norm=True)
    return h.reshape(b, s, d)
```

```python
import functools
import math

import jax
import jax.numpy as jnp
from jax import lax
from jax.experimental import pallas as pl
from jax.experimental.pallas import tpu as pltpu

HEAD_DIM = 128
CHUNK = 64
REL_BUCKETS = 32
REL_MAX_DIST = 128
NORM_EPS = 1e-6
SUBLN_EPS = 1e-5
NEG = -1e30
SCALE = HEAD_DIM ** -0.5
LOG2E = math.log2(math.e)
Q_SCALE = SCALE * LOG2E
M_INIT = -1e30
DIFF_FAR_UNROLL = 4
FOX_FAR_UNROLL = 2
ONES_ROWS = 16
PRUNE_LOG2 = -170.0
FOX_HEADS_PER_STEP = 2

V7X_VMEM_BYTES = 64 * 1024 * 1024
V7X_VMEM_RESERVE_BYTES = 8 * 1024 * 1024
V7X_VMEM_LIMIT_BYTES = V7X_VMEM_BYTES - V7X_VMEM_RESERVE_BYTES
LANES = 128
CHUNK_SHIFT = CHUNK.bit_length() - 1
assert CHUNK == 1 << CHUNK_SHIFT
C_PIECES = 3

F32 = jnp.float32
BF16 = jnp.bfloat16


def _params(*semantics):
    return pltpu.CompilerParams(dimension_semantics=semantics, vmem_limit_bytes=V7X_VMEM_LIMIT_BYTES)


def _rms_scale(x, eps):
    return lax.rsqrt(jnp.mean(x * x, axis=-1, keepdims=True) + eps)


def _dot(a, b):
    return jnp.dot(a, b, preferred_element_type=F32)


def _dot_nt(a, b):
    return lax.dot_general(a, b, (((1,), (1,)), ((), ())), preferred_element_type=F32)


def _split_bf16(x, terms):
    pieces = []
    for _ in range(terms - 1):
        p = x.astype(BF16).astype(x.dtype)
        pieces.append(p)
        x = x - p
    pieces.append(x.astype(BF16).astype(x.dtype))
    return pieces


def _norm_matmul_kernel(x_ref, g_ref, w_ref, o_ref, *, transpose_out, out_scale):
    x = x_ref[...]
    hn = (x * _rms_scale(x, NORM_EPS) * g_ref[...]).astype(BF16)
    if transpose_out:
        y = _dot_nt(w_ref[...], hn)
    else:
        y = _dot(hn, w_ref[...])
    if out_scale != 1.0:
        y = y * out_scale
    o_ref[...] = y.astype(o_ref.dtype)


def norm_matmul(x, g, w, *, tm, transpose_out=False, out_scale=1.0, col_block=0, nout=None):
    n, d = x.shape
    if transpose_out:
        assert col_block == 0 and nout is None
        nout = w.shape[0]
        w_spec = pl.BlockSpec((nout, d), lambda i: (0, 0))
        out_shape = jax.ShapeDtypeStruct((nout, n), BF16)
        out_spec = pl.BlockSpec((nout, tm), lambda i: (0, i))
    else:
        nout = w.shape[1] if nout is None else nout
        w_spec = pl.BlockSpec((d, nout), lambda i: (0, col_block))
        out_shape = jax.ShapeDtypeStruct((n, nout), BF16)
        out_spec = pl.BlockSpec((tm, nout), lambda i: (i, 0))
    return pl.pallas_call(
        functools.partial(_norm_matmul_kernel, transpose_out=transpose_out, out_scale=out_scale),
        out_shape=out_shape,
        grid=(n // tm,),
        in_specs=[
            pl.BlockSpec((tm, d), lambda i: (i, 0)),
            pl.BlockSpec((1, d), lambda i: (0, 0)),
            w_spec,
        ],
        out_specs=out_spec,
        compiler_params=_params("parallel"),
        name="norm_matmul_t" if transpose_out else "norm_matmul",
    )(x, g, w)


def _matmul_residual_kernel(a_ref, w_ref, r_ref, o_ref):
    o_ref[...] = r_ref[...] + _dot(a_ref[...], w_ref[...])


def matmul_residual(a, w, res, *, tm):
    n, k = a.shape
    nout = w.shape[1]
    return pl.pallas_call(
        _matmul_residual_kernel,
        out_shape=jax.ShapeDtypeStruct((n, nout), F32),
        grid=(n // tm,),
        in_specs=[
            pl.BlockSpec((tm, k), lambda i: (i, 0)),
            pl.BlockSpec((k, nout), lambda i: (0, 0)),
            pl.BlockSpec((tm, nout), lambda i: (i, 0)),
        ],
        out_specs=pl.BlockSpec((tm, nout), lambda i: (i, 0)),
        compiler_params=_params("parallel"),
        name="matmul_residual",
    )(a, w, res)


def _mlp_kernel(x_ref, g_ref, win_ref, wout_ref, gf_ref, o_ref, hn_ref, *, final_norm):
    f = pl.program_id(1)

    @pl.when(f == 0)
    def _():
        x = x_ref[...]
        hn_ref[...] = (x * _rms_scale(x, NORM_EPS) * g_ref[...]).astype(hn_ref.dtype)
        o_ref[...] = x

    u = _dot(hn_ref[...], win_ref[...])
    u = jnp.square(jnp.maximum(u, 0.0)).astype(BF16)
    o_ref[...] += _dot(u, wout_ref[...])

    if final_norm:
        @pl.when(f == pl.num_programs(1) - 1)
        def _():
            y = o_ref[...]
            o_ref[...] = y * _rms_scale(y, NORM_EPS) * gf_ref[...]


def mlp_block(x, g, w_in, w_out, g_final, *, layer, tm, tf, final_norm):
    n, d = x.shape
    dff = w_in.shape[2]
    return pl.pallas_call(
        functools.partial(_mlp_kernel, final_norm=final_norm),
        out_shape=jax.ShapeDtypeStruct((n, d), F32),
        grid=(n // tm, dff // tf),
        in_specs=[
            pl.BlockSpec((tm, d), lambda i, f: (i, 0)),
            pl.BlockSpec((1, d), lambda i, f: (0, 0)),
            pl.BlockSpec((None, d, tf), lambda i, f: (layer, 0, f)),
            pl.BlockSpec((None, tf, d), lambda i, f: (layer, f, 0)),
            pl.BlockSpec((1, d), lambda i, f: (0, 0)),
        ],
        out_specs=pl.BlockSpec((tm, d), lambda i, f: (i, 0)),
        scratch_shapes=[pltpu.VMEM((tm, d), BF16)],
        compiler_params=_params("parallel", "arbitrary"),
        name="mlp_block",
    )(x, g, w_in, w_out, g_final)


def _t5_bias_kernel(table_ref, o_ref, *, t):
    h = pl.program_id(0)
    half = REL_BUCKETS // 2
    max_exact = half // 2
    ki = lax.broadcasted_iota(jnp.int32, (t, t), 0)
    qi = lax.broadcasted_iota(jnp.int32, (t, t), 1)
    far = table_ref[half - 1, h]
    for d in range(2):
        rel = ki - d * t - qi
        ret = jnp.where(rel > 0, half, 0)
        n = jnp.abs(rel)
        nf = jnp.maximum(n, 1).astype(F32)
        large = max_exact + (jnp.log(nf / max_exact) / math.log(REL_MAX_DIST / max_exact)
                             * (half - max_exact)).astype(jnp.int32)
        large = jnp.minimum(large, half - 1)
        bucket = ret + jnp.where(n < max_exact, n, large)
        val = jnp.zeros((t, t), F32)
        for b in range(REL_BUCKETS):
            val = jnp.where(bucket == b, table_ref[b, h], val)
        val = (val - far) * LOG2E
        if d == 0:
            mask = (ki >> CHUNK_SHIFT) <= (qi >> CHUNK_SHIFT)
            val = jnp.where(mask, val, NEG)
        o_ref[0, d] = val


def t5_bias_tiles(rel_table, *, t):
    nheads = rel_table.shape[1]
    assert t % CHUNK == 0 and t >= REL_MAX_DIST
    return pl.pallas_call(
        functools.partial(_t5_bias_kernel, t=t),
        out_shape=jax.ShapeDtypeStruct((nheads, 2, t, t), F32),
        grid=(nheads,),
        in_specs=[pl.BlockSpec(memory_space=pltpu.SMEM)],
        out_specs=pl.BlockSpec((1, 2, t, t), lambda h: (h, 0, 0, 0)),
        compiler_params=_params("arbitrary"),
        name="t5_bias_tiles",
    )(rel_table)


def _init_stats(m_ref, l_ref, acc_ref):
    m_ref[...] = jnp.full_like(m_ref, M_INIT)
    if l_ref is not None:
        l_ref[...] = jnp.zeros_like(l_ref)
    acc_ref[...] = jnp.zeros_like(acc_ref)


def _online_softmax_step(load_z, vt, m_ref, l_ref, acc_ref):
    m_prev = m_ref[...]
    m_new = jnp.maximum(m_prev, jnp.max(load_z(), axis=0, keepdims=True))
    p = jnp.exp2(load_z() - m_new)
    alpha = jnp.exp2(m_prev - m_new)
    if l_ref is not None:
        l_ref[...] = alpha * l_ref[...] + jnp.sum(p, axis=0, keepdims=True)
    acc_ref[...] = alpha * acc_ref[...] + _dot(vt, p.astype(BF16))
    m_ref[...] = m_new


def _far_loop(consume, j_start, j_end, unroll, first_slot=0):
    assert unroll % 2 == 0

    def body(u):
        def run(jj, carry):
            for r in range(u):
                slot = (first_slot + r) % 2 if u > 1 else first_slot
                next_slot = (first_slot + r + 1) % 2 if u > 1 else first_slot
                consume(j_start + jj * u + r, j_start + jj * u + r + 1, slot, next_slot)
            return carry
        return run

    n_far = j_end - j_start
    n_main = n_far // unroll
    lax.fori_loop(0, n_main, body(unroll), 0)
    lax.fori_loop(n_main * unroll, n_far, body(1), 0)


def _diff_attn_kernel(q_ref, k_ref, vt_ref, bias_ref, lamv_ref, g_ref, o_ref,
                      m0, l0, a0, m1, l1, a1, z_ref, *, t, lambda_init):
    i = pl.program_id(2)
    stats = ((m0, l0, a0), (m1, l1, a1))
    for s in stats:
        _init_stats(*s)
    q = q_ref[0]

    def scores(j, slot):
        start = pl.multiple_of(j * t, t)
        k = k_ref[0, pl.ds(start, t), :]
        for c in range(2):
            half = slice(c * HEAD_DIM, (c + 1) * HEAD_DIM)
            z_ref[slot, c] = _dot_nt(k[:, half], q[:, half])

    def consume(j, bias_idx, j_next, slot, next_slot):
        if j_next is not None and next_slot == slot:
            held = [z_ref[slot, c] for c in range(2)]
            load = lambda c: held[c]
        else:
            load = lambda c: z_ref[slot, c]
        if j_next is not None:
            scores(j_next, next_slot)
        start = pl.multiple_of(j * t, t)
        vt = vt_ref[:, pl.ds(start, t)]
        for c, s in enumerate(stats):
            if bias_idx is None:
                load_z = functools.partial(load, c)
            else:
                load_z = lambda c=c: load(c) + bias_ref[0, bias_idx]
            _online_softmax_step(load_z, vt, *s)

    scores(0, 0)
    _far_loop(lambda j, j_next, slot, next_slot: consume(j, None, j_next, slot, next_slot),
              0, jnp.maximum(i - 1, 0), DIFF_FAR_UNROLL)

    @pl.when(i >= 1)
    def _():
        consume(i - 1, 1, i, 0, 1)
        consume(i, 0, None, 1, None)

    @pl.when(i == 0)
    def _():
        consume(i, 0, None, 0, None)

    lamv = lamv_ref[...]
    lam = (jnp.exp(jnp.sum(lamv[0:1] * lamv[1:2], axis=1, keepdims=True))
           - jnp.exp(jnp.sum(lamv[2:3] * lamv[3:4], axis=1, keepdims=True)) + lambda_init)
    o = a0[...] * (1.0 / l0[...]) - lam * (a1[...] * (1.0 / l1[...]))
    ms = jnp.mean(o * o, axis=0, keepdims=True)
    o = o * lax.rsqrt(ms + SUBLN_EPS) * g_ref[...] * (1.0 - lambda_init)
    o_ref[0] = o.T.astype(o_ref.dtype)


def diff_attention(q, k, vt, bias_tiles, lamv, subln_g_col, *, nheads, t, lambda_init):
    b, s, d = q.shape
    hw = 2 * HEAD_DIM
    assert d == nheads * hw and s % t == 0
    kernel = functools.partial(_diff_attn_kernel, t=t, lambda_init=lambda_init)
    stat = lambda: pltpu.VMEM((1, t), F32)
    acc = lambda: pltpu.VMEM((hw, t), F32)
    return pl.pallas_call(
        kernel,
        out_shape=jax.ShapeDtypeStruct((b, s, d), BF16),
        grid=(b, nheads, s // t),
        in_specs=[
            pl.BlockSpec((1, t, hw), lambda bi, h, i: (bi, i, h)),
            pl.BlockSpec((1, s, hw), lambda bi, h, i: (bi, 0, h)),
            pl.BlockSpec((hw, s), lambda bi, h, i: (h, bi)),
            pl.BlockSpec((1, 2, t, t), lambda bi, h, i: (h, 0, 0, 0)),
            pl.BlockSpec((4, HEAD_DIM), lambda bi, h, i: (0, 0)),
            pl.BlockSpec((hw, 1), lambda bi, h, i: (0, 0)),
        ],
        out_specs=pl.BlockSpec((1, t, hw), lambda bi, h, i: (bi, i, h)),
        scratch_shapes=[stat(), stat(), acc(), stat(), stat(), acc(), pltpu.VMEM((2, 2, t, t), F32)],
        compiler_params=_params("parallel", "parallel", "arbitrary"),
        name="diff_attention",
    )(q, k, vt, bias_tiles, lamv, subln_g_col)


def _row_sq_norm_max(x):
    xf = x.astype(F32)
    return jnp.max(jnp.sum(xf * xf, axis=1, keepdims=True), axis=0, keepdims=True)


def _fox_attn_kernel(cend_ref, q_ref, k_ref, vt_ref, kc_ref, cq_ref, o_ref, m_ref, a_ref, z_ref, kn_ref,
                     *, t, nheads):
    hp = pl.program_id(1)
    i = pl.program_id(2)
    nk = pl.num_programs(2)
    heads = range(FOX_HEADS_PER_STEP)
    col = lambda e: slice(e * HEAD_DIM, (e + 1) * HEAD_DIM)
    tile_lane = lax.broadcasted_iota(jnp.int32, (1, kn_ref.shape[-1]), 1)
    for e in heads:
        _init_stats(m_ref.at[e], None, a_ref.at[e])

    @pl.when(i == 0)
    def _():
        def norm_body(j, carry):
            start = pl.multiple_of(j * t, t)
            k = k_ref[0, pl.ds(start, t), :]
            out = []
            for e, (running, kn) in zip(heads, carry):
                running = jnp.maximum(running, _row_sq_norm_max(k[:, col(e)]))
                out.append((running, jnp.where(tile_lane == j, running, kn)))
            return tuple(out)

        init = tuple((jnp.zeros((1, 1), F32), jnp.zeros(tile_lane.shape, F32)) for _ in heads)
        for e, (_, kn) in zip(heads, lax.fori_loop(0, nk, norm_body, init)):
            kn_ref[e] = kn

    cq = cq_ref[0]
    lane = lax.broadcasted_iota(jnp.int32, cq.shape, 1)
    npc = C_PIECES * nheads
    k_diag = k_ref[0, pl.ds(pl.multiple_of(i * t, t), t), :]
    qs, firsts = [], []
    for e in heads:
        h = hp * FOX_HEADS_PER_STEP + e
        q_e = q_ref[0][:, col(e)]
        ct = jnp.sum(jnp.where(lane == h, cq, 0.0), axis=1, keepdims=True) * LOG2E
        qc = jnp.where(lane < npc, jnp.where((lane & (nheads - 1)) == h, 1.0, 0.0), 0.0)
        for p, piece in enumerate(_split_bf16(ct, C_PIECES)):
            qc = jnp.where(lane == npc + p, piece, qc)
        qs.append(jnp.concatenate([q_e, qc.astype(BF16)], axis=1))

        q2 = _row_sq_norm_max(q_e)
        m_lo = jnp.min(jnp.sum(q_e.astype(F32) * k_diag[:, col(e)].astype(F32), axis=1, keepdims=True),
                       axis=0, keepdims=True)
        c_hi = jnp.max(ct, axis=0, keepdims=True)
        r = PRUNE_LOG2 + m_lo - c_hi + cend_ref[0, e:e + 1, :] * LOG2E
        needed = jnp.logical_or(r <= 0.0, q2 * kn_ref[e] >= r * r)
        first = jnp.where(jnp.logical_and(needed, tile_lane < i), tile_lane, i).astype(F32)
        firsts.append(jnp.min(first, axis=1, keepdims=True))
    j_start = functools.reduce(jnp.minimum, firsts)[0, 0].astype(jnp.int32)

    def scores(j, slot):
        start = pl.multiple_of(j * t, t)
        k = k_ref[0, pl.ds(start, t), :]
        kc = kc_ref[0, pl.ds(start, t), :]
        for e in heads:
            z_ref[slot, e] = _dot_nt(jnp.concatenate([k[:, col(e)], kc], axis=1), qs[e])

    def consume(j, diagonal, j_next, slot, next_slot):
        if j_next is not None and next_slot == slot:
            held = [z_ref[slot, e] for e in heads]
            load = lambda e: held[e]
        else:
            load = lambda e: z_ref[slot, e]
        if j_next is not None:
            scores(j_next, next_slot)
        start = pl.multiple_of(j * t, t)
        vt = vt_ref[:, pl.ds(start, t)]
        ones = jnp.ones((ONES_ROWS, t), BF16)
        for e in heads:
            if diagonal:
                def load_z(e=e):
                    ki = lax.broadcasted_iota(jnp.int32, (t, t), 0)
                    qi = lax.broadcasted_iota(jnp.int32, (t, t), 1)
                    return jnp.where(ki <= qi, load(e), NEG)
            else:
                load_z = functools.partial(load, e)
            _online_softmax_step(load_z, jnp.concatenate([vt[col(e)], ones], axis=0),
                                 m_ref.at[e], None, a_ref.at[e])

    scores(i, 0)

    @pl.when(j_start == i)
    def _():
        consume(i, True, None, 0, None)

    @pl.when(j_start < i)
    def _():
        consume(i, True, j_start, 0, 1)
        _far_loop(lambda j, j_next, slot, next_slot: consume(j, False, j_next, slot, next_slot),
                  j_start, i - 1, FOX_FAR_UNROLL, first_slot=1)
        consume(i - 1, False, None, 1, None)

    o = [a_ref[e, :HEAD_DIM] * (1.0 / a_ref[e, HEAD_DIM:HEAD_DIM + 1]) for e in heads]
    o_ref[0] = jnp.concatenate([x.T for x in o], axis=1).astype(o_ref.dtype)


def fox_attention(q, k, vt, kc, c_cols, c_end, *, nheads, t):
    b, s, d = q.shape
    g = FOX_HEADS_PER_STEP
    gw = g * HEAD_DIM
    ng = nheads // g
    assert d == nheads * HEAD_DIM and nheads % g == 0 and s % t == 0 and c_end.shape == (b * ng, g, s // t)
    assert nheads & (nheads - 1) == 0 and C_PIECES * nheads + C_PIECES <= LANES
    return pl.pallas_call(
        functools.partial(_fox_attn_kernel, t=t, nheads=nheads),
        out_shape=jax.ShapeDtypeStruct((b, s, d), BF16),
        grid=(b, nheads // g, s // t),
        in_specs=[
            pl.BlockSpec((1, g, s // t), lambda bi, hp, i: (bi * ng + hp, 0, 0)),
            pl.BlockSpec((1, t, gw), lambda bi, hp, i: (bi, i, hp)),
            pl.BlockSpec((1, s, gw), lambda bi, hp, i: (bi, 0, hp)),
            pl.BlockSpec((gw, s), lambda bi, hp, i: (hp, bi)),
            pl.BlockSpec((1, s, LANES), lambda bi, hp, i: (bi, 0, 0)),
            pl.BlockSpec((1, t, LANES), lambda bi, hp, i: (bi, i, 0)),
        ],
        out_specs=pl.BlockSpec((1, t, gw), lambda bi, hp, i: (bi, i, hp)),
        scratch_shapes=[pltpu.VMEM((g, 1, t), F32), pltpu.VMEM((g, HEAD_DIM + ONES_ROWS, t), F32),
                        pltpu.VMEM((2, g, t, t), F32), pltpu.VMEM((g, 1, s // t), F32)],
        compiler_params=_params("parallel", "parallel", "arbitrary"),
        name="fox_attention",
    )(c_end, q, k, vt, kc, c_cols)


def _forget_scan_kernel(x_ref, g_ref, wt_ref, b_ref, kc_ref, ccol_ref, carry_ref, *, ts, nheads):
    @pl.when(pl.program_id(1) == 0)
    def _():
        carry_ref[...] = jnp.zeros_like(carry_ref)

    x = x_ref[0]
    hn = x * _rms_scale(x, NORM_EPS) * g_ref[...]
    hn_hi, hn_lo = (p.astype(BF16) for p in _split_bf16(hn, 2))
    w_hi, w_lo = (p.astype(BF16) for p in _split_bf16(wt_ref[...], 2))
    f = _dot_nt(w_hi, hn_hi) + _dot_nt(w_hi, hn_lo) + _dot_nt(w_lo, hn_hi) + b_ref[...]
    logf = -(jnp.maximum(-f, 0.0) + jnp.log1p(jnp.exp(-jnp.abs(f))))
    row = lax.broadcasted_iota(jnp.int32, (ts, ts), 0)
    col = lax.broadcasted_iota(jnp.int32, (ts, ts), 1)
    upper = jnp.where(row <= col, 1.0, 0.0).astype(BF16)
    c = carry_ref[:, 0:1]
    for piece in _split_bf16(logf, C_PIECES):
        c = c + _dot(piece.astype(BF16), upper)
    carry_ref[...] = jnp.broadcast_to(c[:, ts - 1:ts], carry_ref.shape)
    ccol_ref[0] = c.T

    hrow = lax.broadcasted_iota(jnp.int32, c.shape, 0)
    npc = C_PIECES * nheads
    aug = jnp.where(hrow < npc + C_PIECES, 1.0, 0.0)
    for p, piece in reversed(list(enumerate(_split_bf16(c * (-LOG2E), C_PIECES)))):
        aug = jnp.where(hrow < (p + 1) * nheads, piece, aug)
    kc_ref[0] = aug.T.astype(BF16)


def forget_scan(h, g, w_f_t, b_f, *, nheads, ts):
    b, s, d = h.shape
    kernel = functools.partial(_forget_scan_kernel, ts=ts, nheads=nheads)
    return pl.pallas_call(
        kernel,
        out_shape=(jax.ShapeDtypeStruct((b, s, LANES), BF16), jax.ShapeDtypeStruct((b, s, LANES), F32)),
        grid=(b, s // ts),
        in_specs=[
            pl.BlockSpec((1, ts, d), lambda bi, i: (bi, i, 0)),
            pl.BlockSpec((1, d), lambda bi, i: (0, 0)),
            pl.BlockSpec((LANES, d), lambda bi, i: (0, 0)),
            pl.BlockSpec((LANES, 1), lambda bi, i: (0, 0)),
        ],
        out_specs=(pl.BlockSpec((1, ts, LANES), lambda bi, i: (bi, i, 0)),
                   pl.BlockSpec((1, ts, LANES), lambda bi, i: (bi, i, 0))),
        scratch_shapes=[pltpu.VMEM((LANES, LANES), F32)],
        compiler_params=_params("parallel", "arbitrary"),
        name="forget_scan",
    )(h, g, w_f_t, b_f)


def _pick(n, target):
    t = min(n, target)
    while n % t:
        t //= 2
    return t


def kernel(x, rel_bias_table, attn_norm_g, mlp_norm_g, w_qkv_a, lam_q1, lam_k1, lam_q2, lam_k2, subln_g,
           w_o_a, kv_norm_g, w_k_b, w_v_b, w_f_b, b_f_b, w_q_b, w_o_b, w_mlp_in, w_mlp_out, final_norm_g):
    b, s, d = x.shape
    n = b * s
    depth = attn_norm_g.shape[0]
    assert depth == 2 and w_qkv_a.shape[0] == 1 and w_q_b.shape[0] == 1
    diff_heads = d // (2 * HEAD_DIM)
    fox_heads = d // HEAD_DIM
    dff = w_mlp_in.shape[2]

    t_attn = _pick(s, 512)
    tm = _pick(n, 1024)
    tf = _pick(dff, 512)
    ts = _pick(s, 512)

    row = lambda v: v.reshape(1, -1).astype(F32)
    x2 = x.reshape(n, d)

    bias_tiles = t5_bias_tiles(rel_bias_table.astype(F32), t=t_attn)
    g0 = row(attn_norm_g[0])
    w_qkv = w_qkv_a[0].astype(BF16)
    w_mlp_in_bf, w_mlp_out_bf = w_mlp_in.astype(BF16), w_mlp_out.astype(BF16)
    q = norm_matmul(x2, g0, w_qkv, tm=tm, out_scale=Q_SCALE, col_block=0, nout=d)
    k = norm_matmul(x2, g0, w_qkv, tm=tm, col_block=1, nout=d)
    vt = norm_matmul(x2, g0, w_qkv[:, 2 * d:].T, tm=tm, transpose_out=True)
    lamv = jnp.stack([lam_q1[0], lam_k1[0], lam_q2[0], lam_k2[0]]).astype(F32)
    lambda_init = 0.8 - 0.6 * math.exp(-0.3 * 0)
    o = diff_attention(q.reshape(b, s, d), k.reshape(b, s, d), vt, bias_tiles, lamv,
                       subln_g[0].reshape(-1, 1).astype(F32), nheads=diff_heads, t=t_attn, lambda_init=lambda_init)
    h = matmul_residual(o.reshape(n, d), w_o_a[0].astype(BF16), x2, tm=tm)
    h = mlp_block(h, row(mlp_norm_g[0]), w_mlp_in_bf, w_mlp_out_bf, row(final_norm_g),
                  layer=0, tm=tm, tf=tf, final_norm=False)

    gkv = row(kv_norm_g)
    w_f_t = jnp.zeros((LANES, d), F32)
    b_f = jnp.zeros((LANES, 1), F32)
    for p in range(C_PIECES):
        w_f_t = w_f_t.at[p * fox_heads:(p + 1) * fox_heads].set(w_f_b.T.astype(F32))
        b_f = b_f.at[p * fox_heads:(p + 1) * fox_heads, 0].set(b_f_b.astype(F32))
    kc, c_cols = forget_scan(h.reshape(b, s, d), gkv, w_f_t, b_f, nheads=fox_heads, ts=ts)
    k = norm_matmul(h, gkv, w_k_b.astype(BF16), tm=tm)
    vt = norm_matmul(h, gkv, w_v_b.astype(BF16).T, tm=tm, transpose_out=True)

    q = norm_matmul(h, row(attn_norm_g[1]), w_q_b[0].astype(BF16), tm=tm, out_scale=Q_SCALE)
    c_end = c_cols[:, t_attn - 1::t_attn, :fox_heads].transpose(0, 2, 1).reshape(-1, FOX_HEADS_PER_STEP, s // t_attn)
    o = fox_attention(q.reshape(b, s, d), k.reshape(b, s, d), vt, kc, c_cols, c_end, nheads=fox_heads, t=t_attn)
    h = matmul_residual(o.reshape(n, d), w_o_b[0].astype(BF16), h, tm=tm)
    h = mlp_block(h, row(mlp_norm_g[1]), w_mlp_in_bf, w_mlp_out_bf, row(final_norm_g),
                  layer=1, tm=tm, tf=tf, final_norm=True)
    return h.reshape(b, s, d)
```

```python
import functools
import math

import jax
import jax.numpy as jnp
from jax import lax
from jax.experimental import pallas as pl
from jax.experimental.pallas import tpu as pltpu

HEAD_DIM = 128
CHUNK = 64
REL_BUCKETS = 32
REL_MAX_DIST = 128
NORM_EPS = 1e-6
SUBLN_EPS = 1e-5
NEG = -1e30
SCALE = HEAD_DIM ** -0.5
LOG2E = math.log2(math.e)
Q_SCALE = SCALE * LOG2E
M_INIT = -1e30
DIFF_FAR_UNROLL = 4
FOX_FAR_UNROLL = 2
ONES_ROWS = 16
PRUNE_LOG2 = -170.0
FOX_HEADS_PER_STEP = 2

V7X_VMEM_BYTES = 64 * 1024 * 1024
V7X_VMEM_RESERVE_BYTES = 8 * 1024 * 1024
V7X_VMEM_LIMIT_BYTES = V7X_VMEM_BYTES - V7X_VMEM_RESERVE_BYTES
LANES = 128
CHUNK_SHIFT = CHUNK.bit_length() - 1
assert CHUNK == 1 << CHUNK_SHIFT
C_PIECES = 3

F32 = jnp.float32
BF16 = jnp.bfloat16


def _params(*semantics):
    return pltpu.CompilerParams(dimension_semantics=semantics, vmem_limit_bytes=V7X_VMEM_LIMIT_BYTES)


def _rms_scale(x, eps):
    return lax.rsqrt(jnp.mean(x * x, axis=-1, keepdims=True) + eps)


def _dot(a, b):
    return jnp.dot(a, b, preferred_element_type=F32)


def _dot_nt(a, b):
    return lax.dot_general(a, b, (((1,), (1,)), ((), ())), preferred_element_type=F32)


def _split_bf16(x, terms):
    pieces = []
    for _ in range(terms - 1):
        p = x.astype(BF16).astype(x.dtype)
        pieces.append(p)
        x = x - p
    pieces.append(x.astype(BF16).astype(x.dtype))
    return pieces


def _norm_matmul_kernel(x_ref, g_ref, w_ref, o_ref, *, transpose_out, out_scale):
    x = x_ref[...]
    hn = (x * _rms_scale(x, NORM_EPS) * g_ref[...]).astype(BF16)
    if transpose_out:
        y = _dot_nt(w_ref[...], hn)
    else:
        y = _dot(hn, w_ref[...])
    if out_scale != 1.0:
        y = y * out_scale
    o_ref[...] = y.astype(o_ref.dtype)


def norm_matmul(x, g, w, *, tm, transpose_out=False, out_scale=1.0, col_block=0, nout=None):
    n, d = x.shape
    if transpose_out:
        assert col_block == 0 and nout is None
        nout = w.shape[0]
        w_spec = pl.BlockSpec((nout, d), lambda i: (0, 0))
        out_shape = jax.ShapeDtypeStruct((nout, n), BF16)
        out_spec = pl.BlockSpec((nout, tm), lambda i: (0, i))
    else:
        nout = w.shape[1] if nout is None else nout
        w_spec = pl.BlockSpec((d, nout), lambda i: (0, col_block))
        out_shape = jax.ShapeDtypeStruct((n, nout), BF16)
        out_spec = pl.BlockSpec((tm, nout), lambda i: (i, 0))
    return pl.pallas_call(
        functools.partial(_norm_matmul_kernel, transpose_out=transpose_out, out_scale=out_scale),
        out_shape=out_shape,
        grid=(n // tm,),
        in_specs=[
            pl.BlockSpec((tm, d), lambda i: (i, 0)),
            pl.BlockSpec((1, d), lambda i: (0, 0)),
            w_spec,
        ],
        out_specs=out_spec,
        compiler_params=_params("parallel"),
        name="norm_matmul_t" if transpose_out else "norm_matmul",
    )(x, g, w)


def _matmul_residual_kernel(a_ref, w_ref, r_ref, o_ref):
    o_ref[...] = r_ref[...] + _dot(a_ref[...], w_ref[...])


def matmul_residual(a, w, res, *, tm):
    n, k = a.shape
    nout = w.shape[1]
    return pl.pallas_call(
        _matmul_residual_kernel,
        out_shape=jax.ShapeDtypeStruct((n, nout), F32),
        grid=(n // tm,),
        in_specs=[
            pl.BlockSpec((tm, k), lambda i: (i, 0)),
            pl.BlockSpec((k, nout), lambda i: (0, 0)),
            pl.BlockSpec((tm, nout), lambda i: (i, 0)),
        ],
        out_specs=pl.BlockSpec((tm, nout), lambda i: (i, 0)),
        compiler_params=_params("parallel"),
        name="matmul_residual",
    )(a, w, res)


def _mlp_kernel(x_ref, g_ref, win_ref, wout_ref, gf_ref, o_ref, hn_ref, *, final_norm):
    f = pl.program_id(1)

    @pl.when(f == 0)
    def _():
        x = x_ref[...]
        hn_ref[...] = (x * _rms_scale(x, NORM_EPS) * g_ref[...]).astype(hn_ref.dtype)
        o_ref[...] = x

    u = _dot(hn_ref[...], win_ref[...])
    u = jnp.square(jnp.maximum(u, 0.0)).astype(BF16)
    o_ref[...] += _dot(u, wout_ref[...])

    if final_norm:
        @pl.when(f == pl.num_programs(1) - 1)
        def _():
            y = o_ref[...]
            o_ref[...] = y * _rms_scale(y, NORM_EPS) * gf_ref[...]


def mlp_block(x, g, w_in, w_out, g_final, *, layer, tm, tf, final_norm):
    n, d = x.shape
    dff = w_in.shape[2]
    return pl.pallas_call(
        functools.partial(_mlp_kernel, final_norm=final_norm),
        out_shape=jax.ShapeDtypeStruct((n, d), F32),
        grid=(n // tm, dff // tf),
        in_specs=[
            pl.BlockSpec((tm, d), lambda i, f: (i, 0)),
            pl.BlockSpec((1, d), lambda i, f: (0, 0)),
            pl.BlockSpec((None, d, tf), lambda i, f: (layer, 0, f)),
            pl.BlockSpec((None, tf, d), lambda i, f: (layer, f, 0)),
            pl.BlockSpec((1, d), lambda i, f: (0, 0)),
        ],
        out_specs=pl.BlockSpec((tm, d), lambda i, f: (i, 0)),
        scratch_shapes=[pltpu.VMEM((tm, d), BF16)],
        compiler_params=_params("parallel", "arbitrary"),
        name="mlp_block",
    )(x, g, w_in, w_out, g_final)


def _t5_bias_kernel(table_ref, o_ref, *, t):
    h = pl.program_id(0)
    half = REL_BUCKETS // 2
    max_exact = half // 2
    ki = lax.broadcasted_iota(jnp.int32, (t, t), 0)
    qi = lax.broadcasted_iota(jnp.int32, (t, t), 1)
    far = table_ref[half - 1, h]
    for d in range(2):
        rel = ki - d * t - qi
        ret = jnp.where(rel > 0, half, 0)
        n = jnp.abs(rel)
        nf = jnp.maximum(n, 1).astype(F32)
        large = max_exact + (jnp.log(nf / max_exact) / math.log(REL_MAX_DIST / max_exact)
                             * (half - max_exact)).astype(jnp.int32)
        large = jnp.minimum(large, half - 1)
        bucket = ret + jnp.where(n < max_exact, n, large)
        val = jnp.zeros((t, t), F32)
        for b in range(REL_BUCKETS):
            val = jnp.where(bucket == b, table_ref[b, h], val)
        val = (val - far) * LOG2E
        if d == 0:
            mask = (ki >> CHUNK_SHIFT) <= (qi >> CHUNK_SHIFT)
            val = jnp.where(mask, val, NEG)
        o_ref[0, d] = val


def t5_bias_tiles(rel_table, *, t):
    nheads = rel_table.shape[1]
    assert t % CHUNK == 0 and t >= REL_MAX_DIST
    return pl.pallas_call(
        functools.partial(_t5_bias_kernel, t=t),
        out_shape=jax.ShapeDtypeStruct((nheads, 2, t, t), F32),
        grid=(nheads,),
        in_specs=[pl.BlockSpec(memory_space=pltpu.SMEM)],
        out_specs=pl.BlockSpec((1, 2, t, t), lambda h: (h, 0, 0, 0)),
        compiler_params=_params("arbitrary"),
        name="t5_bias_tiles",
    )(rel_table)


def _init_stats(m_ref, l_ref, acc_ref):
    m_ref[...] = jnp.full_like(m_ref, M_INIT)
    if l_ref is not None:
        l_ref[...] = jnp.zeros_like(l_ref)
    acc_ref[...] = jnp.zeros_like(acc_ref)


def _online_softmax_step(load_z, vt, m_ref, l_ref, acc_ref):
    m_prev = m_ref[...]
    m_new = jnp.maximum(m_prev, jnp.max(load_z(), axis=0, keepdims=True))
    p = jnp.exp2(load_z() - m_new)
    alpha = jnp.exp2(m_prev - m_new)
    if l_ref is not None:
        l_ref[...] = alpha * l_ref[...] + jnp.sum(p, axis=0, keepdims=True)
    acc_ref[...] = alpha * acc_ref[...] + _dot(vt, p.astype(BF16))
    m_ref[...] = m_new


def _far_loop(consume, j_start, j_end, unroll, first_slot=0):
    assert unroll % 2 == 0

    def body(u):
        def run(jj, carry):
            for r in range(u):
                slot = (first_slot + r) % 2 if u > 1 else first_slot
                next_slot = (first_slot + r + 1) % 2 if u > 1 else first_slot
                consume(j_start + jj * u + r, j_start + jj * u + r + 1, slot, next_slot)
            return carry
        return run

    n_far = j_end - j_start
    n_main = n_far // unroll
    lax.fori_loop(0, n_main, body(unroll), 0)
    lax.fori_loop(n_main * unroll, n_far, body(1), 0)


def _diff_attn_kernel(q_ref, qn_ref, k_ref, vt_ref, bias_ref, lamv_ref, g_ref, o_ref,
                      m0, l0, a0, m1, l1, a1, z_ref, *, t, lambda_init):
    i = pl.program_id(2)
    stats = ((m0, l0, a0), (m1, l1, a1))
    for s in stats:
        _init_stats(*s)

    def scores(j, slot, q=None):
        q = q_ref[0] if q is None else q
        start = pl.multiple_of(j * t, t)
        k = k_ref[0, pl.ds(start, t), :]
        for c in range(2):
            half = slice(c * HEAD_DIM, (c + 1) * HEAD_DIM)
            z_ref[slot, c] = _dot_nt(k[:, half], q[:, half])

    def consume(j, bias_idx, j_next, slot, next_slot):
        if j_next is not None and next_slot == slot:
            held = [z_ref[slot, c] for c in range(2)]
            load = lambda c: held[c]
        else:
            load = lambda c: z_ref[slot, c]
        if j_next is not None:
            scores(j_next, next_slot)
        start = pl.multiple_of(j * t, t)
        vt = vt_ref[:, pl.ds(start, t)]
        for c, s in enumerate(stats):
            if bias_idx is None:
                load_z = functools.partial(load, c)
            else:
                load_z = lambda c=c: load(c) + bias_ref[0, bias_idx]
            _online_softmax_step(load_z, vt, *s)

    @pl.when(i == 0)
    def _():
        scores(0, 0)

    _far_loop(lambda j, j_next, slot, next_slot: consume(j, None, j_next, slot, next_slot),
              0, jnp.maximum(i - 1, 0), DIFF_FAR_UNROLL)

    @pl.when(i >= 1)
    def _():
        consume(i - 1, 1, i, 0, 1)
        consume(i, 0, None, 1, None)

    @pl.when(i == 0)
    def _():
        consume(i, 0, None, 0, None)

    lamv = lamv_ref[...]
    lam = (jnp.exp(jnp.sum(lamv[0:1] * lamv[1:2], axis=1, keepdims=True))
           - jnp.exp(jnp.sum(lamv[2:3] * lamv[3:4], axis=1, keepdims=True)) + lambda_init)
    o = a0[...] * (1.0 / l0[...]) - lam * (a1[...] * (1.0 / l1[...]))
    ms = jnp.mean(o * o, axis=0, keepdims=True)
    o = o * lax.rsqrt(ms + SUBLN_EPS) * g_ref[...] * (1.0 - lambda_init)
    o_ref[0] = o.T.astype(o_ref.dtype)

    @pl.when(i + 1 < pl.num_programs(2))
    def _():
        scores(0, 0, qn_ref[0])


def diff_attention(q, k, vt, bias_tiles, lamv, subln_g_col, *, nheads, t, lambda_init):
    b, s, d = q.shape
    hw = 2 * HEAD_DIM
    assert d == nheads * hw and s % t == 0
    kernel = functools.partial(_diff_attn_kernel, t=t, lambda_init=lambda_init)
    stat = lambda: pltpu.VMEM((1, t), F32)
    acc = lambda: pltpu.VMEM((hw, t), F32)
    return pl.pallas_call(
        kernel,
        out_shape=jax.ShapeDtypeStruct((b, s, d), BF16),
        grid=(b, nheads, s // t),
        in_specs=[
            pl.BlockSpec((1, t, hw), lambda bi, h, i: (bi, i, h)),
            pl.BlockSpec((1, t, hw), lambda bi, h, i: (bi, jnp.minimum(i + 1, s // t - 1), h)),
            pl.BlockSpec((1, s, hw), lambda bi, h, i: (bi, 0, h)),
            pl.BlockSpec((hw, s), lambda bi, h, i: (h, bi)),
            pl.BlockSpec((1, 2, t, t), lambda bi, h, i: (h, 0, 0, 0)),
            pl.BlockSpec((4, HEAD_DIM), lambda bi, h, i: (0, 0)),
            pl.BlockSpec((hw, 1), lambda bi, h, i: (0, 0)),
        ],
        out_specs=pl.BlockSpec((1, t, hw), lambda bi, h, i: (bi, i, h)),
        scratch_shapes=[stat(), stat(), acc(), stat(), stat(), acc(), pltpu.VMEM((2, 2, t, t), F32)],
        compiler_params=_params("parallel", "parallel", "arbitrary"),
        name="diff_attention",
    )(q, q, k, vt, bias_tiles, lamv, subln_g_col)


def _row_sq_norm_max(x):
    xf = x.astype(F32)
    return jnp.max(jnp.sum(xf * xf, axis=1, keepdims=True), axis=0, keepdims=True)


def _fox_attn_kernel(cend_ref, q_ref, k_ref, vt_ref, kc_ref, cq_ref, o_ref, m_ref, a_ref, z_ref, kn_ref,
                     *, t, nheads):
    hp = pl.program_id(1)
    i = pl.program_id(2)
    nk = pl.num_programs(2)
    heads = range(FOX_HEADS_PER_STEP)
    col = lambda e: slice(e * HEAD_DIM, (e + 1) * HEAD_DIM)
    tile_lane = lax.broadcasted_iota(jnp.int32, (1, kn_ref.shape[-1]), 1)
    for e in heads:
        _init_stats(m_ref.at[e], None, a_ref.at[e])

    @pl.when(i == 0)
    def _():
        def norm_body(j, carry):
            start = pl.multiple_of(j * t, t)
            k = k_ref[0, pl.ds(start, t), :]
            out = []
            for e, (running, kn) in zip(heads, carry):
                running = jnp.maximum(running, _row_sq_norm_max(k[:, col(e)]))
                out.append((running, jnp.where(tile_lane == j, running, kn)))
            return tuple(out)

        init = tuple((jnp.zeros((1, 1), F32), jnp.zeros(tile_lane.shape, F32)) for _ in heads)
        for e, (_, kn) in zip(heads, lax.fori_loop(0, nk, norm_body, init)):
            kn_ref[e] = kn

    cq = cq_ref[0]
    lane = lax.broadcasted_iota(jnp.int32, cq.shape, 1)
    npc = C_PIECES * nheads
    k_diag = k_ref[0, pl.ds(pl.multiple_of(i * t, t), t), :]
    qs, firsts = [], []
    for e in heads:
        h = hp * FOX_HEADS_PER_STEP + e
        q_e = q_ref[0][:, col(e)]
        ct = jnp.sum(jnp.where(lane == h, cq, 0.0), axis=1, keepdims=True) * LOG2E
        qc = jnp.where(lane < npc, jnp.where((lane & (nheads - 1)) == h, 1.0, 0.0), 0.0)
        for p, piece in enumerate(_split_bf16(ct, C_PIECES)):
            qc = jnp.where(lane == npc + p, piece, qc)
        qs.append(jnp.concatenate([q_e, qc.astype(BF16)], axis=1))

        q2 = _row_sq_norm_max(q_e)
        m_lo = jnp.min(jnp.sum(q_e.astype(F32) * k_diag[:, col(e)].astype(F32), axis=1, keepdims=True),
                       axis=0, keepdims=True)
        c_hi = jnp.max(ct, axis=0, keepdims=True)
        r = PRUNE_LOG2 + m_lo - c_hi + cend_ref[0, e:e + 1, :] * LOG2E
        needed = jnp.logical_or(r <= 0.0, q2 * kn_ref[e] >= r * r)
        first = jnp.where(jnp.logical_and(needed, tile_lane < i), tile_lane, i).astype(F32)
        firsts.append(jnp.min(first, axis=1, keepdims=True))
    j_start = functools.reduce(jnp.minimum, firsts)[0, 0].astype(jnp.int32)

    def scores(j, slot):
        start = pl.multiple_of(j * t, t)
        k = k_ref[0, pl.ds(start, t), :]
        kc = kc_ref[0, pl.ds(start, t), :]
        for e in heads:
            z_ref[slot, e] = _dot_nt(jnp.concatenate([k[:, col(e)], kc], axis=1), qs[e])

    def consume(j, diagonal, j_next, slot, next_slot):
        if j_next is not None and next_slot == slot:
            held = [z_ref[slot, e] for e in heads]
            load = lambda e: held[e]
        else:
            load = lambda e: z_ref[slot, e]
        if j_next is not None:
            scores(j_next, next_slot)
        start = pl.multiple_of(j * t, t)
        vt = vt_ref[:, pl.ds(start, t)]
        ones = jnp.ones((ONES_ROWS, t), BF16)
        for e in heads:
            if diagonal:
                def load_z(e=e):
                    ki = lax.broadcasted_iota(jnp.int32, (t, t), 0)
                    qi = lax.broadcasted_iota(jnp.int32, (t, t), 1)
                    return jnp.where(ki <= qi, load(e), NEG)
            else:
                load_z = functools.partial(load, e)
            _online_softmax_step(load_z, jnp.concatenate([vt[col(e)], ones], axis=0),
                                 m_ref.at[e], None, a_ref.at[e])

    scores(i, 0)

    @pl.when(j_start == i)
    def _():
        consume(i, True, None, 0, None)

    @pl.when(j_start < i)
    def _():
        consume(i, True, j_start, 0, 1)
        _far_loop(lambda j, j_next, slot, next_slot: consume(j, False, j_next, slot, next_slot),
                  j_start, i - 1, FOX_FAR_UNROLL, first_slot=1)
        consume(i - 1, False, None, 1, None)

    o = [a_ref[e, :HEAD_DIM] * (1.0 / a_ref[e, HEAD_DIM:HEAD_DIM + 1]) for e in heads]
    o_ref[0] = jnp.concatenate([x.T for x in o], axis=1).astype(o_ref.dtype)


def fox_attention(q, k, vt, kc, c_cols, c_end, *, nheads, t):
    b, s, d = q.shape
    g = FOX_HEADS_PER_STEP
    gw = g * HEAD_DIM
    ng = nheads // g
    assert d == nheads * HEAD_DIM and nheads % g == 0 and s % t == 0 and c_end.shape == (b * ng, g, s // t)
    assert nheads & (nheads - 1) == 0 and C_PIECES * nheads + C_PIECES <= LANES
    return pl.pallas_call(
        functools.partial(_fox_attn_kernel, t=t, nheads=nheads),
        out_shape=jax.ShapeDtypeStruct((b, s, d), BF16),
        grid=(b, nheads // g, s // t),
        in_specs=[
            pl.BlockSpec((1, g, s // t), lambda bi, hp, i: (bi * ng + hp, 0, 0)),
            pl.BlockSpec((1, t, gw), lambda bi, hp, i: (bi, i, hp)),
            pl.BlockSpec((1, s, gw), lambda bi, hp, i: (bi, 0, hp)),
            pl.BlockSpec((gw, s), lambda bi, hp, i: (hp, bi)),
            pl.BlockSpec((1, s, LANES), lambda bi, hp, i: (bi, 0, 0)),
            pl.BlockSpec((1, t, LANES), lambda bi, hp, i: (bi, i, 0)),
        ],
        out_specs=pl.BlockSpec((1, t, gw), lambda bi, hp, i: (bi, i, hp)),
        scratch_shapes=[pltpu.VMEM((g, 1, t), F32), pltpu.VMEM((g, HEAD_DIM + ONES_ROWS, t), F32),
                        pltpu.VMEM((2, g, t, t), F32), pltpu.VMEM((g, 1, s // t), F32)],
        compiler_params=_params("parallel", "parallel", "arbitrary"),
        name="fox_attention",
    )(c_end, q, k, vt, kc, c_cols)


def _forget_scan_kernel(x_ref, g_ref, wt_ref, b_ref, kc_ref, ccol_ref, carry_ref, *, ts, nheads):
    @pl.when(pl.program_id(1) == 0)
    def _():
        carry_ref[...] = jnp.zeros_like(carry_ref)

    x = x_ref[0]
    hn = x * _rms_scale(x, NORM_EPS) * g_ref[...]
    hn_hi, hn_lo = (p.astype(BF16) for p in _split_bf16(hn, 2))
    w_hi, w_lo = (p.astype(BF16) for p in _split_bf16(wt_ref[...], 2))
    f = _dot_nt(w_hi, hn_hi) + _dot_nt(w_hi, hn_lo) + _dot_nt(w_lo, hn_hi) + b_ref[...]
    logf = -(jnp.maximum(-f, 0.0) + jnp.log1p(jnp.exp(-jnp.abs(f))))
    row = lax.broadcasted_iota(jnp.int32, (ts, ts), 0)
    col = lax.broadcasted_iota(jnp.int32, (ts, ts), 1)
    upper = jnp.where(row <= col, 1.0, 0.0).astype(BF16)
    c = carry_ref[:, 0:1]
    for piece in _split_bf16(logf, C_PIECES):
        c = c + _dot(piece.astype(BF16), upper)
    carry_ref[...] = jnp.broadcast_to(c[:, ts - 1:ts], carry_ref.shape)
    ccol_ref[0] = c.T

    hrow = lax.broadcasted_iota(jnp.int32, c.shape, 0)
    npc = C_PIECES * nheads
    aug = jnp.where(hrow < npc + C_PIECES, 1.0, 0.0)
    for p, piece in reversed(list(enumerate(_split_bf16(c * (-LOG2E), C_PIECES)))):
        aug = jnp.where(hrow < (p + 1) * nheads, piece, aug)
    kc_ref[0] = aug.T.astype(BF16)


def forget_scan(h, g, w_f_t, b_f, *, nheads, ts):
    b, s, d = h.shape
    kernel = functools.partial(_forget_scan_kernel, ts=ts, nheads=nheads)
    return pl.pallas_call(
        kernel,
        out_shape=(jax.ShapeDtypeStruct((b, s, LANES), BF16), jax.ShapeDtypeStruct((b, s, LANES), F32)),
        grid=(b, s // ts),
        in_specs=[
            pl.BlockSpec((1, ts, d), lambda bi, i: (bi, i, 0)),
            pl.BlockSpec((1, d), lambda bi, i: (0, 0)),
            pl.BlockSpec((LANES, d), lambda bi, i: (0, 0)),
            pl.BlockSpec((LANES, 1), lambda bi, i: (0, 0)),
        ],
        out_specs=(pl.BlockSpec((1, ts, LANES), lambda bi, i: (bi, i, 0)),
                   pl.BlockSpec((1, ts, LANES), lambda bi, i: (bi, i, 0))),
        scratch_shapes=[pltpu.VMEM((LANES, LANES), F32)],
        compiler_params=_params("parallel", "arbitrary"),
        name="forget_scan",
    )(h, g, w_f_t, b_f)


def _pick(n, target):
    t = min(n, target)
    while n % t:
        t //= 2
    return t


def kernel(x, rel_bias_table, attn_norm_g, mlp_norm_g, w_qkv_a, lam_q1, lam_k1, lam_q2, lam_k2, subln_g,
           w_o_a, kv_norm_g, w_k_b, w_v_b, w_f_b, b_f_b, w_q_b, w_o_b, w_mlp_in, w_mlp_out, final_norm_g):
    b, s, d = x.shape
    n = b * s
    depth = attn_norm_g.shape[0]
    assert depth == 2 and w_qkv_a.shape[0] == 1 and w_q_b.shape[0] == 1
    diff_heads = d // (2 * HEAD_DIM)
    fox_heads = d // HEAD_DIM
    dff = w_mlp_in.shape[2]

    t_attn = _pick(s, 512)
    tm = _pick(n, 1024)
    tf = _pick(dff, 512)
    ts = _pick(s, 512)

    row = lambda v: v.reshape(1, -1).astype(F32)
    x2 = x.reshape(n, d)

    bias_tiles = t5_bias_tiles(rel_bias_table.astype(F32), t=t_attn)
    g0 = row(attn_norm_g[0])
    w_qkv = w_qkv_a[0].astype(BF16)
    w_mlp_in_bf, w_mlp_out_bf = w_mlp_in.astype(BF16), w_mlp_out.astype(BF16)
    q = norm_matmul(x2, g0, w_qkv, tm=tm, out_scale=Q_SCALE, col_block=0, nout=d)
    k = norm_matmul(x2, g0, w_qkv, tm=tm, col_block=1, nout=d)
    vt = norm_matmul(x2, g0, w_qkv[:, 2 * d:].T, tm=tm, transpose_out=True)
    lamv = jnp.stack([lam_q1[0], lam_k1[0], lam_q2[0], lam_k2[0]]).astype(F32)
    lambda_init = 0.8 - 0.6 * math.exp(-0.3 * 0)
    o = diff_attention(q.reshape(b, s, d), k.reshape(b, s, d), vt, bias_tiles, lamv,
                       subln_g[0].reshape(-1, 1).astype(F32), nheads=diff_heads, t=t_attn, lambda_init=lambda_init)
    h = matmul_residual(o.reshape(n, d), w_o_a[0].astype(BF16), x2, tm=tm)
    h = mlp_block(h, row(mlp_norm_g[0]), w_mlp_in_bf, w_mlp_out_bf, row(final_norm_g),
                  layer=0, tm=tm, tf=tf, final_norm=False)

    gkv = row(kv_norm_g)
    w_f_t = jnp.zeros((LANES, d), F32)
    b_f = jnp.zeros((LANES, 1), F32)
    for p in range(C_PIECES):
        w_f_t = w_f_t.at[p * fox_heads:(p + 1) * fox_heads].set(w_f_b.T.astype(F32))
        b_f = b_f.at[p * fox_heads:(p + 1) * fox_heads, 0].set(b_f_b.astype(F32))
    kc, c_cols = forget_scan(h.reshape(b, s, d), gkv, w_f_t, b_f, nheads=fox_heads, ts=ts)
    k = norm_matmul(h, gkv, w_k_b.astype(BF16), tm=tm)
    vt = norm_matmul(h, gkv, w_v_b.astype(BF16).T, tm=tm, transpose_out=True)

    q = norm_matmul(h, row(attn_norm_g[1]), w_q_b[0].astype(BF16), tm=tm, out_scale=Q_SCALE)
    c_end = c_cols[:, t_attn - 1::t_attn, :fox_heads].transpose(0, 2, 1).reshape(-1, FOX_HEADS_PER_STEP, s // t_attn)
    o = fox_attention(q.reshape(b, s, d), k.reshape(b, s, d), vt, kc, c_cols, c_end, nheads=fox_heads, t=t_attn)
    h = matmul_residual(o.reshape(n, d), w_o_b[0].astype(BF16), h, tm=tm)
    h = mlp_block(h, row(mlp_norm_g[1]), w_mlp_in_bf, w_mlp_out_bf, row(final_norm_g),
                  layer=1, tm=tm, tf=tf, final_norm=True)
    return h.reshape(b, s, d)
```

```python
import functools
import math

import jax
import jax.numpy as jnp
from jax import lax
from jax.experimental import pallas as pl
from jax.experimental.pallas import tpu as pltpu

HEAD_DIM = 128
CHUNK = 64
REL_BUCKETS = 32
REL_MAX_DIST = 128
NORM_EPS = 1e-6
SUBLN_EPS = 1e-5
NEG = -1e30
SCALE = HEAD_DIM ** -0.5
LOG2E = math.log2(math.e)
Q_SCALE = SCALE * LOG2E
M_INIT = -1e30
DIFF_FAR_UNROLLS = (8, 4)
FOX_FAR_UNROLLS = (2,)
ONES_ROWS = 16
PRUNE_LOG2 = -170.0
FOX_HEADS_PER_STEP = 2

V7X_VMEM_BYTES = 64 * 1024 * 1024
V7X_VMEM_RESERVE_BYTES = 8 * 1024 * 1024
V7X_VMEM_LIMIT_BYTES = V7X_VMEM_BYTES - V7X_VMEM_RESERVE_BYTES
LANES = 128
CHUNK_SHIFT = CHUNK.bit_length() - 1
assert CHUNK == 1 << CHUNK_SHIFT
C_PIECES = 3

F32 = jnp.float32
BF16 = jnp.bfloat16


def _params(*semantics):
    return pltpu.CompilerParams(dimension_semantics=semantics, vmem_limit_bytes=V7X_VMEM_LIMIT_BYTES)


def _rms_scale(x, eps):
    return lax.rsqrt(jnp.mean(x * x, axis=-1, keepdims=True) + eps)


def _dot(a, b):
    return jnp.dot(a, b, preferred_element_type=F32)


def _dot_nt(a, b):
    return lax.dot_general(a, b, (((1,), (1,)), ((), ())), preferred_element_type=F32)


def _split_bf16(x, terms):
    pieces = []
    for _ in range(terms - 1):
        p = x.astype(BF16).astype(x.dtype)
        pieces.append(p)
        x = x - p
    pieces.append(x.astype(BF16).astype(x.dtype))
    return pieces


def _norm_matmul_kernel(x_ref, g_ref, w_ref, o_ref, *, transpose_out, out_scale):
    x = x_ref[...]
    hn = (x * _rms_scale(x, NORM_EPS) * g_ref[...]).astype(BF16)
    if transpose_out:
        y = _dot_nt(w_ref[...], hn)
    else:
        y = _dot(hn, w_ref[...])
    if out_scale != 1.0:
        y = y * out_scale
    o_ref[...] = y.astype(o_ref.dtype)


def norm_matmul(x, g, w, *, tm, transpose_out=False, out_scale=1.0, col_block=0, nout=None):
    n, d = x.shape
    if transpose_out:
        assert col_block == 0 and nout is None
        nout = w.shape[0]
        w_spec = pl.BlockSpec((nout, d), lambda i: (0, 0))
        out_shape = jax.ShapeDtypeStruct((nout, n), BF16)
        out_spec = pl.BlockSpec((nout, tm), lambda i: (0, i))
    else:
        nout = w.shape[1] if nout is None else nout
        w_spec = pl.BlockSpec((d, nout), lambda i: (0, col_block))
        out_shape = jax.ShapeDtypeStruct((n, nout), BF16)
        out_spec = pl.BlockSpec((tm, nout), lambda i: (i, 0))
    return pl.pallas_call(
        functools.partial(_norm_matmul_kernel, transpose_out=transpose_out, out_scale=out_scale),
        out_shape=out_shape,
        grid=(n // tm,),
        in_specs=[
            pl.BlockSpec((tm, d), lambda i: (i, 0)),
            pl.BlockSpec((1, d), lambda i: (0, 0)),
            w_spec,
        ],
        out_specs=out_spec,
        compiler_params=_params("parallel"),
        name="norm_matmul_t" if transpose_out else "norm_matmul",
    )(x, g, w)


def _matmul_residual_kernel(a_ref, w_ref, r_ref, o_ref):
    o_ref[...] = r_ref[...] + _dot(a_ref[...], w_ref[...])


def matmul_residual(a, w, res, *, tm):
    n, k = a.shape
    nout = w.shape[1]
    return pl.pallas_call(
        _matmul_residual_kernel,
        out_shape=jax.ShapeDtypeStruct((n, nout), F32),
        grid=(n // tm,),
        in_specs=[
            pl.BlockSpec((tm, k), lambda i: (i, 0)),
            pl.BlockSpec((k, nout), lambda i: (0, 0)),
            pl.BlockSpec((tm, nout), lambda i: (i, 0)),
        ],
        out_specs=pl.BlockSpec((tm, nout), lambda i: (i, 0)),
        compiler_params=_params("parallel"),
        name="matmul_residual",
    )(a, w, res)


def _mlp_kernel(x_ref, g_ref, win_ref, wout_ref, gf_ref, o_ref, hn_ref, *, final_norm):
    f = pl.program_id(1)

    @pl.when(f == 0)
    def _():
        x = x_ref[...]
        hn_ref[...] = (x * _rms_scale(x, NORM_EPS) * g_ref[...]).astype(hn_ref.dtype)
        o_ref[...] = x

    u = _dot(hn_ref[...], win_ref[...])
    u = jnp.square(jnp.maximum(u, 0.0)).astype(BF16)
    o_ref[...] += _dot(u, wout_ref[...])

    if final_norm:
        @pl.when(f == pl.num_programs(1) - 1)
        def _():
            y = o_ref[...]
            o_ref[...] = y * _rms_scale(y, NORM_EPS) * gf_ref[...]


def mlp_block(x, g, w_in, w_out, g_final, *, layer, tm, tf, final_norm):
    n, d = x.shape
    dff = w_in.shape[2]
    return pl.pallas_call(
        functools.partial(_mlp_kernel, final_norm=final_norm),
        out_shape=jax.ShapeDtypeStruct((n, d), F32),
        grid=(n // tm, dff // tf),
        in_specs=[
            pl.BlockSpec((tm, d), lambda i, f: (i, 0)),
            pl.BlockSpec((1, d), lambda i, f: (0, 0)),
            pl.BlockSpec((None, d, tf), lambda i, f: (layer, 0, f)),
            pl.BlockSpec((None, tf, d), lambda i, f: (layer, f, 0)),
            pl.BlockSpec((1, d), lambda i, f: (0, 0)),
        ],
        out_specs=pl.BlockSpec((tm, d), lambda i, f: (i, 0)),
        scratch_shapes=[pltpu.VMEM((tm, d), BF16)],
        compiler_params=_params("parallel", "arbitrary"),
        name="mlp_block",
    )(x, g, w_in, w_out, g_final)


def _t5_bias_kernel(table_ref, o_ref, *, t):
    h = pl.program_id(0)
    half = REL_BUCKETS // 2
    max_exact = half // 2
    ki = lax.broadcasted_iota(jnp.int32, (t, t), 0)
    qi = lax.broadcasted_iota(jnp.int32, (t, t), 1)
    far = table_ref[half - 1, h]
    for d in range(2):
        rel = ki - d * t - qi
        ret = jnp.where(rel > 0, half, 0)
        n = jnp.abs(rel)
        nf = jnp.maximum(n, 1).astype(F32)
        large = max_exact + (jnp.log(nf / max_exact) / math.log(REL_MAX_DIST / max_exact)
                             * (half - max_exact)).astype(jnp.int32)
        large = jnp.minimum(large, half - 1)
        bucket = ret + jnp.where(n < max_exact, n, large)
        val = jnp.zeros((t, t), F32)
        for b in range(REL_BUCKETS):
            val = jnp.where(bucket == b, table_ref[b, h], val)
        val = (val - far) * LOG2E
        if d == 0:
            mask = (ki >> CHUNK_SHIFT) <= (qi >> CHUNK_SHIFT)
            val = jnp.where(mask, val, NEG)
        o_ref[0, d] = val


def t5_bias_tiles(rel_table, *, t):
    nheads = rel_table.shape[1]
    assert t % CHUNK == 0 and t >= REL_MAX_DIST
    return pl.pallas_call(
        functools.partial(_t5_bias_kernel, t=t),
        out_shape=jax.ShapeDtypeStruct((nheads, 2, t, t), F32),
        grid=(nheads,),
        in_specs=[pl.BlockSpec(memory_space=pltpu.SMEM)],
        out_specs=pl.BlockSpec((1, 2, t, t), lambda h: (h, 0, 0, 0)),
        compiler_params=_params("arbitrary"),
        name="t5_bias_tiles",
    )(rel_table)


def _init_stats(m_ref, l_ref, acc_ref):
    m_ref[...] = jnp.full_like(m_ref, M_INIT)
    if l_ref is not None:
        l_ref[...] = jnp.zeros_like(l_ref)
    acc_ref[...] = jnp.zeros_like(acc_ref)


def _online_softmax_step(load_z, vt, m_ref, l_ref, acc_ref):
    m_prev = m_ref[...]
    m_new = jnp.maximum(m_prev, jnp.max(load_z(), axis=0, keepdims=True))
    p = jnp.exp2(load_z() - m_new)
    alpha = jnp.exp2(m_prev - m_new)
    if l_ref is not None:
        l_ref[...] = alpha * l_ref[...] + jnp.sum(p, axis=0, keepdims=True)
    acc_ref[...] = alpha * acc_ref[...] + _dot(vt, p.astype(BF16))
    m_ref[...] = m_new


def _far_loop(consume, j_start, j_end, unrolls, first_slot=0):
    def body(u, base):
        def run(jj, carry):
            for r in range(u):
                slot = (first_slot + r) % 2 if u > 1 else first_slot
                next_slot = (first_slot + r + 1) % 2 if u > 1 else first_slot
                consume(base + jj * u + r, base + jj * u + r + 1, slot, next_slot)
            return carry
        return run

    base = j_start
    for u in tuple(unrolls) + (1,):
        assert u == 1 or u % 2 == 0
        groups = (j_end - base) // u
        lax.fori_loop(0, groups, body(u, base), 0)
        base = base + groups * u


def _diff_attn_kernel(q_ref, k_ref, vt_ref, bias_ref, lamv_ref, g_ref, o_ref,
                      m0, l0, a0, m1, l1, a1, z_ref, *, t, lambda_init):
    i = pl.program_id(2)
    stats = ((m0, l0, a0), (m1, l1, a1))
    for s in stats:
        _init_stats(*s)
    q = q_ref[0]

    def scores(j, slot):
        start = pl.multiple_of(j * t, t)
        k = k_ref[0, pl.ds(start, t), :]
        for c in range(2):
            half = slice(c * HEAD_DIM, (c + 1) * HEAD_DIM)
            z_ref[slot, c] = _dot_nt(k[:, half], q[:, half])

    def consume(j, bias_idx, j_next, slot, next_slot):
        if j_next is not None and next_slot == slot:
            held = [z_ref[slot, c] for c in range(2)]
            load = lambda c: held[c]
        else:
            load = lambda c: z_ref[slot, c]
        if j_next is not None:
            scores(j_next, next_slot)
        start = pl.multiple_of(j * t, t)
        vt = vt_ref[:, pl.ds(start, t)]
        for c, s in enumerate(stats):
            if bias_idx is None:
                load_z = functools.partial(load, c)
            else:
                load_z = lambda c=c: load(c) + bias_ref[0, bias_idx]
            _online_softmax_step(load_z, vt, *s)

    scores(0, 0)
    _far_loop(lambda j, j_next, slot, next_slot: consume(j, None, j_next, slot, next_slot),
              0, jnp.maximum(i - 1, 0), DIFF_FAR_UNROLLS)

    @pl.when(i >= 1)
    def _():
        consume(i - 1, 1, i, 0, 1)
        consume(i, 0, None, 1, None)

    @pl.when(i == 0)
    def _():
        consume(i, 0, None, 0, None)

    lamv = lamv_ref[...]
    lam = (jnp.exp(jnp.sum(lamv[0:1] * lamv[1:2], axis=1, keepdims=True))
           - jnp.exp(jnp.sum(lamv[2:3] * lamv[3:4], axis=1, keepdims=True)) + lambda_init)
    o = a0[...] * (1.0 / l0[...]) - lam * (a1[...] * (1.0 / l1[...]))
    ms = jnp.mean(o * o, axis=0, keepdims=True)
    o = o * lax.rsqrt(ms + SUBLN_EPS) * g_ref[...] * (1.0 - lambda_init)
    o_ref[0] = o.T.astype(o_ref.dtype)


def diff_attention(q, k, vt, bias_tiles, lamv, subln_g_col, *, nheads, t, lambda_init):
    b, s, d = q.shape
    hw = 2 * HEAD_DIM
    assert d == nheads * hw and s % t == 0
    kernel = functools.partial(_diff_attn_kernel, t=t, lambda_init=lambda_init)
    stat = lambda: pltpu.VMEM((1, t), F32)
    acc = lambda: pltpu.VMEM((hw, t), F32)
    return pl.pallas_call(
        kernel,
        out_shape=jax.ShapeDtypeStruct((b, s, d), BF16),
        grid=(b, nheads, s // t),
        in_specs=[
            pl.BlockSpec((1, t, hw), lambda bi, h, i: (bi, i, h)),
            pl.BlockSpec((1, s, hw), lambda bi, h, i: (bi, 0, h)),
            pl.BlockSpec((hw, s), lambda bi, h, i: (h, bi)),
            pl.BlockSpec((1, 2, t, t), lambda bi, h, i: (h, 0, 0, 0)),
            pl.BlockSpec((4, HEAD_DIM), lambda bi, h, i: (0, 0)),
            pl.BlockSpec((hw, 1), lambda bi, h, i: (0, 0)),
        ],
        out_specs=pl.BlockSpec((1, t, hw), lambda bi, h, i: (bi, i, h)),
        scratch_shapes=[stat(), stat(), acc(), stat(), stat(), acc(), pltpu.VMEM((2, 2, t, t), F32)],
        compiler_params=_params("parallel", "parallel", "arbitrary"),
        name="diff_attention",
    )(q, k, vt, bias_tiles, lamv, subln_g_col)


def _row_sq_norm_max(x):
    xf = x.astype(F32)
    return jnp.max(jnp.sum(xf * xf, axis=1, keepdims=True), axis=0, keepdims=True)


def _fox_attn_kernel(cend_ref, q_ref, k_ref, vt_ref, kc_ref, cq_ref, o_ref, m_ref, a_ref, z_ref, kn_ref,
                     *, t, nheads):
    hp = pl.program_id(1)
    i = pl.program_id(2)
    nk = pl.num_programs(2)
    heads = range(FOX_HEADS_PER_STEP)
    col = lambda e: slice(e * HEAD_DIM, (e + 1) * HEAD_DIM)
    tile_lane = lax.broadcasted_iota(jnp.int32, (1, kn_ref.shape[-1]), 1)
    for e in heads:
        _init_stats(m_ref.at[e], None, a_ref.at[e])

    @pl.when(i == 0)
    def _():
        def norm_body(j, carry):
            start = pl.multiple_of(j * t, t)
            k = k_ref[0, pl.ds(start, t), :]
            out = []
            for e, (running, kn) in zip(heads, carry):
                running = jnp.maximum(running, _row_sq_norm_max(k[:, col(e)]))
                out.append((running, jnp.where(tile_lane == j, running, kn)))
            return tuple(out)

        init = tuple((jnp.zeros((1, 1), F32), jnp.zeros(tile_lane.shape, F32)) for _ in heads)
        for e, (_, kn) in zip(heads, lax.fori_loop(0, nk, norm_body, init)):
            kn_ref[e] = kn

    cq = cq_ref[0]
    lane = lax.broadcasted_iota(jnp.int32, cq.shape, 1)
    npc = C_PIECES * nheads
    k_diag = k_ref[0, pl.ds(pl.multiple_of(i * t, t), t), :]
    qs, firsts = [], []
    for e in heads:
        h = hp * FOX_HEADS_PER_STEP + e
        q_e = q_ref[0][:, col(e)]
        ct = jnp.sum(jnp.where(lane == h, cq, 0.0), axis=1, keepdims=True) * LOG2E
        qc = jnp.where(lane < npc, jnp.where((lane & (nheads - 1)) == h, 1.0, 0.0), 0.0)
        for p, piece in enumerate(_split_bf16(ct, C_PIECES)):
            qc = jnp.where(lane == npc + p, piece, qc)
        qs.append(jnp.concatenate([q_e, qc.astype(BF16)], axis=1))

        q2 = _row_sq_norm_max(q_e)
        m_lo = jnp.min(jnp.sum(q_e.astype(F32) * k_diag[:, col(e)].astype(F32), axis=1, keepdims=True),
                       axis=0, keepdims=True)
        c_hi = jnp.max(ct, axis=0, keepdims=True)
        r = PRUNE_LOG2 + m_lo - c_hi + cend_ref[0, e:e + 1, :] * LOG2E
        needed = jnp.logical_or(r <= 0.0, q2 * kn_ref[e] >= r * r)
        first = jnp.where(jnp.logical_and(needed, tile_lane < i), tile_lane, i).astype(F32)
        firsts.append(jnp.min(first, axis=1, keepdims=True))
    j_start = functools.reduce(jnp.minimum, firsts)[0, 0].astype(jnp.int32)

    def scores(j, slot):
        start = pl.multiple_of(j * t, t)
        k = k_ref[0, pl.ds(start, t), :]
        kc = kc_ref[0, pl.ds(start, t), :]
        for e in heads:
            z_ref[slot, e] = _dot_nt(jnp.concatenate([k[:, col(e)], kc], axis=1), qs[e])

    def consume(j, diagonal, j_next, slot, next_slot):
        if j_next is not None and next_slot == slot:
            held = [z_ref[slot, e] for e in heads]
            load = lambda e: held[e]
        else:
            load = lambda e: z_ref[slot, e]
        if j_next is not None:
            scores(j_next, next_slot)
        start = pl.multiple_of(j * t, t)
        vt = vt_ref[:, pl.ds(start, t)]
        ones = jnp.ones((ONES_ROWS, t), BF16)
        for e in heads:
            if diagonal:
                def load_z(e=e):
                    ki = lax.broadcasted_iota(jnp.int32, (t, t), 0)
                    qi = lax.broadcasted_iota(jnp.int32, (t, t), 1)
                    return jnp.where(ki <= qi, load(e), NEG)
            else:
                load_z = functools.partial(load, e)
            _online_softmax_step(load_z, jnp.concatenate([vt[col(e)], ones], axis=0),
                                 m_ref.at[e], None, a_ref.at[e])

    scores(i, 0)

    @pl.when(j_start == i)
    def _():
        consume(i, True, None, 0, None)

    @pl.when(j_start < i)
    def _():
        consume(i, True, j_start, 0, 1)
        _far_loop(lambda j, j_next, slot, next_slot: consume(j, False, j_next, slot, next_slot),
                  j_start, i - 1, FOX_FAR_UNROLLS, first_slot=1)
        consume(i - 1, False, None, 1, None)

    o = [a_ref[e, :HEAD_DIM] * (1.0 / a_ref[e, HEAD_DIM:HEAD_DIM + 1]) for e in heads]
    o_ref[0] = jnp.concatenate([x.T for x in o], axis=1).astype(o_ref.dtype)


def fox_attention(q, k, vt, kc, c_cols, c_end, *, nheads, t):
    b, s, d = q.shape
    g = FOX_HEADS_PER_STEP
    gw = g * HEAD_DIM
    ng = nheads // g
    assert d == nheads * HEAD_DIM and nheads % g == 0 and s % t == 0 and c_end.shape == (b * ng, g, s // t)
    assert nheads & (nheads - 1) == 0 and C_PIECES * nheads + C_PIECES <= LANES
    return pl.pallas_call(
        functools.partial(_fox_attn_kernel, t=t, nheads=nheads),
        out_shape=jax.ShapeDtypeStruct((b, s, d), BF16),
        grid=(b, nheads // g, s // t),
        in_specs=[
            pl.BlockSpec((1, g, s // t), lambda bi, hp, i: (bi * ng + hp, 0, 0)),
            pl.BlockSpec((1, t, gw), lambda bi, hp, i: (bi, i, hp)),
            pl.BlockSpec((1, s, gw), lambda bi, hp, i: (bi, 0, hp)),
            pl.BlockSpec((gw, s), lambda bi, hp, i: (hp, bi)),
            pl.BlockSpec((1, s, LANES), lambda bi, hp, i: (bi, 0, 0)),
            pl.BlockSpec((1, t, LANES), lambda bi, hp, i: (bi, i, 0)),
        ],
        out_specs=pl.BlockSpec((1, t, gw), lambda bi, hp, i: (bi, i, hp)),
        scratch_shapes=[pltpu.VMEM((g, 1, t), F32), pltpu.VMEM((g, HEAD_DIM + ONES_ROWS, t), F32),
                        pltpu.VMEM((2, g, t, t), F32), pltpu.VMEM((g, 1, s // t), F32)],
        compiler_params=_params("parallel", "parallel", "arbitrary"),
        name="fox_attention",
    )(c_end, q, k, vt, kc, c_cols)


def _forget_scan_kernel(x_ref, g_ref, wt_ref, b_ref, kc_ref, ccol_ref, carry_ref, *, ts, nheads):
    @pl.when(pl.program_id(1) == 0)
    def _():
        carry_ref[...] = jnp.zeros_like(carry_ref)

    x = x_ref[0]
    hn = x * _rms_scale(x, NORM_EPS) * g_ref[...]
    hn_hi, hn_lo = (p.astype(BF16) for p in _split_bf16(hn, 2))
    w_hi, w_lo = (p.astype(BF16) for p in _split_bf16(wt_ref[...], 2))
    f = _dot_nt(w_hi, hn_hi) + _dot_nt(w_hi, hn_lo) + _dot_nt(w_lo, hn_hi) + b_ref[...]
    logf = -(jnp.maximum(-f, 0.0) + jnp.log1p(jnp.exp(-jnp.abs(f))))
    row = lax.broadcasted_iota(jnp.int32, (ts, ts), 0)
    col = lax.broadcasted_iota(jnp.int32, (ts, ts), 1)
    upper = jnp.where(row <= col, 1.0, 0.0).astype(BF16)
    c = carry_ref[:, 0:1]
    for piece in _split_bf16(logf, C_PIECES):
        c = c + _dot(piece.astype(BF16), upper)
    carry_ref[...] = jnp.broadcast_to(c[:, ts - 1:ts], carry_ref.shape)
    ccol_ref[0] = c.T

    hrow = lax.broadcasted_iota(jnp.int32, c.shape, 0)
    npc = C_PIECES * nheads
    aug = jnp.where(hrow < npc + C_PIECES, 1.0, 0.0)
    for p, piece in reversed(list(enumerate(_split_bf16(c * (-LOG2E), C_PIECES)))):
        aug = jnp.where(hrow < (p + 1) * nheads, piece, aug)
    kc_ref[0] = aug.T.astype(BF16)


def forget_scan(h, g, w_f_t, b_f, *, nheads, ts):
    b, s, d = h.shape
    kernel = functools.partial(_forget_scan_kernel, ts=ts, nheads=nheads)
    return pl.pallas_call(
        kernel,
        out_shape=(jax.ShapeDtypeStruct((b, s, LANES), BF16), jax.ShapeDtypeStruct((b, s, LANES), F32)),
        grid=(b, s // ts),
        in_specs=[
            pl.BlockSpec((1, ts, d), lambda bi, i: (bi, i, 0)),
            pl.BlockSpec((1, d), lambda bi, i: (0, 0)),
            pl.BlockSpec((LANES, d), lambda bi, i: (0, 0)),
            pl.BlockSpec((LANES, 1), lambda bi, i: (0, 0)),
        ],
        out_specs=(pl.BlockSpec((1, ts, LANES), lambda bi, i: (bi, i, 0)),
                   pl.BlockSpec((1, ts, LANES), lambda bi, i: (bi, i, 0))),
        scratch_shapes=[pltpu.VMEM((LANES, LANES), F32)],
        compiler_params=_params("parallel", "arbitrary"),
        name="forget_scan",
    )(h, g, w_f_t, b_f)


def _pick(n, target):
    t = min(n, target)
    while n % t:
        t //= 2
    return t


def kernel(x, rel_bias_table, attn_norm_g, mlp_norm_g, w_qkv_a, lam_q1, lam_k1, lam_q2, lam_k2, subln_g,
           w_o_a, kv_norm_g, w_k_b, w_v_b, w_f_b, b_f_b, w_q_b, w_o_b, w_mlp_in, w_mlp_out, final_norm_g):
    b, s, d = x.shape
    n = b * s
    depth = attn_norm_g.shape[0]
    assert depth == 2 and w_qkv_a.shape[0] == 1 and w_q_b.shape[0] == 1
    diff_heads = d // (2 * HEAD_DIM)
    fox_heads = d // HEAD_DIM
    dff = w_mlp_in.shape[2]

    t_attn = _pick(s, 512)
    tm = _pick(n, 1024)
    tf = _pick(dff, 512)
    ts = _pick(s, 512)

    row = lambda v: v.reshape(1, -1).astype(F32)
    x2 = x.reshape(n, d)

    bias_tiles = t5_bias_tiles(rel_bias_table.astype(F32), t=t_attn)
    g0 = row(attn_norm_g[0])
    w_qkv = w_qkv_a[0].astype(BF16)
    w_mlp_in_bf, w_mlp_out_bf = w_mlp_in.astype(BF16), w_mlp_out.astype(BF16)
    q = norm_matmul(x2, g0, w_qkv, tm=tm, out_scale=Q_SCALE, col_block=0, nout=d)
    k = norm_matmul(x2, g0, w_qkv, tm=tm, col_block=1, nout=d)
    vt = norm_matmul(x2, g0, w_qkv[:, 2 * d:].T, tm=tm, transpose_out=True)
    lamv = jnp.stack([lam_q1[0], lam_k1[0], lam_q2[0], lam_k2[0]]).astype(F32)
    lambda_init = 0.8 - 0.6 * math.exp(-0.3 * 0)
    o = diff_attention(q.reshape(b, s, d), k.reshape(b, s, d), vt, bias_tiles, lamv,
                       subln_g[0].reshape(-1, 1).astype(F32), nheads=diff_heads, t=t_attn, lambda_init=lambda_init)
    h = matmul_residual(o.reshape(n, d), w_o_a[0].astype(BF16), x2, tm=tm)
    h = mlp_block(h, row(mlp_norm_g[0]), w_mlp_in_bf, w_mlp_out_bf, row(final_norm_g),
                  layer=0, tm=tm, tf=tf, final_norm=False)

    gkv = row(kv_norm_g)
    w_f_t = jnp.zeros((LANES, d), F32)
    b_f = jnp.zeros((LANES, 1), F32)
    for p in range(C_PIECES):
        w_f_t = w_f_t.at[p * fox_heads:(p + 1) * fox_heads].set(w_f_b.T.astype(F32))
        b_f = b_f.at[p * fox_heads:(p + 1) * fox_heads, 0].set(b_f_b.astype(F32))
    kc, c_cols = forget_scan(h.reshape(b, s, d), gkv, w_f_t, b_f, nheads=fox_heads, ts=ts)
    k = norm_matmul(h, gkv, w_k_b.astype(BF16), tm=tm)
    vt = norm_matmul(h, gkv, w_v_b.astype(BF16).T, tm=tm, transpose_out=True)

    q = norm_matmul(h, row(attn_norm_g[1]), w_q_b[0].astype(BF16), tm=tm, out_scale=Q_SCALE)
    c_end = c_cols[:, t_attn - 1::t_attn, :fox_heads].transpose(0, 2, 1).reshape(-1, FOX_HEADS_PER_STEP, s // t_attn)
    o = fox_attention(q.reshape(b, s, d), k.reshape(b, s, d), vt, kc, c_cols, c_end, nheads=fox_heads, t=t_attn)
    h = matmul_residual(o.reshape(n, d), w_o_b[0].astype(BF16), h, tm=tm)
    h = mlp_block(h, row(mlp_norm_g[1]), w_mlp_in_bf, w_mlp_out_bf, row(final_norm_g),
                  layer=1, tm=tm, tf=tf, final_norm=True)
    return h.reshape(b, s, d)
```

```python
import functools
import math

import jax
import jax.numpy as jnp
from jax import lax
from jax.experimental import pallas as pl
from jax.experimental.pallas import tpu as pltpu

HEAD_DIM = 128
CHUNK = 64
REL_BUCKETS = 32
REL_MAX_DIST = 128
NORM_EPS = 1e-6
SUBLN_EPS = 1e-5
NEG = -1e30
SCALE = HEAD_DIM ** -0.5
LOG2E = math.log2(math.e)
Q_SCALE = SCALE * LOG2E
M_INIT = -1e30
DIFF_FAR_UNROLLS = (16, 8, 4, 2)
FOX_FAR_UNROLLS = (2,)
ONES_ROWS = 16
PRUNE_LOG2 = -170.0
FOX_HEADS_PER_STEP = 2

V7X_VMEM_BYTES = 64 * 1024 * 1024
V7X_VMEM_RESERVE_BYTES = 8 * 1024 * 1024
V7X_VMEM_LIMIT_BYTES = V7X_VMEM_BYTES - V7X_VMEM_RESERVE_BYTES
LANES = 128
CHUNK_SHIFT = CHUNK.bit_length() - 1
assert CHUNK == 1 << CHUNK_SHIFT
C_PIECES = 3

F32 = jnp.float32
BF16 = jnp.bfloat16


def _params(*semantics):
    return pltpu.CompilerParams(dimension_semantics=semantics, vmem_limit_bytes=V7X_VMEM_LIMIT_BYTES)


def _rms_scale(x, eps):
    return lax.rsqrt(jnp.mean(x * x, axis=-1, keepdims=True) + eps)


def _dot(a, b):
    return jnp.dot(a, b, preferred_element_type=F32)


def _dot_nt(a, b):
    return lax.dot_general(a, b, (((1,), (1,)), ((), ())), preferred_element_type=F32)


def _split_bf16(x, terms):
    pieces = []
    for _ in range(terms - 1):
        p = x.astype(BF16).astype(x.dtype)
        pieces.append(p)
        x = x - p
    pieces.append(x.astype(BF16).astype(x.dtype))
    return pieces


def _norm_matmul_kernel(x_ref, g_ref, w_ref, o_ref, *, transpose_out, out_scale):
    x = x_ref[...]
    hn = (x * _rms_scale(x, NORM_EPS) * g_ref[...]).astype(BF16)
    if transpose_out:
        y = _dot_nt(w_ref[...], hn)
    else:
        y = _dot(hn, w_ref[...])
    if out_scale != 1.0:
        y = y * out_scale
    o_ref[...] = y.astype(o_ref.dtype)


def norm_matmul(x, g, w, *, tm, transpose_out=False, out_scale=1.0, col_block=0, nout=None):
    n, d = x.shape
    if transpose_out:
        assert col_block == 0 and nout is None
        nout = w.shape[0]
        w_spec = pl.BlockSpec((nout, d), lambda i: (0, 0))
        out_shape = jax.ShapeDtypeStruct((nout, n), BF16)
        out_spec = pl.BlockSpec((nout, tm), lambda i: (0, i))
    else:
        nout = w.shape[1] if nout is None else nout
        w_spec = pl.BlockSpec((d, nout), lambda i: (0, col_block))
        out_shape = jax.ShapeDtypeStruct((n, nout), BF16)
        out_spec = pl.BlockSpec((tm, nout), lambda i: (i, 0))
    return pl.pallas_call(
        functools.partial(_norm_matmul_kernel, transpose_out=transpose_out, out_scale=out_scale),
        out_shape=out_shape,
        grid=(n // tm,),
        in_specs=[
            pl.BlockSpec((tm, d), lambda i: (i, 0)),
            pl.BlockSpec((1, d), lambda i: (0, 0)),
            w_spec,
        ],
        out_specs=out_spec,
        compiler_params=_params("parallel"),
        name="norm_matmul_t" if transpose_out else "norm_matmul",
    )(x, g, w)


def _matmul_residual_kernel(a_ref, w_ref, r_ref, o_ref):
    o_ref[...] = r_ref[...] + _dot(a_ref[...], w_ref[...])


def matmul_residual(a, w, res, *, tm):
    n, k = a.shape
    nout = w.shape[1]
    return pl.pallas_call(
        _matmul_residual_kernel,
        out_shape=jax.ShapeDtypeStruct((n, nout), F32),
        grid=(n // tm,),
        in_specs=[
            pl.BlockSpec((tm, k), lambda i: (i, 0)),
            pl.BlockSpec((k, nout), lambda i: (0, 0)),
            pl.BlockSpec((tm, nout), lambda i: (i, 0)),
        ],
        out_specs=pl.BlockSpec((tm, nout), lambda i: (i, 0)),
        compiler_params=_params("parallel"),
        name="matmul_residual",
    )(a, w, res)


def _mlp_kernel(x_ref, g_ref, win_ref, wout_ref, gf_ref, o_ref, hn_ref, *, final_norm):
    f = pl.program_id(1)

    @pl.when(f == 0)
    def _():
        x = x_ref[...]
        hn_ref[...] = (x * _rms_scale(x, NORM_EPS) * g_ref[...]).astype(hn_ref.dtype)
        o_ref[...] = x

    u = _dot(hn_ref[...], win_ref[...])
    u = jnp.square(jnp.maximum(u, 0.0)).astype(BF16)
    o_ref[...] += _dot(u, wout_ref[...])

    if final_norm:
        @pl.when(f == pl.num_programs(1) - 1)
        def _():
            y = o_ref[...]
            o_ref[...] = y * _rms_scale(y, NORM_EPS) * gf_ref[...]


def mlp_block(x, g, w_in, w_out, g_final, *, layer, tm, tf, final_norm):
    n, d = x.shape
    dff = w_in.shape[2]
    return pl.pallas_call(
        functools.partial(_mlp_kernel, final_norm=final_norm),
        out_shape=jax.ShapeDtypeStruct((n, d), F32),
        grid=(n // tm, dff // tf),
        in_specs=[
            pl.BlockSpec((tm, d), lambda i, f: (i, 0)),
            pl.BlockSpec((1, d), lambda i, f: (0, 0)),
            pl.BlockSpec((None, d, tf), lambda i, f: (layer, 0, f)),
            pl.BlockSpec((None, tf, d), lambda i, f: (layer, f, 0)),
            pl.BlockSpec((1, d), lambda i, f: (0, 0)),
        ],
        out_specs=pl.BlockSpec((tm, d), lambda i, f: (i, 0)),
        scratch_shapes=[pltpu.VMEM((tm, d), BF16)],
        compiler_params=_params("parallel", "arbitrary"),
        name="mlp_block",
    )(x, g, w_in, w_out, g_final)


def _t5_bias_kernel(table_ref, o_ref, *, t):
    h = pl.program_id(0)
    half = REL_BUCKETS // 2
    max_exact = half // 2
    ki = lax.broadcasted_iota(jnp.int32, (t, t), 0)
    qi = lax.broadcasted_iota(jnp.int32, (t, t), 1)
    far = table_ref[half - 1, h]
    for d in range(2):
        rel = ki - d * t - qi
        ret = jnp.where(rel > 0, half, 0)
        n = jnp.abs(rel)
        nf = jnp.maximum(n, 1).astype(F32)
        large = max_exact + (jnp.log(nf / max_exact) / math.log(REL_MAX_DIST / max_exact)
                             * (half - max_exact)).astype(jnp.int32)
        large = jnp.minimum(large, half - 1)
        bucket = ret + jnp.where(n < max_exact, n, large)
        val = jnp.zeros((t, t), F32)
        for b in range(REL_BUCKETS):
            val = jnp.where(bucket == b, table_ref[b, h], val)
        val = (val - far) * LOG2E
        if d == 0:
            mask = (ki >> CHUNK_SHIFT) <= (qi >> CHUNK_SHIFT)
            val = jnp.where(mask, val, NEG)
        o_ref[0, d] = val


def t5_bias_tiles(rel_table, *, t):
    nheads = rel_table.shape[1]
    assert t % CHUNK == 0 and t >= REL_MAX_DIST
    return pl.pallas_call(
        functools.partial(_t5_bias_kernel, t=t),
        out_shape=jax.ShapeDtypeStruct((nheads, 2, t, t), F32),
        grid=(nheads,),
        in_specs=[pl.BlockSpec(memory_space=pltpu.SMEM)],
        out_specs=pl.BlockSpec((1, 2, t, t), lambda h: (h, 0, 0, 0)),
        compiler_params=_params("arbitrary"),
        name="t5_bias_tiles",
    )(rel_table)


def _init_stats(m_ref, l_ref, acc_ref):
    m_ref[...] = jnp.full_like(m_ref, M_INIT)
    if l_ref is not None:
        l_ref[...] = jnp.zeros_like(l_ref)
    acc_ref[...] = jnp.zeros_like(acc_ref)


def _online_softmax_step(load_z, vt, m_ref, l_ref, acc_ref):
    m_prev = m_ref[...]
    m_new = jnp.maximum(m_prev, jnp.max(load_z(), axis=0, keepdims=True))
    p = jnp.exp2(load_z() - m_new)
    alpha = jnp.exp2(m_prev - m_new)
    if l_ref is not None:
        l_ref[...] = alpha * l_ref[...] + jnp.sum(p, axis=0, keepdims=True)
    acc_ref[...] = alpha * acc_ref[...] + _dot(vt, p.astype(BF16))
    m_ref[...] = m_new


def _far_loop(consume, j_start, j_end, unrolls, first_slot=0):
    def body(u, base):
        def run(jj, carry):
            for r in range(u):
                slot = (first_slot + r) % 2 if u > 1 else first_slot
                next_slot = (first_slot + r + 1) % 2 if u > 1 else first_slot
                consume(base + jj * u + r, base + jj * u + r + 1, slot, next_slot)
            return carry
        return run

    base = j_start
    for u in tuple(unrolls) + (1,):
        assert u == 1 or u % 2 == 0
        groups = (j_end - base) // u
        lax.fori_loop(0, groups, body(u, base), 0)
        base = base + groups * u


def _diff_attn_kernel(q_ref, k_ref, vt_ref, bias_ref, lamv_ref, g_ref, o_ref,
                      m0, l0, a0, m1, l1, a1, z_ref, *, t, lambda_init):
    i = pl.program_id(2)
    stats = ((m0, l0, a0), (m1, l1, a1))
    for s in stats:
        _init_stats(*s)
    q = q_ref[0]

    def scores(j, slot):
        start = pl.multiple_of(j * t, t)
        k = k_ref[0, pl.ds(start, t), :]
        for c in range(2):
            half = slice(c * HEAD_DIM, (c + 1) * HEAD_DIM)
            z_ref[slot, c] = _dot_nt(k[:, half], q[:, half])

    def consume(j, bias_idx, j_next, slot, next_slot):
        if j_next is not None and next_slot == slot:
            held = [z_ref[slot, c] for c in range(2)]
            load = lambda c: held[c]
        else:
            load = lambda c: z_ref[slot, c]
        if j_next is not None:
            scores(j_next, next_slot)
        start = pl.multiple_of(j * t, t)
        vt = vt_ref[:, pl.ds(start, t)]
        for c, s in enumerate(stats):
            if bias_idx is None:
                load_z = functools.partial(load, c)
            else:
                load_z = lambda c=c: load(c) + bias_ref[0, bias_idx]
            _online_softmax_step(load_z, vt, *s)

    scores(0, 0)
    _far_loop(lambda j, j_next, slot, next_slot: consume(j, None, j_next, slot, next_slot),
              0, jnp.maximum(i - 1, 0), DIFF_FAR_UNROLLS)

    @pl.when(i >= 1)
    def _():
        consume(i - 1, 1, i, 0, 1)
        consume(i, 0, None, 1, None)

    @pl.when(i == 0)
    def _():
        consume(i, 0, None, 0, None)

    lamv = lamv_ref[...]
    lam = (jnp.exp(jnp.sum(lamv[0:1] * lamv[1:2], axis=1, keepdims=True))
           - jnp.exp(jnp.sum(lamv[2:3] * lamv[3:4], axis=1, keepdims=True)) + lambda_init)
    o = a0[...] * (1.0 / l0[...]) - lam * (a1[...] * (1.0 / l1[...]))
    ms = jnp.mean(o * o, axis=0, keepdims=True)
    o = o * lax.rsqrt(ms + SUBLN_EPS) * g_ref[...] * (1.0 - lambda_init)
    o_ref[0] = o.T.astype(o_ref.dtype)


def diff_attention(q, k, vt, bias_tiles, lamv, subln_g_col, *, nheads, t, lambda_init):
    b, s, d = q.shape
    hw = 2 * HEAD_DIM
    assert d == nheads * hw and s % t == 0
    kernel = functools.partial(_diff_attn_kernel, t=t, lambda_init=lambda_init)
    stat = lambda: pltpu.VMEM((1, t), F32)
    acc = lambda: pltpu.VMEM((hw, t), F32)
    return pl.pallas_call(
        kernel,
        out_shape=jax.ShapeDtypeStruct((b, s, d), BF16),
        grid=(b, nheads, s // t),
        in_specs=[
            pl.BlockSpec((1, t, hw), lambda bi, h, i: (bi, i, h)),
            pl.BlockSpec((1, s, hw), lambda bi, h, i: (bi, 0, h)),
            pl.BlockSpec((hw, s), lambda bi, h, i: (h, bi)),
            pl.BlockSpec((1, 2, t, t), lambda bi, h, i: (h, 0, 0, 0)),
            pl.BlockSpec((4, HEAD_DIM), lambda bi, h, i: (0, 0)),
            pl.BlockSpec((hw, 1), lambda bi, h, i: (0, 0)),
        ],
        out_specs=pl.BlockSpec((1, t, hw), lambda bi, h, i: (bi, i, h)),
        scratch_shapes=[stat(), stat(), acc(), stat(), stat(), acc(), pltpu.VMEM((2, 2, t, t), F32)],
        compiler_params=_params("parallel", "parallel", "arbitrary"),
        name="diff_attention",
    )(q, k, vt, bias_tiles, lamv, subln_g_col)


def _row_sq_norm_max(x):
    xf = x.astype(F32)
    return jnp.max(jnp.sum(xf * xf, axis=1, keepdims=True), axis=0, keepdims=True)


def _fox_attn_kernel(cend_ref, q_ref, k_ref, vt_ref, kc_ref, cq_ref, o_ref, m_ref, a_ref, z_ref, kn_ref,
                     *, t, nheads):
    hp = pl.program_id(1)
    i = pl.program_id(2)
    nk = pl.num_programs(2)
    heads = range(FOX_HEADS_PER_STEP)
    col = lambda e: slice(e * HEAD_DIM, (e + 1) * HEAD_DIM)
    tile_lane = lax.broadcasted_iota(jnp.int32, (1, kn_ref.shape[-1]), 1)
    for e in heads:
        _init_stats(m_ref.at[e], None, a_ref.at[e])

    @pl.when(i == 0)
    def _():
        def norm_body(j, carry):
            start = pl.multiple_of(j * t, t)
            k = k_ref[0, pl.ds(start, t), :]
            out = []
            for e, (running, kn) in zip(heads, carry):
                running = jnp.maximum(running, _row_sq_norm_max(k[:, col(e)]))
                out.append((running, jnp.where(tile_lane == j, running, kn)))
            return tuple(out)

        init = tuple((jnp.zeros((1, 1), F32), jnp.zeros(tile_lane.shape, F32)) for _ in heads)
        for e, (_, kn) in zip(heads, lax.fori_loop(0, nk, norm_body, init)):
            kn_ref[e] = kn

    cq = cq_ref[0]
    lane = lax.broadcasted_iota(jnp.int32, cq.shape, 1)
    npc = C_PIECES * nheads
    k_diag = k_ref[0, pl.ds(pl.multiple_of(i * t, t), t), :]
    qs, firsts = [], []
    for e in heads:
        h = hp * FOX_HEADS_PER_STEP + e
        q_e = q_ref[0][:, col(e)]
        ct = jnp.sum(jnp.where(lane == h, cq, 0.0), axis=1, keepdims=True) * LOG2E
        qc = jnp.where(lane < npc, jnp.where((lane & (nheads - 1)) == h, 1.0, 0.0), 0.0)
        for p, piece in enumerate(_split_bf16(ct, C_PIECES)):
            qc = jnp.where(lane == npc + p, piece, qc)
        qs.append(jnp.concatenate([q_e, qc.astype(BF16)], axis=1))

        q2 = _row_sq_norm_max(q_e)
        m_lo = jnp.min(jnp.sum(q_e.astype(F32) * k_diag[:, col(e)].astype(F32), axis=1, keepdims=True),
                       axis=0, keepdims=True)
        c_hi = jnp.max(ct, axis=0, keepdims=True)
        r = PRUNE_LOG2 + m_lo - c_hi + cend_ref[0, e:e + 1, :] * LOG2E
        needed = jnp.logical_or(r <= 0.0, q2 * kn_ref[e] >= r * r)
        first = jnp.where(jnp.logical_and(needed, tile_lane < i), tile_lane, i).astype(F32)
        firsts.append(jnp.min(first, axis=1, keepdims=True))
    j_start = functools.reduce(jnp.minimum, firsts)[0, 0].astype(jnp.int32)

    def scores(j, slot):
        start = pl.multiple_of(j * t, t)
        k = k_ref[0, pl.ds(start, t), :]
        kc = kc_ref[0, pl.ds(start, t), :]
        for e in heads:
            z_ref[slot, e] = _dot_nt(jnp.concatenate([k[:, col(e)], kc], axis=1), qs[e])

    def consume(j, diagonal, j_next, slot, next_slot):
        if j_next is not None and next_slot == slot:
            held = [z_ref[slot, e] for e in heads]
            load = lambda e: held[e]
        else:
            load = lambda e: z_ref[slot, e]
        if j_next is not None:
            scores(j_next, next_slot)
        start = pl.multiple_of(j * t, t)
        vt = vt_ref[:, pl.ds(start, t)]
        ones = jnp.ones((ONES_ROWS, t), BF16)
        for e in heads:
            if diagonal:
                def load_z(e=e):
                    ki = lax.broadcasted_iota(jnp.int32, (t, t), 0)
                    qi = lax.broadcasted_iota(jnp.int32, (t, t), 1)
                    return jnp.where(ki <= qi, load(e), NEG)
            else:
                load_z = functools.partial(load, e)
            _online_softmax_step(load_z, jnp.concatenate([vt[col(e)], ones], axis=0),
                                 m_ref.at[e], None, a_ref.at[e])

    scores(i, 0)

    @pl.when(j_start == i)
    def _():
        consume(i, True, None, 0, None)

    @pl.when(j_start < i)
    def _():
        consume(i, True, j_start, 0, 1)
        _far_loop(lambda j, j_next, slot, next_slot: consume(j, False, j_next, slot, next_slot),
                  j_start, i - 1, FOX_FAR_UNROLLS, first_slot=1)
        consume(i - 1, False, None, 1, None)

    o = [a_ref[e, :HEAD_DIM] * (1.0 / a_ref[e, HEAD_DIM:HEAD_DIM + 1]) for e in heads]
    o_ref[0] = jnp.concatenate([x.T for x in o], axis=1).astype(o_ref.dtype)


def fox_attention(q, k, vt, kc, c_cols, c_end, *, nheads, t):
    b, s, d = q.shape
    g = FOX_HEADS_PER_STEP
    gw = g * HEAD_DIM
    ng = nheads // g
    assert d == nheads * HEAD_DIM and nheads % g == 0 and s % t == 0 and c_end.shape == (b * ng, g, s // t)
    assert nheads & (nheads - 1) == 0 and C_PIECES * nheads + C_PIECES <= LANES
    return pl.pallas_call(
        functools.partial(_fox_attn_kernel, t=t, nheads=nheads),
        out_shape=jax.ShapeDtypeStruct((b, s, d), BF16),
        grid=(b, nheads // g, s // t),
        in_specs=[
            pl.BlockSpec((1, g, s // t), lambda bi, hp, i: (bi * ng + hp, 0, 0)),
            pl.BlockSpec((1, t, gw), lambda bi, hp, i: (bi, i, hp)),
            pl.BlockSpec((1, s, gw), lambda bi, hp, i: (bi, 0, hp)),
            pl.BlockSpec((gw, s), lambda bi, hp, i: (hp, bi)),
            pl.BlockSpec((1, s, LANES), lambda bi, hp, i: (bi, 0, 0)),
            pl.BlockSpec((1, t, LANES), lambda bi, hp, i: (bi, i, 0)),
        ],
        out_specs=pl.BlockSpec((1, t, gw), lambda bi, hp, i: (bi, i, hp)),
        scratch_shapes=[pltpu.VMEM((g, 1, t), F32), pltpu.VMEM((g, HEAD_DIM + ONES_ROWS, t), F32),
                        pltpu.VMEM((2, g, t, t), F32), pltpu.VMEM((g, 1, s // t), F32)],
        compiler_params=_params("parallel", "parallel", "arbitrary"),
        name="fox_attention",
    )(c_end, q, k, vt, kc, c_cols)


def _forget_scan_kernel(x_ref, g_ref, wt_ref, b_ref, kc_ref, ccol_ref, carry_ref, *, ts, nheads):
    @pl.when(pl.program_id(1) == 0)
    def _():
        carry_ref[...] = jnp.zeros_like(carry_ref)

    x = x_ref[0]
    hn = x * _rms_scale(x, NORM_EPS) * g_ref[...]
    hn_hi, hn_lo = (p.astype(BF16) for p in _split_bf16(hn, 2))
    w_hi, w_lo = (p.astype(BF16) for p in _split_bf16(wt_ref[...], 2))
    f = _dot_nt(w_hi, hn_hi) + _dot_nt(w_hi, hn_lo) + _dot_nt(w_lo, hn_hi) + b_ref[...]
    logf = -(jnp.maximum(-f, 0.0) + jnp.log1p(jnp.exp(-jnp.abs(f))))
    row = lax.broadcasted_iota(jnp.int32, (ts, ts), 0)
    col = lax.broadcasted_iota(jnp.int32, (ts, ts), 1)
    upper = jnp.where(row <= col, 1.0, 0.0).astype(BF16)
    c = carry_ref[:, 0:1]
    for piece in _split_bf16(logf, C_PIECES):
        c = c + _dot(piece.astype(BF16), upper)
    carry_ref[...] = jnp.broadcast_to(c[:, ts - 1:ts], carry_ref.shape)
    ccol_ref[0] = c.T

    hrow = lax.broadcasted_iota(jnp.int32, c.shape, 0)
    npc = C_PIECES * nheads
    aug = jnp.where(hrow < npc + C_PIECES, 1.0, 0.0)
    for p, piece in reversed(list(enumerate(_split_bf16(c * (-LOG2E), C_PIECES)))):
        aug = jnp.where(hrow < (p + 1) * nheads, piece, aug)
    kc_ref[0] = aug.T.astype(BF16)


def forget_scan(h, g, w_f_t, b_f, *, nheads, ts):
    b, s, d = h.shape
    kernel = functools.partial(_forget_scan_kernel, ts=ts, nheads=nheads)
    return pl.pallas_call(
        kernel,
        out_shape=(jax.ShapeDtypeStruct((b, s, LANES), BF16), jax.ShapeDtypeStruct((b, s, LANES), F32)),
        grid=(b, s // ts),
        in_specs=[
            pl.BlockSpec((1, ts, d), lambda bi, i: (bi, i, 0)),
            pl.BlockSpec((1, d), lambda bi, i: (0, 0)),
            pl.BlockSpec((LANES, d), lambda bi, i: (0, 0)),
            pl.BlockSpec((LANES, 1), lambda bi, i: (0, 0)),
        ],
        out_specs=(pl.BlockSpec((1, ts, LANES), lambda bi, i: (bi, i, 0)),
                   pl.BlockSpec((1, ts, LANES), lambda bi, i: (bi, i, 0))),
        scratch_shapes=[pltpu.VMEM((LANES, LANES), F32)],
        compiler_params=_params("parallel", "arbitrary"),
        name="forget_scan",
    )(h, g, w_f_t, b_f)


def _pick(n, target):
    t = min(n, target)
    while n % t:
        t //= 2
    return t


def kernel(x, rel_bias_table, attn_norm_g, mlp_norm_g, w_qkv_a, lam_q1, lam_k1, lam_q2, lam_k2, subln_g,
           w_o_a, kv_norm_g, w_k_b, w_v_b, w_f_b, b_f_b, w_q_b, w_o_b, w_mlp_in, w_mlp_out, final_norm_g):
    b, s, d = x.shape
    n = b * s
    depth = attn_norm_g.shape[0]
    assert depth == 2 and w_qkv_a.shape[0] == 1 and w_q_b.shape[0] == 1
    diff_heads = d // (2 * HEAD_DIM)
    fox_heads = d // HEAD_DIM
    dff = w_mlp_in.shape[2]

    t_attn = _pick(s, 512)
    tm = _pick(n, 1024)
    tf = _pick(dff, 512)
    ts = _pick(s, 512)

    row = lambda v: v.reshape(1, -1).astype(F32)
    x2 = x.reshape(n, d)

    bias_tiles = t5_bias_tiles(rel_bias_table.astype(F32), t=t_attn)
    g0 = row(attn_norm_g[0])
    w_qkv = w_qkv_a[0].astype(BF16)
    w_mlp_in_bf, w_mlp_out_bf = w_mlp_in.astype(BF16), w_mlp_out.astype(BF16)
    q = norm_matmul(x2, g0, w_qkv, tm=tm, out_scale=Q_SCALE, col_block=0, nout=d)
    k = norm_matmul(x2, g0, w_qkv, tm=tm, col_block=1, nout=d)
    vt = norm_matmul(x2, g0, w_qkv[:, 2 * d:].T, tm=tm, transpose_out=True)
    lamv = jnp.stack([lam_q1[0], lam_k1[0], lam_q2[0], lam_k2[0]]).astype(F32)
    lambda_init = 0.8 - 0.6 * math.exp(-0.3 * 0)
    o = diff_attention(q.reshape(b, s, d), k.reshape(b, s, d), vt, bias_tiles, lamv,
                       subln_g[0].reshape(-1, 1).astype(F32), nheads=diff_heads, t=t_attn, lambda_init=lambda_init)
    h = matmul_residual(o.reshape(n, d), w_o_a[0].astype(BF16), x2, tm=tm)
    h = mlp_block(h, row(mlp_norm_g[0]), w_mlp_in_bf, w_mlp_out_bf, row(final_norm_g),
                  layer=0, tm=tm, tf=tf, final_norm=False)

    gkv = row(kv_norm_g)
    w_f_t = jnp.zeros((LANES, d), F32)
    b_f = jnp.zeros((LANES, 1), F32)
    for p in range(C_PIECES):
        w_f_t = w_f_t.at[p * fox_heads:(p + 1) * fox_heads].set(w_f_b.T.astype(F32))
        b_f = b_f.at[p * fox_heads:(p + 1) * fox_heads, 0].set(b_f_b.astype(F32))
    kc, c_cols = forget_scan(h.reshape(b, s, d), gkv, w_f_t, b_f, nheads=fox_heads, ts=ts)
    k = norm_matmul(h, gkv, w_k_b.astype(BF16), tm=tm)
    vt = norm_matmul(h, gkv, w_v_b.astype(BF16).T, tm=tm, transpose_out=True)

    q = norm_matmul(h, row(attn_norm_g[1]), w_q_b[0].astype(BF16), tm=tm, out_scale=Q_SCALE)
    c_end = c_cols[:, t_attn - 1::t_attn, :fox_heads].transpose(0, 2, 1).reshape(-1, FOX_HEADS_PER_STEP, s // t_attn)
    o = fox_attention(q.reshape(b, s, d), k.reshape(b, s, d), vt, kc, c_cols, c_end, nheads=fox_heads, t=t_attn)
    h = matmul_residual(o.reshape(n, d), w_o_b[0].astype(BF16), h, tm=tm)
    h = mlp_block(h, row(mlp_norm_g[1]), w_mlp_in_bf, w_mlp_out_bf, row(final_norm_g),
                  layer=1, tm=tm, tf=tf, final_norm=True)
    return h.reshape(b, s, d)
```

```python
import functools
import math

import jax
import jax.numpy as jnp
from jax import lax
from jax.experimental import pallas as pl
from jax.experimental.pallas import tpu as pltpu

HEAD_DIM = 128
CHUNK = 64
REL_BUCKETS = 32
REL_MAX_DIST = 128
NORM_EPS = 1e-6
SUBLN_EPS = 1e-5
NEG = -1e30
SCALE = HEAD_DIM ** -0.5
LOG2E = math.log2(math.e)
Q_SCALE = SCALE * LOG2E
M_INIT = -1e30
DIFF_FAR_UNROLLS = (16, 8, 4, 2)
FOX_FAR_UNROLLS = (2,)
FOX_STRAIGHT_LINE_TILES = 4
ONES_ROWS = 16
PRUNE_LOG2 = -170.0
FOX_HEADS_PER_STEP = 2

V7X_VMEM_BYTES = 64 * 1024 * 1024
V7X_VMEM_RESERVE_BYTES = 8 * 1024 * 1024
V7X_VMEM_LIMIT_BYTES = V7X_VMEM_BYTES - V7X_VMEM_RESERVE_BYTES
LANES = 128
CHUNK_SHIFT = CHUNK.bit_length() - 1
assert CHUNK == 1 << CHUNK_SHIFT
C_PIECES = 3

F32 = jnp.float32
BF16 = jnp.bfloat16


def _params(*semantics):
    return pltpu.CompilerParams(dimension_semantics=semantics, vmem_limit_bytes=V7X_VMEM_LIMIT_BYTES)


def _rms_scale(x, eps):
    return lax.rsqrt(jnp.mean(x * x, axis=-1, keepdims=True) + eps)


def _dot(a, b):
    return jnp.dot(a, b, preferred_element_type=F32)


def _dot_nt(a, b):
    return lax.dot_general(a, b, (((1,), (1,)), ((), ())), preferred_element_type=F32)


def _split_bf16(x, terms):
    pieces = []
    for _ in range(terms - 1):
        p = x.astype(BF16).astype(x.dtype)
        pieces.append(p)
        x = x - p
    pieces.append(x.astype(BF16).astype(x.dtype))
    return pieces


def _norm_matmul_kernel(x_ref, g_ref, w_ref, o_ref, *, transpose_out, out_scale):
    x = x_ref[...]
    hn = (x * _rms_scale(x, NORM_EPS) * g_ref[...]).astype(BF16)
    if transpose_out:
        y = _dot_nt(w_ref[...], hn)
    else:
        y = _dot(hn, w_ref[...])
    if out_scale != 1.0:
        y = y * out_scale
    o_ref[...] = y.astype(o_ref.dtype)


def norm_matmul(x, g, w, *, tm, transpose_out=False, out_scale=1.0, col_block=0, nout=None):
    n, d = x.shape
    if transpose_out:
        assert col_block == 0 and nout is None
        nout = w.shape[0]
        w_spec = pl.BlockSpec((nout, d), lambda i: (0, 0))
        out_shape = jax.ShapeDtypeStruct((nout, n), BF16)
        out_spec = pl.BlockSpec((nout, tm), lambda i: (0, i))
    else:
        nout = w.shape[1] if nout is None else nout
        w_spec = pl.BlockSpec((d, nout), lambda i: (0, col_block))
        out_shape = jax.ShapeDtypeStruct((n, nout), BF16)
        out_spec = pl.BlockSpec((tm, nout), lambda i: (i, 0))
    return pl.pallas_call(
        functools.partial(_norm_matmul_kernel, transpose_out=transpose_out, out_scale=out_scale),
        out_shape=out_shape,
        grid=(n // tm,),
        in_specs=[
            pl.BlockSpec((tm, d), lambda i: (i, 0)),
            pl.BlockSpec((1, d), lambda i: (0, 0)),
            w_spec,
        ],
        out_specs=out_spec,
        compiler_params=_params("parallel"),
        name="norm_matmul_t" if transpose_out else "norm_matmul",
    )(x, g, w)


def _matmul_residual_kernel(a_ref, w_ref, r_ref, o_ref):
    o_ref[...] = r_ref[...] + _dot(a_ref[...], w_ref[...])


def matmul_residual(a, w, res, *, tm):
    n, k = a.shape
    nout = w.shape[1]
    return pl.pallas_call(
        _matmul_residual_kernel,
        out_shape=jax.ShapeDtypeStruct((n, nout), F32),
        grid=(n // tm,),
        in_specs=[
            pl.BlockSpec((tm, k), lambda i: (i, 0)),
            pl.BlockSpec((k, nout), lambda i: (0, 0)),
            pl.BlockSpec((tm, nout), lambda i: (i, 0)),
        ],
        out_specs=pl.BlockSpec((tm, nout), lambda i: (i, 0)),
        compiler_params=_params("parallel"),
        name="matmul_residual",
    )(a, w, res)


def _mlp_kernel(x_ref, g_ref, win_ref, wout_ref, gf_ref, o_ref, hn_ref, *, final_norm):
    f = pl.program_id(1)

    @pl.when(f == 0)
    def _():
        x = x_ref[...]
        hn_ref[...] = (x * _rms_scale(x, NORM_EPS) * g_ref[...]).astype(hn_ref.dtype)
        o_ref[...] = x

    u = _dot(hn_ref[...], win_ref[...])
    u = jnp.square(jnp.maximum(u, 0.0)).astype(BF16)
    o_ref[...] += _dot(u, wout_ref[...])

    if final_norm:
        @pl.when(f == pl.num_programs(1) - 1)
        def _():
            y = o_ref[...]
            o_ref[...] = y * _rms_scale(y, NORM_EPS) * gf_ref[...]


def mlp_block(x, g, w_in, w_out, g_final, *, layer, tm, tf, final_norm):
    n, d = x.shape
    dff = w_in.shape[2]
    return pl.pallas_call(
        functools.partial(_mlp_kernel, final_norm=final_norm),
        out_shape=jax.ShapeDtypeStruct((n, d), F32),
        grid=(n // tm, dff // tf),
        in_specs=[
            pl.BlockSpec((tm, d), lambda i, f: (i, 0)),
            pl.BlockSpec((1, d), lambda i, f: (0, 0)),
            pl.BlockSpec((None, d, tf), lambda i, f: (layer, 0, f)),
            pl.BlockSpec((None, tf, d), lambda i, f: (layer, f, 0)),
            pl.BlockSpec((1, d), lambda i, f: (0, 0)),
        ],
        out_specs=pl.BlockSpec((tm, d), lambda i, f: (i, 0)),
        scratch_shapes=[pltpu.VMEM((tm, d), BF16)],
        compiler_params=_params("parallel", "arbitrary"),
        name="mlp_block",
    )(x, g, w_in, w_out, g_final)


def _t5_bias_kernel(table_ref, o_ref, *, t):
    h = pl.program_id(0)
    half = REL_BUCKETS // 2
    max_exact = half // 2
    ki = lax.broadcasted_iota(jnp.int32, (t, t), 0)
    qi = lax.broadcasted_iota(jnp.int32, (t, t), 1)
    far = table_ref[half - 1, h]
    for d in range(2):
        rel = ki - d * t - qi
        ret = jnp.where(rel > 0, half, 0)
        n = jnp.abs(rel)
        nf = jnp.maximum(n, 1).astype(F32)
        large = max_exact + (jnp.log(nf / max_exact) / math.log(REL_MAX_DIST / max_exact)
                             * (half - max_exact)).astype(jnp.int32)
        large = jnp.minimum(large, half - 1)
        bucket = ret + jnp.where(n < max_exact, n, large)
        val = jnp.zeros((t, t), F32)
        for b in range(REL_BUCKETS):
            val = jnp.where(bucket == b, table_ref[b, h], val)
        val = (val - far) * LOG2E
        if d == 0:
            mask = (ki >> CHUNK_SHIFT) <= (qi >> CHUNK_SHIFT)
            val = jnp.where(mask, val, NEG)
        o_ref[0, d] = val


def t5_bias_tiles(rel_table, *, t):
    nheads = rel_table.shape[1]
    assert t % CHUNK == 0 and t >= REL_MAX_DIST
    return pl.pallas_call(
        functools.partial(_t5_bias_kernel, t=t),
        out_shape=jax.ShapeDtypeStruct((nheads, 2, t, t), F32),
        grid=(nheads,),
        in_specs=[pl.BlockSpec(memory_space=pltpu.SMEM)],
        out_specs=pl.BlockSpec((1, 2, t, t), lambda h: (h, 0, 0, 0)),
        compiler_params=_params("arbitrary"),
        name="t5_bias_tiles",
    )(rel_table)


def _init_stats(m_ref, l_ref, acc_ref):
    m_ref[...] = jnp.full_like(m_ref, M_INIT)
    if l_ref is not None:
        l_ref[...] = jnp.zeros_like(l_ref)
    acc_ref[...] = jnp.zeros_like(acc_ref)


def _online_softmax_step(load_z, vt, m_ref, l_ref, acc_ref):
    m_prev = m_ref[...]
    m_new = jnp.maximum(m_prev, jnp.max(load_z(), axis=0, keepdims=True))
    p = jnp.exp2(load_z() - m_new)
    alpha = jnp.exp2(m_prev - m_new)
    if l_ref is not None:
        l_ref[...] = alpha * l_ref[...] + jnp.sum(p, axis=0, keepdims=True)
    acc_ref[...] = alpha * acc_ref[...] + _dot(vt, p.astype(BF16))
    m_ref[...] = m_new


def _far_loop(consume, j_start, j_end, unrolls, first_slot=0):
    def body(u, base):
        def run(jj, carry):
            for r in range(u):
                slot = (first_slot + r) % 2 if u > 1 else first_slot
                next_slot = (first_slot + r + 1) % 2 if u > 1 else first_slot
                consume(base + jj * u + r, base + jj * u + r + 1, slot, next_slot)
            return carry
        return run

    base = j_start
    for u in tuple(unrolls) + (1,):
        assert u == 1 or u % 2 == 0
        groups = (j_end - base) // u
        lax.fori_loop(0, groups, body(u, base), 0)
        base = base + groups * u


def _diff_attn_kernel(q_ref, k_ref, vt_ref, bias_ref, lamv_ref, g_ref, o_ref,
                      m0, l0, a0, m1, l1, a1, z_ref, *, t, lambda_init):
    i = pl.program_id(2)
    stats = ((m0, l0, a0), (m1, l1, a1))
    for s in stats:
        _init_stats(*s)
    q = q_ref[0]

    def scores(j, slot):
        start = pl.multiple_of(j * t, t)
        k = k_ref[0, pl.ds(start, t), :]
        for c in range(2):
            half = slice(c * HEAD_DIM, (c + 1) * HEAD_DIM)
            z_ref[slot, c] = _dot_nt(k[:, half], q[:, half])

    def consume(j, bias_idx, j_next, slot, next_slot):
        if j_next is not None and next_slot == slot:
            held = [z_ref[slot, c] for c in range(2)]
            load = lambda c: held[c]
        else:
            load = lambda c: z_ref[slot, c]
        if j_next is not None:
            scores(j_next, next_slot)
        start = pl.multiple_of(j * t, t)
        vt = vt_ref[:, pl.ds(start, t)]
        for c, s in enumerate(stats):
            if bias_idx is None:
                load_z = functools.partial(load, c)
            else:
                load_z = lambda c=c: load(c) + bias_ref[0, bias_idx]
            _online_softmax_step(load_z, vt, *s)

    scores(0, 0)
    _far_loop(lambda j, j_next, slot, next_slot: consume(j, None, j_next, slot, next_slot),
              0, jnp.maximum(i - 1, 0), DIFF_FAR_UNROLLS)

    @pl.when(i >= 1)
    def _():
        consume(i - 1, 1, i, 0, 1)
        consume(i, 0, None, 1, None)

    @pl.when(i == 0)
    def _():
        consume(i, 0, None, 0, None)

    lamv = lamv_ref[...]
    lam = (jnp.exp(jnp.sum(lamv[0:1] * lamv[1:2], axis=1, keepdims=True))
           - jnp.exp(jnp.sum(lamv[2:3] * lamv[3:4], axis=1, keepdims=True)) + lambda_init)
    o = a0[...] * (1.0 / l0[...]) - lam * (a1[...] * (1.0 / l1[...]))
    ms = jnp.mean(o * o, axis=0, keepdims=True)
    o = o * lax.rsqrt(ms + SUBLN_EPS) * g_ref[...] * (1.0 - lambda_init)
    o_ref[0] = o.T.astype(o_ref.dtype)


def diff_attention(q, k, vt, bias_tiles, lamv, subln_g_col, *, nheads, t, lambda_init):
    b, s, d = q.shape
    hw = 2 * HEAD_DIM
    assert d == nheads * hw and s % t == 0
    kernel = functools.partial(_diff_attn_kernel, t=t, lambda_init=lambda_init)
    stat = lambda: pltpu.VMEM((1, t), F32)
    acc = lambda: pltpu.VMEM((hw, t), F32)
    return pl.pallas_call(
        kernel,
        out_shape=jax.ShapeDtypeStruct((b, s, d), BF16),
        grid=(b, nheads, s // t),
        in_specs=[
            pl.BlockSpec((1, t, hw), lambda bi, h, i: (bi, i, h)),
            pl.BlockSpec((1, s, hw), lambda bi, h, i: (bi, 0, h)),
            pl.BlockSpec((hw, s), lambda bi, h, i: (h, bi)),
            pl.BlockSpec((1, 2, t, t), lambda bi, h, i: (h, 0, 0, 0)),
            pl.BlockSpec((4, HEAD_DIM), lambda bi, h, i: (0, 0)),
            pl.BlockSpec((hw, 1), lambda bi, h, i: (0, 0)),
        ],
        out_specs=pl.BlockSpec((1, t, hw), lambda bi, h, i: (bi, i, h)),
        scratch_shapes=[stat(), stat(), acc(), stat(), stat(), acc(), pltpu.VMEM((2, 2, t, t), F32)],
        compiler_params=_params("parallel", "parallel", "arbitrary"),
        name="diff_attention",
    )(q, k, vt, bias_tiles, lamv, subln_g_col)


def _row_sq_norm_max(x):
    xf = x.astype(F32)
    return jnp.max(jnp.sum(xf * xf, axis=1, keepdims=True), axis=0, keepdims=True)


def _fox_attn_kernel(cend_ref, q_ref, k_ref, vt_ref, kc_ref, cq_ref, o_ref, m_ref, a_ref, z_ref, kn_ref,
                     *, t, nheads):
    hp = pl.program_id(1)
    i = pl.program_id(2)
    nk = pl.num_programs(2)
    heads = range(FOX_HEADS_PER_STEP)
    col = lambda e: slice(e * HEAD_DIM, (e + 1) * HEAD_DIM)
    tile_lane = lax.broadcasted_iota(jnp.int32, (1, kn_ref.shape[-1]), 1)
    for e in heads:
        _init_stats(m_ref.at[e], None, a_ref.at[e])

    @pl.when(i == 0)
    def _():
        def norm_body(j, carry):
            start = pl.multiple_of(j * t, t)
            k = k_ref[0, pl.ds(start, t), :]
            out = []
            for e, (running, kn) in zip(heads, carry):
                running = jnp.maximum(running, _row_sq_norm_max(k[:, col(e)]))
                out.append((running, jnp.where(tile_lane == j, running, kn)))
            return tuple(out)

        init = tuple((jnp.zeros((1, 1), F32), jnp.zeros(tile_lane.shape, F32)) for _ in heads)
        for e, (_, kn) in zip(heads, lax.fori_loop(0, nk, norm_body, init)):
            kn_ref[e] = kn

    cq = cq_ref[0]
    lane = lax.broadcasted_iota(jnp.int32, cq.shape, 1)
    npc = C_PIECES * nheads
    k_diag = k_ref[0, pl.ds(pl.multiple_of(i * t, t), t), :]
    qs, firsts = [], []
    for e in heads:
        h = hp * FOX_HEADS_PER_STEP + e
        q_e = q_ref[0][:, col(e)]
        ct = jnp.sum(jnp.where(lane == h, cq, 0.0), axis=1, keepdims=True) * LOG2E
        qc = jnp.where(lane < npc, jnp.where((lane & (nheads - 1)) == h, 1.0, 0.0), 0.0)
        for p, piece in enumerate(_split_bf16(ct, C_PIECES)):
            qc = jnp.where(lane == npc + p, piece, qc)
        qs.append(jnp.concatenate([q_e, qc.astype(BF16)], axis=1))

        q2 = _row_sq_norm_max(q_e)
        m_lo = jnp.min(jnp.sum(q_e.astype(F32) * k_diag[:, col(e)].astype(F32), axis=1, keepdims=True),
                       axis=0, keepdims=True)
        c_hi = jnp.max(ct, axis=0, keepdims=True)
        r = PRUNE_LOG2 + m_lo - c_hi + cend_ref[0, e:e + 1, :] * LOG2E
        needed = jnp.logical_or(r <= 0.0, q2 * kn_ref[e] >= r * r)
        first = jnp.where(jnp.logical_and(needed, tile_lane < i), tile_lane, i).astype(F32)
        firsts.append(jnp.min(first, axis=1, keepdims=True))
    j_start = functools.reduce(jnp.minimum, firsts)[0, 0].astype(jnp.int32)

    def scores(j, slot):
        start = pl.multiple_of(j * t, t)
        k = k_ref[0, pl.ds(start, t), :]
        kc = kc_ref[0, pl.ds(start, t), :]
        for e in heads:
            z_ref[slot, e] = _dot_nt(jnp.concatenate([k[:, col(e)], kc], axis=1), qs[e])

    def consume(j, diagonal, j_next, slot, next_slot):
        if j_next is not None and next_slot == slot:
            held = [z_ref[slot, e] for e in heads]
            load = lambda e: held[e]
        else:
            load = lambda e: z_ref[slot, e]
        if j_next is not None:
            scores(j_next, next_slot)
        start = pl.multiple_of(j * t, t)
        vt = vt_ref[:, pl.ds(start, t)]
        ones = jnp.ones((ONES_ROWS, t), BF16)
        for e in heads:
            if diagonal:
                def load_z(e=e):
                    ki = lax.broadcasted_iota(jnp.int32, (t, t), 0)
                    qi = lax.broadcasted_iota(jnp.int32, (t, t), 1)
                    return jnp.where(ki <= qi, load(e), NEG)
            else:
                load_z = functools.partial(load, e)
            _online_softmax_step(load_z, jnp.concatenate([vt[col(e)], ones], axis=0),
                                 m_ref.at[e], None, a_ref.at[e])

    scores(i, 0)
    n_far = i - j_start

    def straight_line(nf):
        def run():
            consume(i, True, j_start if nf else None, 0, 1 if nf else None)
            for r in range(nf):
                last = r == nf - 1
                consume(j_start + r, False, None if last else j_start + r + 1,
                        (1 + r) % 2, None if last else r % 2)
        return run

    for nf in range(FOX_STRAIGHT_LINE_TILES + 1):
        pl.when(n_far == nf)(straight_line(nf))

    @pl.when(n_far > FOX_STRAIGHT_LINE_TILES)
    def _():
        consume(i, True, j_start, 0, 1)
        _far_loop(lambda j, j_next, slot, next_slot: consume(j, False, j_next, slot, next_slot),
                  j_start, i - 1, FOX_FAR_UNROLLS, first_slot=1)
        consume(i - 1, False, None, 1, None)

    o = [a_ref[e, :HEAD_DIM] * (1.0 / a_ref[e, HEAD_DIM:HEAD_DIM + 1]) for e in heads]
    o_ref[0] = jnp.concatenate([x.T for x in o], axis=1).astype(o_ref.dtype)


def fox_attention(q, k, vt, kc, c_cols, c_end, *, nheads, t):
    b, s, d = q.shape
    g = FOX_HEADS_PER_STEP
    gw = g * HEAD_DIM
    ng = nheads // g
    assert d == nheads * HEAD_DIM and nheads % g == 0 and s % t == 0 and c_end.shape == (b * ng, g, s // t)
    assert nheads & (nheads - 1) == 0 and C_PIECES * nheads + C_PIECES <= LANES
    return pl.pallas_call(
        functools.partial(_fox_attn_kernel, t=t, nheads=nheads),
        out_shape=jax.ShapeDtypeStruct((b, s, d), BF16),
        grid=(b, nheads // g, s // t),
        in_specs=[
            pl.BlockSpec((1, g, s // t), lambda bi, hp, i: (bi * ng + hp, 0, 0)),
            pl.BlockSpec((1, t, gw), lambda bi, hp, i: (bi, i, hp)),
            pl.BlockSpec((1, s, gw), lambda bi, hp, i: (bi, 0, hp)),
            pl.BlockSpec((gw, s), lambda bi, hp, i: (hp, bi)),
            pl.BlockSpec((1, s, LANES), lambda bi, hp, i: (bi, 0, 0)),
            pl.BlockSpec((1, t, LANES), lambda bi, hp, i: (bi, i, 0)),
        ],
        out_specs=pl.BlockSpec((1, t, gw), lambda bi, hp, i: (bi, i, hp)),
        scratch_shapes=[pltpu.VMEM((g, 1, t), F32), pltpu.VMEM((g, HEAD_DIM + ONES_ROWS, t), F32),
                        pltpu.VMEM((2, g, t, t), F32), pltpu.VMEM((g, 1, s // t), F32)],
        compiler_params=_params("parallel", "parallel", "arbitrary"),
        name="fox_attention",
    )(c_end, q, k, vt, kc, c_cols)


def _forget_scan_kernel(x_ref, g_ref, wt_ref, b_ref, kc_ref, ccol_ref, carry_ref, *, ts, nheads):
    @pl.when(pl.program_id(1) == 0)
    def _():
        carry_ref[...] = jnp.zeros_like(carry_ref)

    x = x_ref[0]
    hn = x * _rms_scale(x, NORM_EPS) * g_ref[...]
    hn_hi, hn_lo = (p.astype(BF16) for p in _split_bf16(hn, 2))
    w_hi, w_lo = (p.astype(BF16) for p in _split_bf16(wt_ref[...], 2))
    f = _dot_nt(w_hi, hn_hi) + _dot_nt(w_hi, hn_lo) + _dot_nt(w_lo, hn_hi) + b_ref[...]
    logf = -(jnp.maximum(-f, 0.0) + jnp.log1p(jnp.exp(-jnp.abs(f))))
    row = lax.broadcasted_iota(jnp.int32, (ts, ts), 0)
    col = lax.broadcasted_iota(jnp.int32, (ts, ts), 1)
    upper = jnp.where(row <= col, 1.0, 0.0).astype(BF16)
    c = carry_ref[:, 0:1]
    for piece in _split_bf16(logf, C_PIECES):
        c = c + _dot(piece.astype(BF16), upper)
    carry_ref[...] = jnp.broadcast_to(c[:, ts - 1:ts], carry_ref.shape)
    ccol_ref[0] = c.T

    hrow = lax.broadcasted_iota(jnp.int32, c.shape, 0)
    npc = C_PIECES * nheads
    aug = jnp.where(hrow < npc + C_PIECES, 1.0, 0.0)
    for p, piece in reversed(list(enumerate(_split_bf16(c * (-LOG2E), C_PIECES)))):
        aug = jnp.where(hrow < (p + 1) * nheads, piece, aug)
    kc_ref[0] = aug.T.astype(BF16)


def forget_scan(h, g, w_f_t, b_f, *, nheads, ts):
    b, s, d = h.shape
    kernel = functools.partial(_forget_scan_kernel, ts=ts, nheads=nheads)
    return pl.pallas_call(
        kernel,
        out_shape=(jax.ShapeDtypeStruct((b, s, LANES), BF16), jax.ShapeDtypeStruct((b, s, LANES), F32)),
        grid=(b, s // ts),
        in_specs=[
            pl.BlockSpec((1, ts, d), lambda bi, i: (bi, i, 0)),
            pl.BlockSpec((1, d), lambda bi, i: (0, 0)),
            pl.BlockSpec((LANES, d), lambda bi, i: (0, 0)),
            pl.BlockSpec((LANES, 1), lambda bi, i: (0, 0)),
        ],
        out_specs=(pl.BlockSpec((1, ts, LANES), lambda bi, i: (bi, i, 0)),
                   pl.BlockSpec((1, ts, LANES), lambda bi, i: (bi, i, 0))),
        scratch_shapes=[pltpu.VMEM((LANES, LANES), F32)],
        compiler_params=_params("parallel", "arbitrary"),
        name="forget_scan",
    )(h, g, w_f_t, b_f)


def _pick(n, target):
    t = min(n, target)
    while n % t:
        t //= 2
    return t


def kernel(x, rel_bias_table, attn_norm_g, mlp_norm_g, w_qkv_a, lam_q1, lam_k1, lam_q2, lam_k2, subln_g,
           w_o_a, kv_norm_g, w_k_b, w_v_b, w_f_b, b_f_b, w_q_b, w_o_b, w_mlp_in, w_mlp_out, final_norm_g):
    b, s, d = x.shape
    n = b * s
    depth = attn_norm_g.shape[0]
    assert depth == 2 and w_qkv_a.shape[0] == 1 and w_q_b.shape[0] == 1
    diff_heads = d // (2 * HEAD_DIM)
    fox_heads = d // HEAD_DIM
    dff = w_mlp_in.shape[2]

    t_attn = _pick(s, 512)
    tm = _pick(n, 1024)
    tf = _pick(dff, 512)
    ts = _pick(s, 512)

    row = lambda v: v.reshape(1, -1).astype(F32)
    x2 = x.reshape(n, d)

    bias_tiles = t5_bias_tiles(rel_bias_table.astype(F32), t=t_attn)
    g0 = row(attn_norm_g[0])
    w_qkv = w_qkv_a[0].astype(BF16)
    w_mlp_in_bf, w_mlp_out_bf = w_mlp_in.astype(BF16), w_mlp_out.astype(BF16)
    q = norm_matmul(x2, g0, w_qkv, tm=tm, out_scale=Q_SCALE, col_block=0, nout=d)
    k = norm_matmul(x2, g0, w_qkv, tm=tm, col_block=1, nout=d)
    vt = norm_matmul(x2, g0, w_qkv[:, 2 * d:].T, tm=tm, transpose_out=True)
    lamv = jnp.stack([lam_q1[0], lam_k1[0], lam_q2[0], lam_k2[0]]).astype(F32)
    lambda_init = 0.8 - 0.6 * math.exp(-0.3 * 0)
    o = diff_attention(q.reshape(b, s, d), k.reshape(b, s, d), vt, bias_tiles, lamv,
                       subln_g[0].reshape(-1, 1).astype(F32), nheads=diff_heads, t=t_attn, lambda_init=lambda_init)
    h = matmul_residual(o.reshape(n, d), w_o_a[0].astype(BF16), x2, tm=tm)
    h = mlp_block(h, row(mlp_norm_g[0]), w_mlp_in_bf, w_mlp_out_bf, row(final_norm_g),
                  layer=0, tm=tm, tf=tf, final_norm=False)

    gkv = row(kv_norm_g)
    w_f_t = jnp.zeros((LANES, d), F32)
    b_f = jnp.zeros((LANES, 1), F32)
    for p in range(C_PIECES):
        w_f_t = w_f_t.at[p * fox_heads:(p + 1) * fox_heads].set(w_f_b.T.astype(F32))
        b_f = b_f.at[p * fox_heads:(p + 1) * fox_heads, 0].set(b_f_b.astype(F32))
    kc, c_cols = forget_scan(h.reshape(b, s, d), gkv, w_f_t, b_f, nheads=fox_heads, ts=ts)
    k = norm_matmul(h, gkv, w_k_b.astype(BF16), tm=tm)
    vt = norm_matmul(h, gkv, w_v_b.astype(BF16).T, tm=tm, transpose_out=True)

    q = norm_matmul(h, row(attn_norm_g[1]), w_q_b[0].astype(BF16), tm=tm, out_scale=Q_SCALE)
    c_end = c_cols[:, t_attn - 1::t_attn, :fox_heads].transpose(0, 2, 1).reshape(-1, FOX_HEADS_PER_STEP, s // t_attn)
    o = fox_attention(q.reshape(b, s, d), k.reshape(b, s, d), vt, kc, c_cols, c_end, nheads=fox_heads, t=t_attn)
    h = matmul_residual(o.reshape(n, d), w_o_b[0].astype(BF16), h, tm=tm)
    h = mlp_block(h, row(mlp_norm_g[1]), w_mlp_in_bf, w_mlp_out_bf, row(final_norm_g),
                  layer=1, tm=tm, tf=tf, final_norm=True)
    return h.reshape(b, s, d)
```

```python
import functools
import math

import jax
import jax.numpy as jnp
from jax import lax
from jax.experimental import pallas as pl
from jax.experimental.pallas import tpu as pltpu

HEAD_DIM = 128
CHUNK = 64
REL_BUCKETS = 32
REL_MAX_DIST = 128
NORM_EPS = 1e-6
SUBLN_EPS = 1e-5
NEG = -1e30
SCALE = HEAD_DIM ** -0.5
LOG2E = math.log2(math.e)
Q_SCALE = SCALE * LOG2E
M_INIT = -1e30
DIFF_FAR_UNROLLS = (16, 8, 4)
FOX_FAR_UNROLLS = (2,)
FOX_STRAIGHT_LINE_TILES = 4
ONES_ROWS = 16
PRUNE_LOG2 = -170.0
FOX_HEADS_PER_STEP = 2

V7X_VMEM_BYTES = 64 * 1024 * 1024
V7X_VMEM_RESERVE_BYTES = 8 * 1024 * 1024
V7X_VMEM_LIMIT_BYTES = V7X_VMEM_BYTES - V7X_VMEM_RESERVE_BYTES
LANES = 128
CHUNK_SHIFT = CHUNK.bit_length() - 1
assert CHUNK == 1 << CHUNK_SHIFT
C_PIECES = 3

F32 = jnp.float32
BF16 = jnp.bfloat16


def _params(*semantics):
    return pltpu.CompilerParams(dimension_semantics=semantics, vmem_limit_bytes=V7X_VMEM_LIMIT_BYTES)


def _rms_scale(x, eps):
    return lax.rsqrt(jnp.mean(x * x, axis=-1, keepdims=True) + eps)


def _dot(a, b):
    return jnp.dot(a, b, preferred_element_type=F32)


def _dot_nt(a, b):
    return lax.dot_general(a, b, (((1,), (1,)), ((), ())), preferred_element_type=F32)


def _split_bf16(x, terms):
    pieces = []
    for _ in range(terms - 1):
        p = x.astype(BF16).astype(x.dtype)
        pieces.append(p)
        x = x - p
    pieces.append(x.astype(BF16).astype(x.dtype))
    return pieces


def _norm_matmul_kernel(x_ref, g_ref, w_ref, o_ref, *, transpose_out, out_scale):
    x = x_ref[...]
    hn = (x * _rms_scale(x, NORM_EPS) * g_ref[...]).astype(BF16)
    if transpose_out:
        y = _dot_nt(w_ref[...], hn)
    else:
        y = _dot(hn, w_ref[...])
    if out_scale != 1.0:
        y = y * out_scale
    o_ref[...] = y.astype(o_ref.dtype)


def norm_matmul(x, g, w, *, tm, transpose_out=False, out_scale=1.0, col_block=0, nout=None):
    n, d = x.shape
    if transpose_out:
        assert col_block == 0 and nout is None
        nout = w.shape[0]
        w_spec = pl.BlockSpec((nout, d), lambda i: (0, 0))
        out_shape = jax.ShapeDtypeStruct((nout, n), BF16)
        out_spec = pl.BlockSpec((nout, tm), lambda i: (0, i))
    else:
        nout = w.shape[1] if nout is None else nout
        w_spec = pl.BlockSpec((d, nout), lambda i: (0, col_block))
        out_shape = jax.ShapeDtypeStruct((n, nout), BF16)
        out_spec = pl.BlockSpec((tm, nout), lambda i: (i, 0))
    return pl.pallas_call(
        functools.partial(_norm_matmul_kernel, transpose_out=transpose_out, out_scale=out_scale),
        out_shape=out_shape,
        grid=(n // tm,),
        in_specs=[
            pl.BlockSpec((tm, d), lambda i: (i, 0)),
            pl.BlockSpec((1, d), lambda i: (0, 0)),
            w_spec,
        ],
        out_specs=out_spec,
        compiler_params=_params("parallel"),
        name="norm_matmul_t" if transpose_out else "norm_matmul",
    )(x, g, w)


def _matmul_residual_kernel(a_ref, w_ref, r_ref, o_ref):
    o_ref[...] = r_ref[...] + _dot(a_ref[...], w_ref[...])


def matmul_residual(a, w, res, *, tm):
    n, k = a.shape
    nout = w.shape[1]
    return pl.pallas_call(
        _matmul_residual_kernel,
        out_shape=jax.ShapeDtypeStruct((n, nout), F32),
        grid=(n // tm,),
        in_specs=[
            pl.BlockSpec((tm, k), lambda i: (i, 0)),
            pl.BlockSpec((k, nout), lambda i: (0, 0)),
            pl.BlockSpec((tm, nout), lambda i: (i, 0)),
        ],
        out_specs=pl.BlockSpec((tm, nout), lambda i: (i, 0)),
        compiler_params=_params("parallel"),
        name="matmul_residual",
    )(a, w, res)


def _mlp_kernel(x_ref, g_ref, win_ref, wout_ref, gf_ref, o_ref, hn_ref, *, final_norm):
    f = pl.program_id(1)

    @pl.when(f == 0)
    def _():
        x = x_ref[...]
        hn_ref[...] = (x * _rms_scale(x, NORM_EPS) * g_ref[...]).astype(hn_ref.dtype)
        o_ref[...] = x

    u = _dot(hn_ref[...], win_ref[...])
    u = jnp.square(jnp.maximum(u, 0.0)).astype(BF16)
    o_ref[...] += _dot(u, wout_ref[...])

    if final_norm:
        @pl.when(f == pl.num_programs(1) - 1)
        def _():
            y = o_ref[...]
            o_ref[...] = y * _rms_scale(y, NORM_EPS) * gf_ref[...]


def mlp_block(x, g, w_in, w_out, g_final, *, layer, tm, tf, final_norm):
    n, d = x.shape
    dff = w_in.shape[2]
    return pl.pallas_call(
        functools.partial(_mlp_kernel, final_norm=final_norm),
        out_shape=jax.ShapeDtypeStruct((n, d), F32),
        grid=(n // tm, dff // tf),
        in_specs=[
            pl.BlockSpec((tm, d), lambda i, f: (i, 0)),
            pl.BlockSpec((1, d), lambda i, f: (0, 0)),
            pl.BlockSpec((None, d, tf), lambda i, f: (layer, 0, f)),
            pl.BlockSpec((None, tf, d), lambda i, f: (layer, f, 0)),
            pl.BlockSpec((1, d), lambda i, f: (0, 0)),
        ],
        out_specs=pl.BlockSpec((tm, d), lambda i, f: (i, 0)),
        scratch_shapes=[pltpu.VMEM((tm, d), BF16)],
        compiler_params=_params("parallel", "arbitrary"),
        name="mlp_block",
    )(x, g, w_in, w_out, g_final)


def _t5_bias_kernel(table_ref, o_ref, *, t):
    h = pl.program_id(0)
    half = REL_BUCKETS // 2
    max_exact = half // 2
    ki = lax.broadcasted_iota(jnp.int32, (t, t), 0)
    qi = lax.broadcasted_iota(jnp.int32, (t, t), 1)
    far = table_ref[half - 1, h]
    for d in range(2):
        rel = ki - d * t - qi
        ret = jnp.where(rel > 0, half, 0)
        n = jnp.abs(rel)
        nf = jnp.maximum(n, 1).astype(F32)
        large = max_exact + (jnp.log(nf / max_exact) / math.log(REL_MAX_DIST / max_exact)
                             * (half - max_exact)).astype(jnp.int32)
        large = jnp.minimum(large, half - 1)
        bucket = ret + jnp.where(n < max_exact, n, large)
        val = jnp.zeros((t, t), F32)
        for b in range(REL_BUCKETS):
            val = jnp.where(bucket == b, table_ref[b, h], val)
        val = (val - far) * LOG2E
        if d == 0:
            mask = (ki >> CHUNK_SHIFT) <= (qi >> CHUNK_SHIFT)
            val = jnp.where(mask, val, NEG)
        o_ref[0, d] = val


def t5_bias_tiles(rel_table, *, t):
    nheads = rel_table.shape[1]
    assert t % CHUNK == 0 and t >= REL_MAX_DIST
    return pl.pallas_call(
        functools.partial(_t5_bias_kernel, t=t),
        out_shape=jax.ShapeDtypeStruct((nheads, 2, t, t), F32),
        grid=(nheads,),
        in_specs=[pl.BlockSpec(memory_space=pltpu.SMEM)],
        out_specs=pl.BlockSpec((1, 2, t, t), lambda h: (h, 0, 0, 0)),
        compiler_params=_params("arbitrary"),
        name="t5_bias_tiles",
    )(rel_table)


def _init_stats(m_ref, l_ref, acc_ref):
    m_ref[...] = jnp.full_like(m_ref, M_INIT)
    if l_ref is not None:
        l_ref[...] = jnp.zeros_like(l_ref)
    acc_ref[...] = jnp.zeros_like(acc_ref)


def _online_softmax_step(load_z, vt, m_ref, l_ref, acc_ref):
    m_prev = m_ref[...]
    m_new = jnp.maximum(m_prev, jnp.max(load_z(), axis=0, keepdims=True))
    p = jnp.exp2(load_z() - m_new)
    alpha = jnp.exp2(m_prev - m_new)
    if l_ref is not None:
        l_ref[...] = alpha * l_ref[...] + jnp.sum(p, axis=0, keepdims=True)
    acc_ref[...] = alpha * acc_ref[...] + _dot(vt, p.astype(BF16))
    m_ref[...] = m_new


def _far_loop(consume, j_start, j_end, unrolls, first_slot=0, singles=True):
    def body(u, base):
        def run(jj, carry):
            for r in range(u):
                slot = (first_slot + r) % 2 if u > 1 else first_slot
                next_slot = (first_slot + r + 1) % 2 if u > 1 else first_slot
                consume(base + jj * u + r, base + jj * u + r + 1, slot, next_slot)
            return carry
        return run

    base = j_start
    for u in tuple(unrolls) + ((1,) if singles else ()):
        assert u == 1 or u % 2 == 0
        groups = (j_end - base) // u
        lax.fori_loop(0, groups, body(u, base), 0)
        base = base + groups * u


def _diff_attn_kernel(q_ref, k_ref, vt_ref, bias_ref, lamv_ref, g_ref, o_ref,
                      m0, l0, a0, m1, l1, a1, z_ref, *, t, lambda_init):
    i = pl.program_id(2)
    stats = ((m0, l0, a0), (m1, l1, a1))
    for s in stats:
        _init_stats(*s)
    q = q_ref[0]

    def scores(j, slot):
        start = pl.multiple_of(j * t, t)
        k = k_ref[0, pl.ds(start, t), :]
        for c in range(2):
            half = slice(c * HEAD_DIM, (c + 1) * HEAD_DIM)
            z_ref[slot, c] = _dot_nt(k[:, half], q[:, half])

    def consume(j, bias_idx, j_next, slot, next_slot):
        if j_next is not None and next_slot == slot:
            held = [z_ref[slot, c] for c in range(2)]
            load = lambda c: held[c]
        else:
            load = lambda c: z_ref[slot, c]
        if j_next is not None:
            scores(j_next, next_slot)
        start = pl.multiple_of(j * t, t)
        vt = vt_ref[:, pl.ds(start, t)]
        for c, s in enumerate(stats):
            if bias_idx is None:
                load_z = functools.partial(load, c)
            else:
                load_z = lambda c=c: load(c) + bias_ref[0, bias_idx]
            _online_softmax_step(load_z, vt, *s)

    scores(0, 0)
    n_far = jnp.maximum(i - 1, 0)
    n_tail = n_far % DIFF_FAR_UNROLLS[-1]
    _far_loop(lambda j, j_next, slot, next_slot: consume(j, None, j_next, slot, next_slot),
              0, n_far - n_tail, DIFF_FAR_UNROLLS, singles=False)

    def tail(nt):
        def run():
            tiles = [(n_far - nt + r, None) for r in range(nt)] + [(i - 1, 1), (i, 0)]
            for r, (j, bias_idx) in enumerate(tiles):
                last = r == len(tiles) - 1
                consume(j, bias_idx, None if last else tiles[r + 1][0], r % 2, None if last else (r + 1) % 2)
        return run

    for nt in range(DIFF_FAR_UNROLLS[-1]):
        pl.when(jnp.logical_and(i >= 1, n_tail == nt))(tail(nt))

    @pl.when(i == 0)
    def _():
        consume(i, 0, None, 0, None)

    lamv = lamv_ref[...]
    lam = (jnp.exp(jnp.sum(lamv[0:1] * lamv[1:2], axis=1, keepdims=True))
           - jnp.exp(jnp.sum(lamv[2:3] * lamv[3:4], axis=1, keepdims=True)) + lambda_init)
    o = a0[...] * (1.0 / l0[...]) - lam * (a1[...] * (1.0 / l1[...]))
    ms = jnp.mean(o * o, axis=0, keepdims=True)
    o = o * lax.rsqrt(ms + SUBLN_EPS) * g_ref[...] * (1.0 - lambda_init)
    o_ref[0] = o.T.astype(o_ref.dtype)


def diff_attention(q, k, vt, bias_tiles, lamv, subln_g_col, *, nheads, t, lambda_init):
    b, s, d = q.shape
    hw = 2 * HEAD_DIM
    assert d == nheads * hw and s % t == 0
    kernel = functools.partial(_diff_attn_kernel, t=t, lambda_init=lambda_init)
    stat = lambda: pltpu.VMEM((1, t), F32)
    acc = lambda: pltpu.VMEM((hw, t), F32)
    return pl.pallas_call(
        kernel,
        out_shape=jax.ShapeDtypeStruct((b, s, d), BF16),
        grid=(b, nheads, s // t),
        in_specs=[
            pl.BlockSpec((1, t, hw), lambda bi, h, i: (bi, i, h)),
            pl.BlockSpec((1, s, hw), lambda bi, h, i: (bi, 0, h)),
            pl.BlockSpec((hw, s), lambda bi, h, i: (h, bi)),
            pl.BlockSpec((1, 2, t, t), lambda bi, h, i: (h, 0, 0, 0)),
            pl.BlockSpec((4, HEAD_DIM), lambda bi, h, i: (0, 0)),
            pl.BlockSpec((hw, 1), lambda bi, h, i: (0, 0)),
        ],
        out_specs=pl.BlockSpec((1, t, hw), lambda bi, h, i: (bi, i, h)),
        scratch_shapes=[stat(), stat(), acc(), stat(), stat(), acc(), pltpu.VMEM((2, 2, t, t), F32)],
        compiler_params=_params("parallel", "parallel", "arbitrary"),
        name="diff_attention",
    )(q, k, vt, bias_tiles, lamv, subln_g_col)


def _row_sq_norm_max(x):
    xf = x.astype(F32)
    return jnp.max(jnp.sum(xf * xf, axis=1, keepdims=True), axis=0, keepdims=True)


def _fox_attn_kernel(cend_ref, q_ref, k_ref, vt_ref, kc_ref, cq_ref, o_ref, m_ref, a_ref, z_ref, kn_ref,
                     *, t, nheads):
    hp = pl.program_id(1)
    i = pl.program_id(2)
    nk = pl.num_programs(2)
    heads = range(FOX_HEADS_PER_STEP)
    col = lambda e: slice(e * HEAD_DIM, (e + 1) * HEAD_DIM)
    tile_lane = lax.broadcasted_iota(jnp.int32, (1, kn_ref.shape[-1]), 1)
    for e in heads:
        _init_stats(m_ref.at[e], None, a_ref.at[e])

    @pl.when(i == 0)
    def _():
        def norm_body(j, carry):
            start = pl.multiple_of(j * t, t)
            k = k_ref[0, pl.ds(start, t), :]
            out = []
            for e, (running, kn) in zip(heads, carry):
                running = jnp.maximum(running, _row_sq_norm_max(k[:, col(e)]))
                out.append((running, jnp.where(tile_lane == j, running, kn)))
            return tuple(out)

        init = tuple((jnp.zeros((1, 1), F32), jnp.zeros(tile_lane.shape, F32)) for _ in heads)
        for e, (_, kn) in zip(heads, lax.fori_loop(0, nk, norm_body, init)):
            kn_ref[e] = kn

    cq = cq_ref[0]
    lane = lax.broadcasted_iota(jnp.int32, cq.shape, 1)
    npc = C_PIECES * nheads
    k_diag = k_ref[0, pl.ds(pl.multiple_of(i * t, t), t), :]
    qs, firsts = [], []
    for e in heads:
        h = hp * FOX_HEADS_PER_STEP + e
        q_e = q_ref[0][:, col(e)]
        ct = jnp.sum(jnp.where(lane == h, cq, 0.0), axis=1, keepdims=True) * LOG2E
        qc = jnp.where(lane < npc, jnp.where((lane & (nheads - 1)) == h, 1.0, 0.0), 0.0)
        for p, piece in enumerate(_split_bf16(ct, C_PIECES)):
            qc = jnp.where(lane == npc + p, piece, qc)
        qs.append(jnp.concatenate([q_e, qc.astype(BF16)], axis=1))

        q2 = _row_sq_norm_max(q_e)
        m_lo = jnp.min(jnp.sum(q_e.astype(F32) * k_diag[:, col(e)].astype(F32), axis=1, keepdims=True),
                       axis=0, keepdims=True)
        c_hi = jnp.max(ct, axis=0, keepdims=True)
        r = PRUNE_LOG2 + m_lo - c_hi + cend_ref[0, e:e + 1, :] * LOG2E
        needed = jnp.logical_or(r <= 0.0, q2 * kn_ref[e] >= r * r)
        first = jnp.where(jnp.logical_and(needed, tile_lane < i), tile_lane, i).astype(F32)
        firsts.append(jnp.min(first, axis=1, keepdims=True))
    j_start = functools.reduce(jnp.minimum, firsts)[0, 0].astype(jnp.int32)

    def scores(j, slot):
        start = pl.multiple_of(j * t, t)
        k = k_ref[0, pl.ds(start, t), :]
        kc = kc_ref[0, pl.ds(start, t), :]
        for e in heads:
            z_ref[slot, e] = _dot_nt(jnp.concatenate([k[:, col(e)], kc], axis=1), qs[e])

    def consume(j, diagonal, j_next, slot, next_slot):
        if j_next is not None and next_slot == slot:
            held = [z_ref[slot, e] for e in heads]
            load = lambda e: held[e]
        else:
            load = lambda e: z_ref[slot, e]
        if j_next is not None:
            scores(j_next, next_slot)
        start = pl.multiple_of(j * t, t)
        vt = vt_ref[:, pl.ds(start, t)]
        ones = jnp.ones((ONES_ROWS, t), BF16)
        for e in heads:
            if diagonal:
                def load_z(e=e):
                    ki = lax.broadcasted_iota(jnp.int32, (t, t), 0)
                    qi = lax.broadcasted_iota(jnp.int32, (t, t), 1)
                    return jnp.where(ki <= qi, load(e), NEG)
            else:
                load_z = functools.partial(load, e)
            _online_softmax_step(load_z, jnp.concatenate([vt[col(e)], ones], axis=0),
                                 m_ref.at[e], None, a_ref.at[e])

    scores(i, 0)
    n_far = i - j_start

    def straight_line(nf):
        def run():
            consume(i, True, j_start if nf else None, 0, 1 if nf else None)
            for r in range(nf):
                last = r == nf - 1
                consume(j_start + r, False, None if last else j_start + r + 1,
                        (1 + r) % 2, None if last else r % 2)
        return run

    for nf in range(FOX_STRAIGHT_LINE_TILES + 1):
        pl.when(n_far == nf)(straight_line(nf))

    @pl.when(n_far > FOX_STRAIGHT_LINE_TILES)
    def _():
        consume(i, True, j_start, 0, 1)
        _far_loop(lambda j, j_next, slot, next_slot: consume(j, False, j_next, slot, next_slot),
                  j_start, i - 1, FOX_FAR_UNROLLS, first_slot=1)
        consume(i - 1, False, None, 1, None)

    o = [a_ref[e, :HEAD_DIM] * (1.0 / a_ref[e, HEAD_DIM:HEAD_DIM + 1]) for e in heads]
    o_ref[0] = jnp.concatenate([x.T for x in o], axis=1).astype(o_ref.dtype)


def fox_attention(q, k, vt, kc, c_cols, c_end, *, nheads, t):
    b, s, d = q.shape
    g = FOX_HEADS_PER_STEP
    gw = g * HEAD_DIM
    ng = nheads // g
    assert d == nheads * HEAD_DIM and nheads % g == 0 and s % t == 0 and c_end.shape == (b * ng, g, s // t)
    assert nheads & (nheads - 1) == 0 and C_PIECES * nheads + C_PIECES <= LANES
    return pl.pallas_call(
        functools.partial(_fox_attn_kernel, t=t, nheads=nheads),
        out_shape=jax.ShapeDtypeStruct((b, s, d), BF16),
        grid=(b, nheads // g, s // t),
        in_specs=[
            pl.BlockSpec((1, g, s // t), lambda bi, hp, i: (bi * ng + hp, 0, 0)),
            pl.BlockSpec((1, t, gw), lambda bi, hp, i: (bi, i, hp)),
            pl.BlockSpec((1, s, gw), lambda bi, hp, i: (bi, 0, hp)),
            pl.BlockSpec((gw, s), lambda bi, hp, i: (hp, bi)),
            pl.BlockSpec((1, s, LANES), lambda bi, hp, i: (bi, 0, 0)),
            pl.BlockSpec((1, t, LANES), lambda bi, hp, i: (bi, i, 0)),
        ],
        out_specs=pl.BlockSpec((1, t, gw), lambda bi, hp, i: (bi, i, hp)),
        scratch_shapes=[pltpu.VMEM((g, 1, t), F32), pltpu.VMEM((g, HEAD_DIM + ONES_ROWS, t), F32),
                        pltpu.VMEM((2, g, t, t), F32), pltpu.VMEM((g, 1, s // t), F32)],
        compiler_params=_params("parallel", "parallel", "arbitrary"),
        name="fox_attention",
    )(c_end, q, k, vt, kc, c_cols)


def _forget_scan_kernel(x_ref, g_ref, wt_ref, b_ref, kc_ref, ccol_ref, carry_ref, *, ts, nheads):
    @pl.when(pl.program_id(1) == 0)
    def _():
        carry_ref[...] = jnp.zeros_like(carry_ref)

    x = x_ref[0]
    hn = x * _rms_scale(x, NORM_EPS) * g_ref[...]
    hn_hi, hn_lo = (p.astype(BF16) for p in _split_bf16(hn, 2))
    w_hi, w_lo = (p.astype(BF16) for p in _split_bf16(wt_ref[...], 2))
    f = _dot_nt(w_hi, hn_hi) + _dot_nt(w_hi, hn_lo) + _dot_nt(w_lo, hn_hi) + b_ref[...]
    logf = -(jnp.maximum(-f, 0.0) + jnp.log1p(jnp.exp(-jnp.abs(f))))
    row = lax.broadcasted_iota(jnp.int32, (ts, ts), 0)
    col = lax.broadcasted_iota(jnp.int32, (ts, ts), 1)
    upper = jnp.where(row <= col, 1.0, 0.0).astype(BF16)
    c = carry_ref[:, 0:1]
    for piece in _split_bf16(logf, C_PIECES):
        c = c + _dot(piece.astype(BF16), upper)
    carry_ref[...] = jnp.broadcast_to(c[:, ts - 1:ts], carry_ref.shape)
    ccol_ref[0] = c.T

    hrow = lax.broadcasted_iota(jnp.int32, c.shape, 0)
    npc = C_PIECES * nheads
    aug = jnp.where(hrow < npc + C_PIECES, 1.0, 0.0)
    for p, piece in reversed(list(enumerate(_split_bf16(c * (-LOG2E), C_PIECES)))):
        aug = jnp.where(hrow < (p + 1) * nheads, piece, aug)
    kc_ref[0] = aug.T.astype(BF16)


def forget_scan(h, g, w_f_t, b_f, *, nheads, ts):
    b, s, d = h.shape
    kernel = functools.partial(_forget_scan_kernel, ts=ts, nheads=nheads)
    return pl.pallas_call(
        kernel,
        out_shape=(jax.ShapeDtypeStruct((b, s, LANES), BF16), jax.ShapeDtypeStruct((b, s, LANES), F32)),
        grid=(b, s // ts),
        in_specs=[
            pl.BlockSpec((1, ts, d), lambda bi, i: (bi, i, 0)),
            pl.BlockSpec((1, d), lambda bi, i: (0, 0)),
            pl.BlockSpec((LANES, d), lambda bi, i: (0, 0)),
            pl.BlockSpec((LANES, 1), lambda bi, i: (0, 0)),
        ],
        out_specs=(pl.BlockSpec((1, ts, LANES), lambda bi, i: (bi, i, 0)),
                   pl.BlockSpec((1, ts, LANES), lambda bi, i: (bi, i, 0))),
        scratch_shapes=[pltpu.VMEM((LANES, LANES), F32)],
        compiler_params=_params("parallel", "arbitrary"),
        name="forget_scan",
    )(h, g, w_f_t, b_f)


def _pick(n, target):
    t = min(n, target)
    while n % t:
        t //= 2
    return t


def kernel(x, rel_bias_table, attn_norm_g, mlp_norm_g, w_qkv_a, lam_q1, lam_k1, lam_q2, lam_k2, subln_g,
           w_o_a, kv_norm_g, w_k_b, w_v_b, w_f_b, b_f_b, w_q_b, w_o_b, w_mlp_in, w_mlp_out, final_norm_g):
    b, s, d = x.shape
    n = b * s
    depth = attn_norm_g.shape[0]
    assert depth == 2 and w_qkv_a.shape[0] == 1 and w_q_b.shape[0] == 1
    diff_heads = d // (2 * HEAD_DIM)
    fox_heads = d // HEAD_DIM
    dff = w_mlp_in.shape[2]

    t_attn = _pick(s, 512)
    tm = _pick(n, 1024)
    tf = _pick(dff, 512)
    ts = _pick(s, 512)

    row = lambda v: v.reshape(1, -1).astype(F32)
    x2 = x.reshape(n, d)

    bias_tiles = t5_bias_tiles(rel_bias_table.astype(F32), t=t_attn)
    g0 = row(attn_norm_g[0])
    w_qkv = w_qkv_a[0].astype(BF16)
    w_mlp_in_bf, w_mlp_out_bf = w_mlp_in.astype(BF16), w_mlp_out.astype(BF16)
    q = norm_matmul(x2, g0, w_qkv, tm=tm, out_scale=Q_SCALE, col_block=0, nout=d)
    k = norm_matmul(x2, g0, w_qkv, tm=tm, col_block=1, nout=d)
    vt = norm_matmul(x2, g0, w_qkv[:, 2 * d:].T, tm=tm, transpose_out=True)
    lamv = jnp.stack([lam_q1[0], lam_k1[0], lam_q2[0], lam_k2[0]]).astype(F32)
    lambda_init = 0.8 - 0.6 * math.exp(-0.3 * 0)
    o = diff_attention(q.reshape(b, s, d), k.reshape(b, s, d), vt, bias_tiles, lamv,
                       subln_g[0].reshape(-1, 1).astype(F32), nheads=diff_heads, t=t_attn, lambda_init=lambda_init)
    h = matmul_residual(o.reshape(n, d), w_o_a[0].astype(BF16), x2, tm=tm)
    h = mlp_block(h, row(mlp_norm_g[0]), w_mlp_in_bf, w_mlp_out_bf, row(final_norm_g),
                  layer=0, tm=tm, tf=tf, final_norm=False)

    gkv = row(kv_norm_g)
    w_f_t = jnp.zeros((LANES, d), F32)
    b_f = jnp.zeros((LANES, 1), F32)
    for p in range(C_PIECES):
        w_f_t = w_f_t.at[p * fox_heads:(p + 1) * fox_heads].set(w_f_b.T.astype(F32))
        b_f = b_f.at[p * fox_heads:(p + 1) * fox_heads, 0].set(b_f_b.astype(F32))
    kc, c_cols = forget_scan(h.reshape(b, s, d), gkv, w_f_t, b_f, nheads=fox_heads, ts=ts)
    k = norm_matmul(h, gkv, w_k_b.astype(BF16), tm=tm)
    vt = norm_matmul(h, gkv, w_v_b.astype(BF16).T, tm=tm, transpose_out=True)

    q = norm_matmul(h, row(attn_norm_g[1]), w_q_b[0].astype(BF16), tm=tm, out_scale=Q_SCALE)
    c_end = c_cols[:, t_attn - 1::t_attn, :fox_heads].transpose(0, 2, 1).reshape(-1, FOX_HEADS_PER_STEP, s // t_attn)
    o = fox_attention(q.reshape(b, s, d), k.reshape(b, s, d), vt, kc, c_cols, c_end, nheads=fox_heads, t=t_attn)
    h = matmul_residual(o.reshape(n, d), w_o_b[0].astype(BF16), h, tm=tm)
    h = mlp_block(h, row(mlp_norm_g[1]), w_mlp_in_bf, w_mlp_out_bf, row(final_norm_g),
                  layer=1, tm=tm, tf=tf, final_norm=True)
    return h.reshape(b, s, d)
```

```python
import functools
import math

import jax
import jax.numpy as jnp
from jax import lax
from jax.experimental import pallas as pl
from jax.experimental.pallas import tpu as pltpu

HEAD_DIM = 128
CHUNK = 64
REL_BUCKETS = 32
REL_MAX_DIST = 128
NORM_EPS = 1e-6
SUBLN_EPS = 1e-5
NEG = -1e30
SCALE = HEAD_DIM ** -0.5
LOG2E = math.log2(math.e)
Q_SCALE = SCALE * LOG2E
M_INIT = -1e30
DIFF_FAR_UNROLLS = (16, 8)
FOX_FAR_UNROLLS = (2,)
FOX_STRAIGHT_LINE_TILES = 4
ONES_ROWS = 16
PRUNE_LOG2 = -170.0
FOX_HEADS_PER_STEP = 2

V7X_VMEM_BYTES = 64 * 1024 * 1024
V7X_VMEM_RESERVE_BYTES = 8 * 1024 * 1024
V7X_VMEM_LIMIT_BYTES = V7X_VMEM_BYTES - V7X_VMEM_RESERVE_BYTES
LANES = 128
CHUNK_SHIFT = CHUNK.bit_length() - 1
assert CHUNK == 1 << CHUNK_SHIFT
C_PIECES = 3

F32 = jnp.float32
BF16 = jnp.bfloat16


def _params(*semantics):
    return pltpu.CompilerParams(dimension_semantics=semantics, vmem_limit_bytes=V7X_VMEM_LIMIT_BYTES)


def _rms_scale(x, eps):
    return lax.rsqrt(jnp.mean(x * x, axis=-1, keepdims=True) + eps)


def _dot(a, b):
    return jnp.dot(a, b, preferred_element_type=F32)


def _dot_nt(a, b):
    return lax.dot_general(a, b, (((1,), (1,)), ((), ())), preferred_element_type=F32)


def _split_bf16(x, terms):
    pieces = []
    for _ in range(terms - 1):
        p = x.astype(BF16).astype(x.dtype)
        pieces.append(p)
        x = x - p
    pieces.append(x.astype(BF16).astype(x.dtype))
    return pieces


def _norm_matmul_kernel(x_ref, g_ref, w_ref, o_ref, *, transpose_out, out_scale):
    x = x_ref[...]
    hn = (x * _rms_scale(x, NORM_EPS) * g_ref[...]).astype(BF16)
    if transpose_out:
        y = _dot_nt(w_ref[...], hn)
    else:
        y = _dot(hn, w_ref[...])
    if out_scale != 1.0:
        y = y * out_scale
    o_ref[...] = y.astype(o_ref.dtype)


def norm_matmul(x, g, w, *, tm, transpose_out=False, out_scale=1.0, col_block=0, nout=None):
    n, d = x.shape
    if transpose_out:
        assert col_block == 0 and nout is None
        nout = w.shape[0]
        w_spec = pl.BlockSpec((nout, d), lambda i: (0, 0))
        out_shape = jax.ShapeDtypeStruct((nout, n), BF16)
        out_spec = pl.BlockSpec((nout, tm), lambda i: (0, i))
    else:
        nout = w.shape[1] if nout is None else nout
        w_spec = pl.BlockSpec((d, nout), lambda i: (0, col_block))
        out_shape = jax.ShapeDtypeStruct((n, nout), BF16)
        out_spec = pl.BlockSpec((tm, nout), lambda i: (i, 0))
    return pl.pallas_call(
        functools.partial(_norm_matmul_kernel, transpose_out=transpose_out, out_scale=out_scale),
        out_shape=out_shape,
        grid=(n // tm,),
        in_specs=[
            pl.BlockSpec((tm, d), lambda i: (i, 0)),
            pl.BlockSpec((1, d), lambda i: (0, 0)),
            w_spec,
        ],
        out_specs=out_spec,
        compiler_params=_params("parallel"),
        name="norm_matmul_t" if transpose_out else "norm_matmul",
    )(x, g, w)


def _matmul_residual_kernel(a_ref, w_ref, r_ref, o_ref):
    o_ref[...] = r_ref[...] + _dot(a_ref[...], w_ref[...])


def matmul_residual(a, w, res, *, tm):
    n, k = a.shape
    nout = w.shape[1]
    return pl.pallas_call(
        _matmul_residual_kernel,
        out_shape=jax.ShapeDtypeStruct((n, nout), F32),
        grid=(n // tm,),
        in_specs=[
            pl.BlockSpec((tm, k), lambda i: (i, 0)),
            pl.BlockSpec((k, nout), lambda i: (0, 0)),
            pl.BlockSpec((tm, nout), lambda i: (i, 0)),
        ],
        out_specs=pl.BlockSpec((tm, nout), lambda i: (i, 0)),
        compiler_params=_params("parallel"),
        name="matmul_residual",
    )(a, w, res)


def _mlp_kernel(x_ref, g_ref, win_ref, wout_ref, gf_ref, o_ref, hn_ref, *, final_norm):
    f = pl.program_id(1)

    @pl.when(f == 0)
    def _():
        x = x_ref[...]
        hn_ref[...] = (x * _rms_scale(x, NORM_EPS) * g_ref[...]).astype(hn_ref.dtype)
        o_ref[...] = x

    u = _dot(hn_ref[...], win_ref[...])
    u = jnp.square(jnp.maximum(u, 0.0)).astype(BF16)
    o_ref[...] += _dot(u, wout_ref[...])

    if final_norm:
        @pl.when(f == pl.num_programs(1) - 1)
        def _():
            y = o_ref[...]
            o_ref[...] = y * _rms_scale(y, NORM_EPS) * gf_ref[...]


def mlp_block(x, g, w_in, w_out, g_final, *, layer, tm, tf, final_norm):
    n, d = x.shape
    dff = w_in.shape[2]
    return pl.pallas_call(
        functools.partial(_mlp_kernel, final_norm=final_norm),
        out_shape=jax.ShapeDtypeStruct((n, d), F32),
        grid=(n // tm, dff // tf),
        in_specs=[
            pl.BlockSpec((tm, d), lambda i, f: (i, 0)),
            pl.BlockSpec((1, d), lambda i, f: (0, 0)),
            pl.BlockSpec((None, d, tf), lambda i, f: (layer, 0, f)),
            pl.BlockSpec((None, tf, d), lambda i, f: (layer, f, 0)),
            pl.BlockSpec((1, d), lambda i, f: (0, 0)),
        ],
        out_specs=pl.BlockSpec((tm, d), lambda i, f: (i, 0)),
        scratch_shapes=[pltpu.VMEM((tm, d), BF16)],
        compiler_params=_params("parallel", "arbitrary"),
        name="mlp_block",
    )(x, g, w_in, w_out, g_final)


def _t5_bias_kernel(table_ref, o_ref, *, t):
    h = pl.program_id(0)
    half = REL_BUCKETS // 2
    max_exact = half // 2
    ki = lax.broadcasted_iota(jnp.int32, (t, t), 0)
    qi = lax.broadcasted_iota(jnp.int32, (t, t), 1)
    far = table_ref[half - 1, h]
    for d in range(2):
        rel = ki - d * t - qi
        ret = jnp.where(rel > 0, half, 0)
        n = jnp.abs(rel)
        nf = jnp.maximum(n, 1).astype(F32)
        large = max_exact + (jnp.log(nf / max_exact) / math.log(REL_MAX_DIST / max_exact)
                             * (half - max_exact)).astype(jnp.int32)
        large = jnp.minimum(large, half - 1)
        bucket = ret + jnp.where(n < max_exact, n, large)
        val = jnp.zeros((t, t), F32)
        for b in range(REL_BUCKETS):
            val = jnp.where(bucket == b, table_ref[b, h], val)
        val = (val - far) * LOG2E
        if d == 0:
            mask = (ki >> CHUNK_SHIFT) <= (qi >> CHUNK_SHIFT)
            val = jnp.where(mask, val, NEG)
        o_ref[0, d] = val


def t5_bias_tiles(rel_table, *, t):
    nheads = rel_table.shape[1]
    assert t % CHUNK == 0 and t >= REL_MAX_DIST
    return pl.pallas_call(
        functools.partial(_t5_bias_kernel, t=t),
        out_shape=jax.ShapeDtypeStruct((nheads, 2, t, t), F32),
        grid=(nheads,),
        in_specs=[pl.BlockSpec(memory_space=pltpu.SMEM)],
        out_specs=pl.BlockSpec((1, 2, t, t), lambda h: (h, 0, 0, 0)),
        compiler_params=_params("arbitrary"),
        name="t5_bias_tiles",
    )(rel_table)


def _init_stats(m_ref, l_ref, acc_ref):
    m_ref[...] = jnp.full_like(m_ref, M_INIT)
    if l_ref is not None:
        l_ref[...] = jnp.zeros_like(l_ref)
    acc_ref[...] = jnp.zeros_like(acc_ref)


def _online_softmax_step(load_z, vt, m_ref, l_ref, acc_ref):
    m_prev = m_ref[...]
    m_new = jnp.maximum(m_prev, jnp.max(load_z(), axis=0, keepdims=True))
    p = jnp.exp2(load_z() - m_new)
    alpha = jnp.exp2(m_prev - m_new)
    if l_ref is not None:
        l_ref[...] = alpha * l_ref[...] + jnp.sum(p, axis=0, keepdims=True)
    acc_ref[...] = alpha * acc_ref[...] + _dot(vt, p.astype(BF16))
    m_ref[...] = m_new


def _far_loop(consume, j_start, j_end, unrolls, first_slot=0, singles=True):
    def body(u, base):
        def run(jj, carry):
            for r in range(u):
                slot = (first_slot + r) % 2 if u > 1 else first_slot
                next_slot = (first_slot + r + 1) % 2 if u > 1 else first_slot
                consume(base + jj * u + r, base + jj * u + r + 1, slot, next_slot)
            return carry
        return run

    base = j_start
    for u in tuple(unrolls) + ((1,) if singles else ()):
        assert u == 1 or u % 2 == 0
        groups = (j_end - base) // u
        lax.fori_loop(0, groups, body(u, base), 0)
        base = base + groups * u


def _diff_attn_kernel(q_ref, k_ref, vt_ref, bias_ref, lamv_ref, g_ref, o_ref,
                      m0, l0, a0, m1, l1, a1, z_ref, *, t, lambda_init):
    i = pl.program_id(2)
    stats = ((m0, l0, a0), (m1, l1, a1))
    for s in stats:
        _init_stats(*s)
    q = q_ref[0]

    def scores(j, slot):
        start = pl.multiple_of(j * t, t)
        k = k_ref[0, pl.ds(start, t), :]
        for c in range(2):
            half = slice(c * HEAD_DIM, (c + 1) * HEAD_DIM)
            z_ref[slot, c] = _dot_nt(k[:, half], q[:, half])

    def consume(j, bias_idx, j_next, slot, next_slot):
        if j_next is not None and next_slot == slot:
            held = [z_ref[slot, c] for c in range(2)]
            load = lambda c: held[c]
        else:
            load = lambda c: z_ref[slot, c]
        if j_next is not None:
            scores(j_next, next_slot)
        start = pl.multiple_of(j * t, t)
        vt = vt_ref[:, pl.ds(start, t)]
        for c, s in enumerate(stats):
            if bias_idx is None:
                load_z = functools.partial(load, c)
            else:
                load_z = lambda c=c: load(c) + bias_ref[0, bias_idx]
            _online_softmax_step(load_z, vt, *s)

    scores(0, 0)
    n_far = jnp.maximum(i - 1, 0)
    n_tail = n_far % DIFF_FAR_UNROLLS[-1]
    _far_loop(lambda j, j_next, slot, next_slot: consume(j, None, j_next, slot, next_slot),
              0, n_far - n_tail, DIFF_FAR_UNROLLS, singles=False)

    def tail(nt):
        def run():
            tiles = [(n_far - nt + r, None) for r in range(nt)] + [(i - 1, 1), (i, 0)]
            for r, (j, bias_idx) in enumerate(tiles):
                last = r == len(tiles) - 1
                consume(j, bias_idx, None if last else tiles[r + 1][0], r % 2, None if last else (r + 1) % 2)
        return run

    for nt in range(DIFF_FAR_UNROLLS[-1]):
        pl.when(jnp.logical_and(i >= 1, n_tail == nt))(tail(nt))

    @pl.when(i == 0)
    def _():
        consume(i, 0, None, 0, None)

    lamv = lamv_ref[...]
    lam = (jnp.exp(jnp.sum(lamv[0:1] * lamv[1:2], axis=1, keepdims=True))
           - jnp.exp(jnp.sum(lamv[2:3] * lamv[3:4], axis=1, keepdims=True)) + lambda_init)
    o = a0[...] * (1.0 / l0[...]) - lam * (a1[...] * (1.0 / l1[...]))
    ms = jnp.mean(o * o, axis=0, keepdims=True)
    o = o * lax.rsqrt(ms + SUBLN_EPS) * g_ref[...] * (1.0 - lambda_init)
    o_ref[0] = o.T.astype(o_ref.dtype)


def diff_attention(q, k, vt, bias_tiles, lamv, subln_g_col, *, nheads, t, lambda_init):
    b, s, d = q.shape
    hw = 2 * HEAD_DIM
    assert d == nheads * hw and s % t == 0
    kernel = functools.partial(_diff_attn_kernel, t=t, lambda_init=lambda_init)
    stat = lambda: pltpu.VMEM((1, t), F32)
    acc = lambda: pltpu.VMEM((hw, t), F32)
    return pl.pallas_call(
        kernel,
        out_shape=jax.ShapeDtypeStruct((b, s, d), BF16),
        grid=(b, nheads, s // t),
        in_specs=[
            pl.BlockSpec((1, t, hw), lambda bi, h, i: (bi, i, h)),
            pl.BlockSpec((1, s, hw), lambda bi, h, i: (bi, 0, h)),
            pl.BlockSpec((hw, s), lambda bi, h, i: (h, bi)),
            pl.BlockSpec((1, 2, t, t), lambda bi, h, i: (h, 0, 0, 0)),
            pl.BlockSpec((4, HEAD_DIM), lambda bi, h, i: (0, 0)),
            pl.BlockSpec((hw, 1), lambda bi, h, i: (0, 0)),
        ],
        out_specs=pl.BlockSpec((1, t, hw), lambda bi, h, i: (bi, i, h)),
        scratch_shapes=[stat(), stat(), acc(), stat(), stat(), acc(), pltpu.VMEM((2, 2, t, t), F32)],
        compiler_params=_params("parallel", "parallel", "arbitrary"),
        name="diff_attention",
    )(q, k, vt, bias_tiles, lamv, subln_g_col)


def _row_sq_norm_max(x):
    xf = x.astype(F32)
    return jnp.max(jnp.sum(xf * xf, axis=1, keepdims=True), axis=0, keepdims=True)


def _fox_attn_kernel(cend_ref, q_ref, k_ref, vt_ref, kc_ref, cq_ref, o_ref, m_ref, a_ref, z_ref, kn_ref,
                     *, t, nheads):
    hp = pl.program_id(1)
    i = pl.program_id(2)
    nk = pl.num_programs(2)
    heads = range(FOX_HEADS_PER_STEP)
    col = lambda e: slice(e * HEAD_DIM, (e + 1) * HEAD_DIM)
    tile_lane = lax.broadcasted_iota(jnp.int32, (1, kn_ref.shape[-1]), 1)
    for e in heads:
        _init_stats(m_ref.at[e], None, a_ref.at[e])

    @pl.when(i == 0)
    def _():
        def norm_body(j, carry):
            start = pl.multiple_of(j * t, t)
            k = k_ref[0, pl.ds(start, t), :]
            out = []
            for e, (running, kn) in zip(heads, carry):
                running = jnp.maximum(running, _row_sq_norm_max(k[:, col(e)]))
                out.append((running, jnp.where(tile_lane == j, running, kn)))
            return tuple(out)

        init = tuple((jnp.zeros((1, 1), F32), jnp.zeros(tile_lane.shape, F32)) for _ in heads)
        for e, (_, kn) in zip(heads, lax.fori_loop(0, nk, norm_body, init)):
            kn_ref[e] = kn

    cq = cq_ref[0]
    lane = lax.broadcasted_iota(jnp.int32, cq.shape, 1)
    npc = C_PIECES * nheads
    k_diag = k_ref[0, pl.ds(pl.multiple_of(i * t, t), t), :]
    qs, firsts = [], []
    for e in heads:
        h = hp * FOX_HEADS_PER_STEP + e
        q_e = q_ref[0][:, col(e)]
        ct = jnp.sum(jnp.where(lane == h, cq, 0.0), axis=1, keepdims=True) * LOG2E
        qc = jnp.where(lane < npc, jnp.where((lane & (nheads - 1)) == h, 1.0, 0.0), 0.0)
        for p, piece in enumerate(_split_bf16(ct, C_PIECES)):
            qc = jnp.where(lane == npc + p, piece, qc)
        qs.append(jnp.concatenate([q_e, qc.astype(BF16)], axis=1))

        q2 = _row_sq_norm_max(q_e)
        m_lo = jnp.min(jnp.sum(q_e.astype(F32) * k_diag[:, col(e)].astype(F32), axis=1, keepdims=True),
                       axis=0, keepdims=True)
        c_hi = jnp.max(ct, axis=0, keepdims=True)
        r = PRUNE_LOG2 + m_lo - c_hi + cend_ref[0, e:e + 1, :] * LOG2E
        needed = jnp.logical_or(r <= 0.0, q2 * kn_ref[e] >= r * r)
        first = jnp.where(jnp.logical_and(needed, tile_lane < i), tile_lane, i).astype(F32)
        firsts.append(jnp.min(first, axis=1, keepdims=True))
    j_start = functools.reduce(jnp.minimum, firsts)[0, 0].astype(jnp.int32)

    def scores(j, slot):
        start = pl.multiple_of(j * t, t)
        k = k_ref[0, pl.ds(start, t), :]
        kc = kc_ref[0, pl.ds(start, t), :]
        for e in heads:
            z_ref[slot, e] = _dot_nt(jnp.concatenate([k[:, col(e)], kc], axis=1), qs[e])

    def consume(j, diagonal, j_next, slot, next_slot):
        if j_next is not None and next_slot == slot:
            held = [z_ref[slot, e] for e in heads]
            load = lambda e: held[e]
        else:
            load = lambda e: z_ref[slot, e]
        if j_next is not None:
            scores(j_next, next_slot)
        start = pl.multiple_of(j * t, t)
        vt = vt_ref[:, pl.ds(start, t)]
        ones = jnp.ones((ONES_ROWS, t), BF16)
        for e in heads:
            if diagonal:
                def load_z(e=e):
                    ki = lax.broadcasted_iota(jnp.int32, (t, t), 0)
                    qi = lax.broadcasted_iota(jnp.int32, (t, t), 1)
                    return jnp.where(ki <= qi, load(e), NEG)
            else:
                load_z = functools.partial(load, e)
            _online_softmax_step(load_z, jnp.concatenate([vt[col(e)], ones], axis=0),
                                 m_ref.at[e], None, a_ref.at[e])

    scores(i, 0)
    n_far = i - j_start

    def straight_line(nf):
        def run():
            consume(i, True, j_start if nf else None, 0, 1 if nf else None)
            for r in range(nf):
                last = r == nf - 1
                consume(j_start + r, False, None if last else j_start + r + 1,
                        (1 + r) % 2, None if last else r % 2)
        return run

    for nf in range(FOX_STRAIGHT_LINE_TILES + 1):
        pl.when(n_far == nf)(straight_line(nf))

    @pl.when(n_far > FOX_STRAIGHT_LINE_TILES)
    def _():
        consume(i, True, j_start, 0, 1)
        _far_loop(lambda j, j_next, slot, next_slot: consume(j, False, j_next, slot, next_slot),
                  j_start, i - 1, FOX_FAR_UNROLLS, first_slot=1)
        consume(i - 1, False, None, 1, None)

    o = [a_ref[e, :HEAD_DIM] * (1.0 / a_ref[e, HEAD_DIM:HEAD_DIM + 1]) for e in heads]
    o_ref[0] = jnp.concatenate([x.T for x in o], axis=1).astype(o_ref.dtype)


def fox_attention(q, k, vt, kc, c_cols, c_end, *, nheads, t):
    b, s, d = q.shape
    g = FOX_HEADS_PER_STEP
    gw = g * HEAD_DIM
    ng = nheads // g
    assert d == nheads * HEAD_DIM and nheads % g == 0 and s % t == 0 and c_end.shape == (b * ng, g, s // t)
    assert nheads & (nheads - 1) == 0 and C_PIECES * nheads + C_PIECES <= LANES
    return pl.pallas_call(
        functools.partial(_fox_attn_kernel, t=t, nheads=nheads),
        out_shape=jax.ShapeDtypeStruct((b, s, d), BF16),
        grid=(b, nheads // g, s // t),
        in_specs=[
            pl.BlockSpec((1, g, s // t), lambda bi, hp, i: (bi * ng + hp, 0, 0)),
            pl.BlockSpec((1, t, gw), lambda bi, hp, i: (bi, i, hp)),
            pl.BlockSpec((1, s, gw), lambda bi, hp, i: (bi, 0, hp)),
            pl.BlockSpec((gw, s), lambda bi, hp, i: (hp, bi)),
            pl.BlockSpec((1, s, LANES), lambda bi, hp, i: (bi, 0, 0)),
            pl.BlockSpec((1, t, LANES), lambda bi, hp, i: (bi, i, 0)),
        ],
        out_specs=pl.BlockSpec((1, t, gw), lambda bi, hp, i: (bi, i, hp)),
        scratch_shapes=[pltpu.VMEM((g, 1, t), F32), pltpu.VMEM((g, HEAD_DIM + ONES_ROWS, t), F32),
                        pltpu.VMEM((2, g, t, t), F32), pltpu.VMEM((g, 1, s // t), F32)],
        compiler_params=_params("parallel", "parallel", "arbitrary"),
        name="fox_attention",
    )(c_end, q, k, vt, kc, c_cols)


def _forget_scan_kernel(x_ref, g_ref, wt_ref, b_ref, kc_ref, ccol_ref, carry_ref, *, ts, nheads):
    @pl.when(pl.program_id(1) == 0)
    def _():
        carry_ref[...] = jnp.zeros_like(carry_ref)

    x = x_ref[0]
    hn = x * _rms_scale(x, NORM_EPS) * g_ref[...]
    hn_hi, hn_lo = (p.astype(BF16) for p in _split_bf16(hn, 2))
    w_hi, w_lo = (p.astype(BF16) for p in _split_bf16(wt_ref[...], 2))
    f = _dot_nt(w_hi, hn_hi) + _dot_nt(w_hi, hn_lo) + _dot_nt(w_lo, hn_hi) + b_ref[...]
    logf = -(jnp.maximum(-f, 0.0) + jnp.log1p(jnp.exp(-jnp.abs(f))))
    row = lax.broadcasted_iota(jnp.int32, (ts, ts), 0)
    col = lax.broadcasted_iota(jnp.int32, (ts, ts), 1)
    upper = jnp.where(row <= col, 1.0, 0.0).astype(BF16)
    c = carry_ref[:, 0:1]
    for piece in _split_bf16(logf, C_PIECES):
        c = c + _dot(piece.astype(BF16), upper)
    carry_ref[...] = jnp.broadcast_to(c[:, ts - 1:ts], carry_ref.shape)
    ccol_ref[0] = c.T

    hrow = lax.broadcasted_iota(jnp.int32, c.shape, 0)
    npc = C_PIECES * nheads
    aug = jnp.where(hrow < npc + C_PIECES, 1.0, 0.0)
    for p, piece in reversed(list(enumerate(_split_bf16(c * (-LOG2E), C_PIECES)))):
        aug = jnp.where(hrow < (p + 1) * nheads, piece, aug)
    kc_ref[0] = aug.T.astype(BF16)


def forget_scan(h, g, w_f_t, b_f, *, nheads, ts):
    b, s, d = h.shape
    kernel = functools.partial(_forget_scan_kernel, ts=ts, nheads=nheads)
    return pl.pallas_call(
        kernel,
        out_shape=(jax.ShapeDtypeStruct((b, s, LANES), BF16), jax.ShapeDtypeStruct((b, s, LANES), F32)),
        grid=(b, s // ts),
        in_specs=[
            pl.BlockSpec((1, ts, d), lambda bi, i: (bi, i, 0)),
            pl.BlockSpec((1, d), lambda bi, i: (0, 0)),
            pl.BlockSpec((LANES, d), lambda bi, i: (0, 0)),
            pl.BlockSpec((LANES, 1), lambda bi, i: (0, 0)),
        ],
        out_specs=(pl.BlockSpec((1, ts, LANES), lambda bi, i: (bi, i, 0)),
                   pl.BlockSpec((1, ts, LANES), lambda bi, i: (bi, i, 0))),
        scratch_shapes=[pltpu.VMEM((LANES, LANES), F32)],
        compiler_params=_params("parallel", "arbitrary"),
        name="forget_scan",
    )(h, g, w_f_t, b_f)


def _pick(n, target):
    t = min(n, target)
    while n % t:
        t //= 2
    return t


def kernel(x, rel_bias_table, attn_norm_g, mlp_norm_g, w_qkv_a, lam_q1, lam_k1, lam_q2, lam_k2, subln_g,
           w_o_a, kv_norm_g, w_k_b, w_v_b, w_f_b, b_f_b, w_q_b, w_o_b, w_mlp_in, w_mlp_out, final_norm_g):
    b, s, d = x.shape
    n = b * s
    depth = attn_norm_g.shape[0]
    assert depth == 2 and w_qkv_a.shape[0] == 1 and w_q_b.shape[0] == 1
    diff_heads = d // (2 * HEAD_DIM)
    fox_heads = d // HEAD_DIM
    dff = w_mlp_in.shape[2]

    t_attn = _pick(s, 512)
    tm = _pick(n, 1024)
    tf = _pick(dff, 512)
    ts = _pick(s, 512)

    row = lambda v: v.reshape(1, -1).astype(F32)
    x2 = x.reshape(n, d)

    bias_tiles = t5_bias_tiles(rel_bias_table.astype(F32), t=t_attn)
    g0 = row(attn_norm_g[0])
    w_qkv = w_qkv_a[0].astype(BF16)
    w_mlp_in_bf, w_mlp_out_bf = w_mlp_in.astype(BF16), w_mlp_out.astype(BF16)
    q = norm_matmul(x2, g0, w_qkv, tm=tm, out_scale=Q_SCALE, col_block=0, nout=d)
    k = norm_matmul(x2, g0, w_qkv, tm=tm, col_block=1, nout=d)
    vt = norm_matmul(x2, g0, w_qkv[:, 2 * d:].T, tm=tm, transpose_out=True)
    lamv = jnp.stack([lam_q1[0], lam_k1[0], lam_q2[0], lam_k2[0]]).astype(F32)
    lambda_init = 0.8 - 0.6 * math.exp(-0.3 * 0)
    o = diff_attention(q.reshape(b, s, d), k.reshape(b, s, d), vt, bias_tiles, lamv,
                       subln_g[0].reshape(-1, 1).astype(F32), nheads=diff_heads, t=t_attn, lambda_init=lambda_init)
    h = matmul_residual(o.reshape(n, d), w_o_a[0].astype(BF16), x2, tm=tm)
    h = mlp_block(h, row(mlp_norm_g[0]), w_mlp_in_bf, w_mlp_out_bf, row(final_norm_g),
                  layer=0, tm=tm, tf=tf, final_norm=False)

    gkv = row(kv_norm_g)
    w_f_t = jnp.zeros((LANES, d), F32)
    b_f = jnp.zeros((LANES, 1), F32)
    for p in range(C_PIECES):
        w_f_t = w_f_t.at[p * fox_heads:(p + 1) * fox_heads].set(w_f_b.T.astype(F32))
        b_f = b_f.at[p * fox_heads:(p + 1) * fox_heads, 0].set(b_f_b.astype(F32))
    kc, c_cols = forget_scan(h.reshape(b, s, d), gkv, w_f_t, b_f, nheads=fox_heads, ts=ts)
    k = norm_matmul(h, gkv, w_k_b.astype(BF16), tm=tm)
    vt = norm_matmul(h, gkv, w_v_b.astype(BF16).T, tm=tm, transpose_out=True)

    q = norm_matmul(h, row(attn_norm_g[1]), w_q_b[0].astype(BF16), tm=tm, out_scale=Q_SCALE)
    c_end = c_cols[:, t_attn - 1::t_attn, :fox_heads].transpose(0, 2, 1).reshape(-1, FOX_HEADS_PER_STEP, s // t_attn)
    o = fox_attention(q.reshape(b, s, d), k.reshape(b, s, d), vt, kc, c_cols, c_end, nheads=fox_heads, t=t_attn)
    h = matmul_residual(o.reshape(n, d), w_o_b[0].astype(BF16), h, tm=tm)
    h = mlp_block(h, row(mlp_norm_g[1]), w_mlp_in_bf, w_mlp_out_bf, row(final_norm_g),
                  layer=1, tm=tm, tf=tf, final_norm=True)
    return h.reshape(b, s, d)
```

```python
import functools
import math

import jax
import jax.numpy as jnp
from jax import lax
from jax.experimental import pallas as pl
from jax.experimental.pallas import tpu as pltpu

HEAD_DIM = 128
CHUNK = 64
REL_BUCKETS = 32
REL_MAX_DIST = 128
NORM_EPS = 1e-6
SUBLN_EPS = 1e-5
NEG = -1e30
SCALE = HEAD_DIM ** -0.5
LOG2E = math.log2(math.e)
Q_SCALE = SCALE * LOG2E
M_INIT = -1e30
DIFF_FAR_UNROLLS = (16, 8, 4)
FOX_FAR_UNROLLS = (2,)
FOX_STRAIGHT_LINE_TILES = 3
ONES_ROWS = 16
PRUNE_LOG2 = -170.0
FOX_HEADS_PER_STEP = 4

V7X_VMEM_BYTES = 64 * 1024 * 1024
V7X_VMEM_RESERVE_BYTES = 8 * 1024 * 1024
V7X_VMEM_LIMIT_BYTES = V7X_VMEM_BYTES - V7X_VMEM_RESERVE_BYTES
LANES = 128
CHUNK_SHIFT = CHUNK.bit_length() - 1
assert CHUNK == 1 << CHUNK_SHIFT
C_PIECES = 3

F32 = jnp.float32
BF16 = jnp.bfloat16


def _params(*semantics):
    return pltpu.CompilerParams(dimension_semantics=semantics, vmem_limit_bytes=V7X_VMEM_LIMIT_BYTES)


def _rms_scale(x, eps):
    return lax.rsqrt(jnp.mean(x * x, axis=-1, keepdims=True) + eps)


def _dot(a, b):
    return jnp.dot(a, b, preferred_element_type=F32)


def _dot_nt(a, b):
    return lax.dot_general(a, b, (((1,), (1,)), ((), ())), preferred_element_type=F32)


def _split_bf16(x, terms):
    pieces = []
    for _ in range(terms - 1):
        p = x.astype(BF16).astype(x.dtype)
        pieces.append(p)
        x = x - p
    pieces.append(x.astype(BF16).astype(x.dtype))
    return pieces


def _norm_matmul_kernel(x_ref, g_ref, w_ref, o_ref, *, transpose_out, out_scale):
    x = x_ref[...]
    hn = (x * _rms_scale(x, NORM_EPS) * g_ref[...]).astype(BF16)
    if transpose_out:
        y = _dot_nt(w_ref[...], hn)
    else:
        y = _dot(hn, w_ref[...])
    if out_scale != 1.0:
        y = y * out_scale
    o_ref[...] = y.astype(o_ref.dtype)


def norm_matmul(x, g, w, *, tm, transpose_out=False, out_scale=1.0, col_block=0, nout=None):
    n, d = x.shape
    if transpose_out:
        assert col_block == 0 and nout is None
        nout = w.shape[0]
        w_spec = pl.BlockSpec((nout, d), lambda i: (0, 0))
        out_shape = jax.ShapeDtypeStruct((nout, n), BF16)
        out_spec = pl.BlockSpec((nout, tm), lambda i: (0, i))
    else:
        nout = w.shape[1] if nout is None else nout
        w_spec = pl.BlockSpec((d, nout), lambda i: (0, col_block))
        out_shape = jax.ShapeDtypeStruct((n, nout), BF16)
        out_spec = pl.BlockSpec((tm, nout), lambda i: (i, 0))
    return pl.pallas_call(
        functools.partial(_norm_matmul_kernel, transpose_out=transpose_out, out_scale=out_scale),
        out_shape=out_shape,
        grid=(n // tm,),
        in_specs=[
            pl.BlockSpec((tm, d), lambda i: (i, 0)),
            pl.BlockSpec((1, d), lambda i: (0, 0)),
            w_spec,
        ],
        out_specs=out_spec,
        compiler_params=_params("parallel"),
        name="norm_matmul_t" if transpose_out else "norm_matmul",
    )(x, g, w)


def _matmul_residual_kernel(a_ref, w_ref, r_ref, o_ref):
    o_ref[...] = r_ref[...] + _dot(a_ref[...], w_ref[...])


def matmul_residual(a, w, res, *, tm):
    n, k = a.shape
    nout = w.shape[1]
    return pl.pallas_call(
        _matmul_residual_kernel,
        out_shape=jax.ShapeDtypeStruct((n, nout), F32),
        grid=(n // tm,),
        in_specs=[
            pl.BlockSpec((tm, k), lambda i: (i, 0)),
            pl.BlockSpec((k, nout), lambda i: (0, 0)),
            pl.BlockSpec((tm, nout), lambda i: (i, 0)),
        ],
        out_specs=pl.BlockSpec((tm, nout), lambda i: (i, 0)),
        compiler_params=_params("parallel"),
        name="matmul_residual",
    )(a, w, res)


def _mlp_kernel(x_ref, g_ref, win_ref, wout_ref, gf_ref, o_ref, hn_ref, *, final_norm):
    f = pl.program_id(1)

    @pl.when(f == 0)
    def _():
        x = x_ref[...]
        hn_ref[...] = (x * _rms_scale(x, NORM_EPS) * g_ref[...]).astype(hn_ref.dtype)
        o_ref[...] = x

    u = _dot(hn_ref[...], win_ref[...])
    u = jnp.square(jnp.maximum(u, 0.0)).astype(BF16)
    o_ref[...] += _dot(u, wout_ref[...])

    if final_norm:
        @pl.when(f == pl.num_programs(1) - 1)
        def _():
            y = o_ref[...]
            o_ref[...] = y * _rms_scale(y, NORM_EPS) * gf_ref[...]


def mlp_block(x, g, w_in, w_out, g_final, *, layer, tm, tf, final_norm):
    n, d = x.shape
    dff = w_in.shape[2]
    return pl.pallas_call(
        functools.partial(_mlp_kernel, final_norm=final_norm),
        out_shape=jax.ShapeDtypeStruct((n, d), F32),
        grid=(n // tm, dff // tf),
        in_specs=[
            pl.BlockSpec((tm, d), lambda i, f: (i, 0)),
            pl.BlockSpec((1, d), lambda i, f: (0, 0)),
            pl.BlockSpec((None, d, tf), lambda i, f: (layer, 0, f)),
            pl.BlockSpec((None, tf, d), lambda i, f: (layer, f, 0)),
            pl.BlockSpec((1, d), lambda i, f: (0, 0)),
        ],
        out_specs=pl.BlockSpec((tm, d), lambda i, f: (i, 0)),
        scratch_shapes=[pltpu.VMEM((tm, d), BF16)],
        compiler_params=_params("parallel", "arbitrary"),
        name="mlp_block",
    )(x, g, w_in, w_out, g_final)


def _t5_bias_kernel(table_ref, o_ref, *, t):
    h = pl.program_id(0)
    half = REL_BUCKETS // 2
    max_exact = half // 2
    ki = lax.broadcasted_iota(jnp.int32, (t, t), 0)
    qi = lax.broadcasted_iota(jnp.int32, (t, t), 1)
    far = table_ref[half - 1, h]
    for d in range(2):
        rel = ki - d * t - qi
        ret = jnp.where(rel > 0, half, 0)
        n = jnp.abs(rel)
        nf = jnp.maximum(n, 1).astype(F32)
        large = max_exact + (jnp.log(nf / max_exact) / math.log(REL_MAX_DIST / max_exact)
                             * (half - max_exact)).astype(jnp.int32)
        large = jnp.minimum(large, half - 1)
        bucket = ret + jnp.where(n < max_exact, n, large)
        val = jnp.zeros((t, t), F32)
        for b in range(REL_BUCKETS):
            val = jnp.where(bucket == b, table_ref[b, h], val)
        val = (val - far) * LOG2E
        if d == 0:
            mask = (ki >> CHUNK_SHIFT) <= (qi >> CHUNK_SHIFT)
            val = jnp.where(mask, val, NEG)
        o_ref[0, d] = val


def t5_bias_tiles(rel_table, *, t):
    nheads = rel_table.shape[1]
    assert t % CHUNK == 0 and t >= REL_MAX_DIST
    return pl.pallas_call(
        functools.partial(_t5_bias_kernel, t=t),
        out_shape=jax.ShapeDtypeStruct((nheads, 2, t, t), F32),
        grid=(nheads,),
        in_specs=[pl.BlockSpec(memory_space=pltpu.SMEM)],
        out_specs=pl.BlockSpec((1, 2, t, t), lambda h: (h, 0, 0, 0)),
        compiler_params=_params("arbitrary"),
        name="t5_bias_tiles",
    )(rel_table)


def _init_stats(m_ref, l_ref, acc_ref):
    m_ref[...] = jnp.full_like(m_ref, M_INIT)
    if l_ref is not None:
        l_ref[...] = jnp.zeros_like(l_ref)
    acc_ref[...] = jnp.zeros_like(acc_ref)


def _online_softmax_step(load_z, vt, m_ref, l_ref, acc_ref):
    m_prev = m_ref[...]
    m_new = jnp.maximum(m_prev, jnp.max(load_z(), axis=0, keepdims=True))
    p = jnp.exp2(load_z() - m_new)
    alpha = jnp.exp2(m_prev - m_new)
    if l_ref is not None:
        l_ref[...] = alpha * l_ref[...] + jnp.sum(p, axis=0, keepdims=True)
    acc_ref[...] = alpha * acc_ref[...] + _dot(vt, p.astype(BF16))
    m_ref[...] = m_new


def _far_loop(consume, j_start, j_end, unrolls, first_slot=0, singles=True):
    def body(u, base):
        def run(jj, carry):
            for r in range(u):
                slot = (first_slot + r) % 2 if u > 1 else first_slot
                next_slot = (first_slot + r + 1) % 2 if u > 1 else first_slot
                consume(base + jj * u + r, base + jj * u + r + 1, slot, next_slot)
            return carry
        return run

    base = j_start
    for u in tuple(unrolls) + ((1,) if singles else ()):
        assert u == 1 or u % 2 == 0
        groups = (j_end - base) // u
        lax.fori_loop(0, groups, body(u, base), 0)
        base = base + groups * u


def _diff_attn_kernel(q_ref, k_ref, vt_ref, bias_ref, lamv_ref, g_ref, o_ref,
                      m0, l0, a0, m1, l1, a1, z_ref, *, t, lambda_init):
    i = pl.program_id(2)
    stats = ((m0, l0, a0), (m1, l1, a1))
    for s in stats:
        _init_stats(*s)
    q = q_ref[0]

    def scores(j, slot):
        start = pl.multiple_of(j * t, t)
        k = k_ref[0, pl.ds(start, t), :]
        for c in range(2):
            half = slice(c * HEAD_DIM, (c + 1) * HEAD_DIM)
            z_ref[slot, c] = _dot_nt(k[:, half], q[:, half])

    def consume(j, bias_idx, j_next, slot, next_slot):
        if j_next is not None and next_slot == slot:
            held = [z_ref[slot, c] for c in range(2)]
            load = lambda c: held[c]
        else:
            load = lambda c: z_ref[slot, c]
        if j_next is not None:
            scores(j_next, next_slot)
        start = pl.multiple_of(j * t, t)
        vt = vt_ref[:, pl.ds(start, t)]
        for c, s in enumerate(stats):
            if bias_idx is None:
                load_z = functools.partial(load, c)
            else:
                load_z = lambda c=c: load(c) + bias_ref[0, bias_idx]
            _online_softmax_step(load_z, vt, *s)

    scores(0, 0)
    n_far = jnp.maximum(i - 1, 0)
    n_tail = n_far % DIFF_FAR_UNROLLS[-1]
    _far_loop(lambda j, j_next, slot, next_slot: consume(j, None, j_next, slot, next_slot),
              0, n_far - n_tail, DIFF_FAR_UNROLLS, singles=False)

    def tail(nt):
        def run():
            tiles = [(n_far - nt + r, None) for r in range(nt)] + [(i - 1, 1), (i, 0)]
            for r, (j, bias_idx) in enumerate(tiles):
                last = r == len(tiles) - 1
                consume(j, bias_idx, None if last else tiles[r + 1][0], r % 2, None if last else (r + 1) % 2)
        return run

    for nt in range(DIFF_FAR_UNROLLS[-1]):
        pl.when(jnp.logical_and(i >= 1, n_tail == nt))(tail(nt))

    @pl.when(i == 0)
    def _():
        consume(i, 0, None, 0, None)

    lamv = lamv_ref[...]
    lam = (jnp.exp(jnp.sum(lamv[0:1] * lamv[1:2], axis=1, keepdims=True))
           - jnp.exp(jnp.sum(lamv[2:3] * lamv[3:4], axis=1, keepdims=True)) + lambda_init)
    o = a0[...] * (1.0 / l0[...]) - lam * (a1[...] * (1.0 / l1[...]))
    ms = jnp.mean(o * o, axis=0, keepdims=True)
    o = o * lax.rsqrt(ms + SUBLN_EPS) * g_ref[...] * (1.0 - lambda_init)
    o_ref[0] = o.T.astype(o_ref.dtype)


def diff_attention(q, k, vt, bias_tiles, lamv, subln_g_col, *, nheads, t, lambda_init):
    b, s, d = q.shape
    hw = 2 * HEAD_DIM
    assert d == nheads * hw and s % t == 0
    kernel = functools.partial(_diff_attn_kernel, t=t, lambda_init=lambda_init)
    stat = lambda: pltpu.VMEM((1, t), F32)
    acc = lambda: pltpu.VMEM((hw, t), F32)
    return pl.pallas_call(
        kernel,
        out_shape=jax.ShapeDtypeStruct((b, s, d), BF16),
        grid=(b, nheads, s // t),
        in_specs=[
            pl.BlockSpec((1, t, hw), lambda bi, h, i: (bi, i, h)),
            pl.BlockSpec((1, s, hw), lambda bi, h, i: (bi, 0, h)),
            pl.BlockSpec((hw, s), lambda bi, h, i: (h, bi)),
            pl.BlockSpec((1, 2, t, t), lambda bi, h, i: (h, 0, 0, 0)),
            pl.BlockSpec((4, HEAD_DIM), lambda bi, h, i: (0, 0)),
            pl.BlockSpec((hw, 1), lambda bi, h, i: (0, 0)),
        ],
        out_specs=pl.BlockSpec((1, t, hw), lambda bi, h, i: (bi, i, h)),
        scratch_shapes=[stat(), stat(), acc(), stat(), stat(), acc(), pltpu.VMEM((2, 2, t, t), F32)],
        compiler_params=_params("parallel", "parallel", "arbitrary"),
        name="diff_attention",
    )(q, k, vt, bias_tiles, lamv, subln_g_col)


def _row_sq_norm_max(x):
    xf = x.astype(F32)
    return jnp.max(jnp.sum(xf * xf, axis=1, keepdims=True), axis=0, keepdims=True)


def _fox_attn_kernel(cend_ref, q_ref, k_ref, vt_ref, kc_ref, cq_ref, o_ref, m_ref, a_ref, z_ref, kn_ref,
                     *, t, nheads):
    hp = pl.program_id(1)
    i = pl.program_id(2)
    nk = pl.num_programs(2)
    heads = range(FOX_HEADS_PER_STEP)
    col = lambda e: slice(e * HEAD_DIM, (e + 1) * HEAD_DIM)
    tile_lane = lax.broadcasted_iota(jnp.int32, (1, kn_ref.shape[-1]), 1)
    for e in heads:
        _init_stats(m_ref.at[e], None, a_ref.at[e])

    @pl.when(i == 0)
    def _():
        def norm_body(j, carry):
            start = pl.multiple_of(j * t, t)
            k = k_ref[0, pl.ds(start, t), :]
            out = []
            for e, (running, kn) in zip(heads, carry):
                running = jnp.maximum(running, _row_sq_norm_max(k[:, col(e)]))
                out.append((running, jnp.where(tile_lane == j, running, kn)))
            return tuple(out)

        init = tuple((jnp.zeros((1, 1), F32), jnp.zeros(tile_lane.shape, F32)) for _ in heads)
        for e, (_, kn) in zip(heads, lax.fori_loop(0, nk, norm_body, init)):
            kn_ref[e] = kn

    cq = cq_ref[0]
    lane = lax.broadcasted_iota(jnp.int32, cq.shape, 1)
    npc = C_PIECES * nheads
    k_diag = k_ref[0, pl.ds(pl.multiple_of(i * t, t), t), :]
    qs, firsts = [], []
    for e in heads:
        h = hp * FOX_HEADS_PER_STEP + e
        q_e = q_ref[0][:, col(e)]
        ct = jnp.sum(jnp.where(lane == h, cq, 0.0), axis=1, keepdims=True) * LOG2E
        qc = jnp.where(lane < npc, jnp.where((lane & (nheads - 1)) == h, 1.0, 0.0), 0.0)
        for p, piece in enumerate(_split_bf16(ct, C_PIECES)):
            qc = jnp.where(lane == npc + p, piece, qc)
        qs.append(jnp.concatenate([q_e, qc.astype(BF16)], axis=1))

        q2 = _row_sq_norm_max(q_e)
        m_lo = jnp.min(jnp.sum(q_e.astype(F32) * k_diag[:, col(e)].astype(F32), axis=1, keepdims=True),
                       axis=0, keepdims=True)
        c_hi = jnp.max(ct, axis=0, keepdims=True)
        r = PRUNE_LOG2 + m_lo - c_hi + cend_ref[0, e:e + 1, :] * LOG2E
        needed = jnp.logical_or(r <= 0.0, q2 * kn_ref[e] >= r * r)
        first = jnp.where(jnp.logical_and(needed, tile_lane < i), tile_lane, i).astype(F32)
        firsts.append(jnp.min(first, axis=1, keepdims=True))
    j_start = functools.reduce(jnp.minimum, firsts)[0, 0].astype(jnp.int32)

    def scores(j, slot):
        start = pl.multiple_of(j * t, t)
        k = k_ref[0, pl.ds(start, t), :]
        kc = kc_ref[0, pl.ds(start, t), :]
        for e in heads:
            z_ref[slot, e] = _dot_nt(jnp.concatenate([k[:, col(e)], kc], axis=1), qs[e])

    def consume(j, diagonal, j_next, slot, next_slot):
        if j_next is not None and next_slot == slot:
            held = [z_ref[slot, e] for e in heads]
            load = lambda e: held[e]
        else:
            load = lambda e: z_ref[slot, e]
        if j_next is not None:
            scores(j_next, next_slot)
        start = pl.multiple_of(j * t, t)
        vt = vt_ref[:, pl.ds(start, t)]
        ones = jnp.ones((ONES_ROWS, t), BF16)
        for e in heads:
            if diagonal:
                def load_z(e=e):
                    ki = lax.broadcasted_iota(jnp.int32, (t, t), 0)
                    qi = lax.broadcasted_iota(jnp.int32, (t, t), 1)
                    return jnp.where(ki <= qi, load(e), NEG)
            else:
                load_z = functools.partial(load, e)
            _online_softmax_step(load_z, jnp.concatenate([vt[col(e)], ones], axis=0),
                                 m_ref.at[e], None, a_ref.at[e])

    scores(i, 0)
    n_far = i - j_start

    def straight_line(nf):
        def run():
            consume(i, True, j_start if nf else None, 0, 1 if nf else None)
            for r in range(nf):
                last = r == nf - 1
                consume(j_start + r, False, None if last else j_start + r + 1,
                        (1 + r) % 2, None if last else r % 2)
        return run

    for nf in range(FOX_STRAIGHT_LINE_TILES + 1):
        pl.when(n_far == nf)(straight_line(nf))

    @pl.when(n_far > FOX_STRAIGHT_LINE_TILES)
    def _():
        consume(i, True, j_start, 0, 1)
        _far_loop(lambda j, j_next, slot, next_slot: consume(j, False, j_next, slot, next_slot),
                  j_start, i - 1, FOX_FAR_UNROLLS, first_slot=1)
        consume(i - 1, False, None, 1, None)

    o = [a_ref[e, :HEAD_DIM] * (1.0 / a_ref[e, HEAD_DIM:HEAD_DIM + 1]) for e in heads]
    o_ref[0] = jnp.concatenate([x.T for x in o], axis=1).astype(o_ref.dtype)


def fox_attention(q, k, vt, kc, c_cols, c_end, *, nheads, t):
    b, s, d = q.shape
    g = FOX_HEADS_PER_STEP
    gw = g * HEAD_DIM
    ng = nheads // g
    assert d == nheads * HEAD_DIM and nheads % g == 0 and s % t == 0 and c_end.shape == (b * ng, g, s // t)
    assert nheads & (nheads - 1) == 0 and C_PIECES * nheads + C_PIECES <= LANES
    return pl.pallas_call(
        functools.partial(_fox_attn_kernel, t=t, nheads=nheads),
        out_shape=jax.ShapeDtypeStruct((b, s, d), BF16),
        grid=(b, nheads // g, s // t),
        in_specs=[
            pl.BlockSpec((1, g, s // t), lambda bi, hp, i: (bi * ng + hp, 0, 0)),
            pl.BlockSpec((1, t, gw), lambda bi, hp, i: (bi, i, hp)),
            pl.BlockSpec((1, s, gw), lambda bi, hp, i: (bi, 0, hp), pipeline_mode=pl.Buffered(1)),
            pl.BlockSpec((gw, s), lambda bi, hp, i: (hp, bi), pipeline_mode=pl.Buffered(1)),
            pl.BlockSpec((1, s, LANES), lambda bi, hp, i: (bi, 0, 0), pipeline_mode=pl.Buffered(1)),
            pl.BlockSpec((1, t, LANES), lambda bi, hp, i: (bi, i, 0)),
        ],
        out_specs=pl.BlockSpec((1, t, gw), lambda bi, hp, i: (bi, i, hp)),
        scratch_shapes=[pltpu.VMEM((g, 1, t), F32), pltpu.VMEM((g, HEAD_DIM + ONES_ROWS, t), F32),
                        pltpu.VMEM((2, g, t, t), F32), pltpu.VMEM((g, 1, s // t), F32)],
        compiler_params=_params("parallel", "parallel", "arbitrary"),
        name="fox_attention",
    )(c_end, q, k, vt, kc, c_cols)


def _forget_scan_kernel(x_ref, g_ref, wt_ref, b_ref, kc_ref, ccol_ref, carry_ref, *, ts, nheads):
    @pl.when(pl.program_id(1) == 0)
    def _():
        carry_ref[...] = jnp.zeros_like(carry_ref)

    x = x_ref[0]
    hn = x * _rms_scale(x, NORM_EPS) * g_ref[...]
    hn_hi, hn_lo = (p.astype(BF16) for p in _split_bf16(hn, 2))
    w_hi, w_lo = (p.astype(BF16) for p in _split_bf16(wt_ref[...], 2))
    f = _dot_nt(w_hi, hn_hi) + _dot_nt(w_hi, hn_lo) + _dot_nt(w_lo, hn_hi) + b_ref[...]
    logf = -(jnp.maximum(-f, 0.0) + jnp.log1p(jnp.exp(-jnp.abs(f))))
    row = lax.broadcasted_iota(jnp.int32, (ts, ts), 0)
    col = lax.broadcasted_iota(jnp.int32, (ts, ts), 1)
    upper = jnp.where(row <= col, 1.0, 0.0).astype(BF16)
    c = carry_ref[:, 0:1]
    for piece in _split_bf16(logf, C_PIECES):
        c = c + _dot(piece.astype(BF16), upper)
    carry_ref[...] = jnp.broadcast_to(c[:, ts - 1:ts], carry_ref.shape)
    ccol_ref[0] = c.T

    hrow = lax.broadcasted_iota(jnp.int32, c.shape, 0)
    npc = C_PIECES * nheads
    aug = jnp.where(hrow < npc + C_PIECES, 1.0, 0.0)
    for p, piece in reversed(list(enumerate(_split_bf16(c * (-LOG2E), C_PIECES)))):
        aug = jnp.where(hrow < (p + 1) * nheads, piece, aug)
    kc_ref[0] = aug.T.astype(BF16)


def forget_scan(h, g, w_f_t, b_f, *, nheads, ts):
    b, s, d = h.shape
    kernel = functools.partial(_forget_scan_kernel, ts=ts, nheads=nheads)
    return pl.pallas_call(
        kernel,
        out_shape=(jax.ShapeDtypeStruct((b, s, LANES), BF16), jax.ShapeDtypeStruct((b, s, LANES), F32)),
        grid=(b, s // ts),
        in_specs=[
            pl.BlockSpec((1, ts, d), lambda bi, i: (bi, i, 0)),
            pl.BlockSpec((1, d), lambda bi, i: (0, 0)),
            pl.BlockSpec((LANES, d), lambda bi, i: (0, 0)),
            pl.BlockSpec((LANES, 1), lambda bi, i: (0, 0)),
        ],
        out_specs=(pl.BlockSpec((1, ts, LANES), lambda bi, i: (bi, i, 0)),
                   pl.BlockSpec((1, ts, LANES), lambda bi, i: (bi, i, 0))),
        scratch_shapes=[pltpu.VMEM((LANES, LANES), F32)],
        compiler_params=_params("parallel", "arbitrary"),
        name="forget_scan",
    )(h, g, w_f_t, b_f)


def _pick(n, target):
    t = min(n, target)
    while n % t:
        t //= 2
    return t


def kernel(x, rel_bias_table, attn_norm_g, mlp_norm_g, w_qkv_a, lam_q1, lam_k1, lam_q2, lam_k2, subln_g,
           w_o_a, kv_norm_g, w_k_b, w_v_b, w_f_b, b_f_b, w_q_b, w_o_b, w_mlp_in, w_mlp_out, final_norm_g):
    b, s, d = x.shape
    n = b * s
    depth = attn_norm_g.shape[0]
    assert depth == 2 and w_qkv_a.shape[0] == 1 and w_q_b.shape[0] == 1
    diff_heads = d // (2 * HEAD_DIM)
    fox_heads = d // HEAD_DIM
    dff = w_mlp_in.shape[2]

    t_attn = _pick(s, 512)
    tm = _pick(n, 1024)
    tf = _pick(dff, 512)
    ts = _pick(s, 512)

    row = lambda v: v.reshape(1, -1).astype(F32)
    x2 = x.reshape(n, d)

    bias_tiles = t5_bias_tiles(rel_bias_table.astype(F32), t=t_attn)
    g0 = row(attn_norm_g[0])
    w_qkv = w_qkv_a[0].astype(BF16)
    w_mlp_in_bf, w_mlp_out_bf = w_mlp_in.astype(BF16), w_mlp_out.astype(BF16)
    q = norm_matmul(x2, g0, w_qkv, tm=tm, out_scale=Q_SCALE, col_block=0, nout=d)
    k = norm_matmul(x2, g0, w_qkv, tm=tm, col_block=1, nout=d)
    vt = norm_matmul(x2, g0, w_qkv[:, 2 * d:].T, tm=tm, transpose_out=True)
    lamv = jnp.stack([lam_q1[0], lam_k1[0], lam_q2[0], lam_k2[0]]).astype(F32)
    lambda_init = 0.8 - 0.6 * math.exp(-0.3 * 0)
    o = diff_attention(q.reshape(b, s, d), k.reshape(b, s, d), vt, bias_tiles, lamv,
                       subln_g[0].reshape(-1, 1).astype(F32), nheads=diff_heads, t=t_attn, lambda_init=lambda_init)
    h = matmul_residual(o.reshape(n, d), w_o_a[0].astype(BF16), x2, tm=tm)
    h = mlp_block(h, row(mlp_norm_g[0]), w_mlp_in_bf, w_mlp_out_bf, row(final_norm_g),
                  layer=0, tm=tm, tf=tf, final_norm=False)

    gkv = row(kv_norm_g)
    w_f_t = jnp.zeros((LANES, d), F32)
    b_f = jnp.zeros((LANES, 1), F32)
    for p in range(C_PIECES):
        w_f_t = w_f_t.at[p * fox_heads:(p + 1) * fox_heads].set(w_f_b.T.astype(F32))
        b_f = b_f.at[p * fox_heads:(p + 1) * fox_heads, 0].set(b_f_b.astype(F32))
    kc, c_cols = forget_scan(h.reshape(b, s, d), gkv, w_f_t, b_f, nheads=fox_heads, ts=ts)
    k = norm_matmul(h, gkv, w_k_b.astype(BF16), tm=tm)
    vt = norm_matmul(h, gkv, w_v_b.astype(BF16).T, tm=tm, transpose_out=True)

    q = norm_matmul(h, row(attn_norm_g[1]), w_q_b[0].astype(BF16), tm=tm, out_scale=Q_SCALE)
    c_end = c_cols[:, t_attn - 1::t_attn, :fox_heads].transpose(0, 2, 1).reshape(-1, FOX_HEADS_PER_STEP, s // t_attn)
    o = fox_attention(q.reshape(b, s, d), k.reshape(b, s, d), vt, kc, c_cols, c_end, nheads=fox_heads, t=t_attn)
    h = matmul_residual(o.reshape(n, d), w_o_b[0].astype(BF16), h, tm=tm)
    h = mlp_block(h, row(mlp_norm_g[1]), w_mlp_in_bf, w_mlp_out_bf, row(final_norm_g),
                  layer=1, tm=tm, tf=tf, final_norm=True)
    return h.reshape(b, s, d)
```
